```python
import math
import jax
import jax.numpy as jnp
from jax import lax
import numpy as np

D_MODEL = 1024
BATCH = 16
SEQ = 2048
DEPTH = 1

MIX_WIDTH = 2 * D_MODEL
SSD_WIDTH = MIX_WIDTH // 2
SSD_HEAD_DIM = 64
SSD_HEADS = SSD_WIDTH // SSD_HEAD_DIM
SSD_GROUPS = 4
SSD_HPG = SSD_HEADS // SSD_GROUPS
SSD_STATE = 128
SSD_CONV = 4
SSD_CHUNK = 128
CONV_CH = SSD_WIDTH + 2 * SSD_GROUPS * SSD_STATE
S5_WIDTH = MIX_WIDTH - SSD_WIDTH
S5_GROUP_CH = 16
S5_GROUPS = S5_WIDTH // S5_GROUP_CH
S5_STATE = 64
DT_MIN = 1e-3
DT_MAX = 1e-1
IN_WIDTH = SSD_WIDTH + CONV_CH + SSD_HEADS + S5_WIDTH
PEER_HEADS = 8
PEER_NKEYS = 128
PEER_EXPERTS = PEER_NKEYS * PEER_NKEYS
PEER_TOPK = 16
PEER_KEY_DIM = 256
PEER_HALF = PEER_KEY_DIM // 2
PEER_TOK_BLOCK = 128
EPS = 1e-6

kernel_name = 'hybrid_ssd_s5_peer_adaln_block'


def rms_norm(x, g):
    xf = x.astype(jnp.float32)
    y = xf * lax.rsqrt(jnp.mean(xf * xf, axis=-1, keepdims=True) + EPS)
    return (y * g.astype(jnp.float32)).astype(x.dtype)


def ssd_mixer(z, xbc, dt_raw, conv_w, conv_b, dt_bias, a_log, d_ssd, norm_ssd_g):
    b, s, _ = xbc.shape
    nc = s // SSD_CHUNK
    f32 = jnp.float32
    xpad = jnp.pad(xbc, ((0, 0), (SSD_CONV - 1, 0), (0, 0)))
    conv = conv_b
    for k in range(SSD_CONV):
        conv = conv + xpad[:, k:k + s, :] * conv_w[k]
    xbc = jax.nn.silu(conv.astype(f32))
    xs, bm, cm = jnp.split(xbc, [SSD_WIDTH, SSD_WIDTH + SSD_GROUPS * SSD_STATE], axis=-1)
    dt = jax.nn.softplus(dt_raw.astype(f32) + dt_bias.astype(f32))
    a = -jnp.exp(a_log.astype(f32))
    xc = xs.reshape(b, nc, SSD_CHUNK, SSD_GROUPS, SSD_HPG, SSD_HEAD_DIM)
    bc = bm.reshape(b, nc, SSD_CHUNK, SSD_GROUPS, SSD_STATE)
    cc = cm.reshape(b, nc, SSD_CHUNK, SSD_GROUPS, SSD_STATE)
    dtc = dt.reshape(b, nc, SSD_CHUNK, SSD_GROUPS, SSD_HPG)
    xdt = xc * dtc[..., None]
    da_cs = jnp.cumsum(dtc * a.reshape(SSD_GROUPS, SSD_HPG), axis=2)
    seg = jnp.moveaxis(da_cs, 2, -1)
    diff = seg[..., :, None] - seg[..., None, :]
    causal = jnp.tril(jnp.ones((SSD_CHUNK, SSD_CHUNK), dtype=bool))
    lmat = jnp.where(causal, jnp.exp(jnp.where(causal, diff, 0.0)), 0.0)
    cb = jnp.einsum('bclgn,bcsgn->bcgls', cc, bc)
    y_diag = jnp.einsum('bcgls,bcgrls,bcsgrp->bclgrp', cb, lmat, xdt)
    decay_states = jnp.exp(da_cs[:, :, -1:] - da_cs)
    states = jnp.einsum('bclgn,bclgr,bclgrp->bcgrpn', bc, decay_states, xdt)
    chunk_decay = jnp.exp(da_cs[:, :, -1])

    def step(carry, inp):
        st, dec = inp
        return carry * dec[..., None, None] + st, carry

    init = jnp.zeros((b, SSD_GROUPS, SSD_HPG, SSD_HEAD_DIM, SSD_STATE), f32)
    _, prev = lax.scan(step, init, (jnp.moveaxis(states, 1, 0), jnp.moveaxis(chunk_decay, 1, 0)))
    prev = jnp.moveaxis(prev, 0, 1)
    y_off = jnp.einsum('bclgn,bcgrpn,bclgr->bclgrp', cc, prev, jnp.exp(da_cs))
    y = y_diag + y_off + d_ssd.astype(f32).reshape(SSD_GROUPS, SSD_HPG, 1) * xc
    y = y.reshape(b, s, SSD_WIDTH) * jax.nn.silu(z.astype(f32))
    yg = y.reshape(b, s, SSD_GROUPS, SSD_WIDTH // SSD_GROUPS)
    yg = yg * lax.rsqrt(jnp.mean(yg * yg, axis=-1, keepdims=True) + EPS)
    return (yg.reshape(b, s, SSD_WIDTH) * norm_ssd_g.astype(f32)).astype(z.dtype)


def s5_mixer(u, a_re, a_im, log_dt, b_re, b_im, c_re, c_im, d_s5, glu_w, glu_b, norm_s5_g):
    b, s, _ = u.shape
    f32 = jnp.float32
    uf = u.astype(f32).reshape(b, s, S5_GROUPS, S5_GROUP_CH)
    lr = a_re.astype(f32)
    li = a_im.astype(f32)
    dt = jnp.exp(log_dt.astype(f32))[:, None]
    mag = jnp.exp(lr * dt)
    lb_re = mag * jnp.cos(li * dt)
    lb_im = mag * jnp.sin(li * dt)
    den = lr * lr + li * li
    coef_re = ((lb_re - 1.0) * lr + lb_im * li) / den
    coef_im = (lb_im * lr - (lb_re - 1.0) * li) / den
    br = b_re.astype(f32)
    bi = b_im.astype(f32)
    bb_re = coef_re[..., None] * br - coef_im[..., None] * bi
    bb_im = coef_re[..., None] * bi + coef_im[..., None] * br
    bu_re = jnp.einsum('bsgh,gph->sbgp', uf, bb_re)
    bu_im = jnp.einsum('bsgh,gph->sbgp', uf, bb_im)
    a_re_t = jnp.broadcast_to(lb_re, (s, 1, S5_GROUPS, S5_STATE))
    a_im_t = jnp.broadcast_to(lb_im, (s, 1, S5_GROUPS, S5_STATE))

    def combine(e1, e2):
        a1r, a1i, b1r, b1i = e1
        a2r, a2i, b2r, b2i = e2
        return (a2r * a1r - a2i * a1i,
                a2r * a1i + a2i * a1r,
                a2r * b1r - a2i * b1i + b2r,
                a2r * b1i + a2i * b1r + b2i)

    _, _, xr, xi = lax.associative_scan(combine, (a_re_t, a_im_t, bu_re, bu_im), axis=0)
    y = (jnp.einsum('sbgp,ghp->bsgh', xr, c_re.astype(f32))
         - jnp.einsum('sbgp,ghp->bsgh', xi, c_im.astype(f32))
         + d_s5.astype(f32) * uf)
    v = jax.nn.gelu(y, approximate=False)
    out = v * jax.nn.sigmoid(jnp.einsum('bsgh,ghk->bsgk', v, glu_w.astype(f32)) + glu_b.astype(f32))
    out = out.reshape(b, s, S5_WIDTH)
    return rms_norm(out, norm_s5_g).astype(u.dtype)


def peer_ffn(h, w_query, sub_keys, expert_u, expert_v):
    b, s, d = h.shape
    f32 = jnp.float32
    q = (h @ w_query).reshape(b, s, PEER_HEADS, 2, PEER_HALF)
    scores = jnp.einsum('bshcd,hckd->bshck', q, sub_keys).astype(f32)
    s1, i1 = lax.top_k(scores[..., 0, :], PEER_TOPK)
    s2, i2 = lax.top_k(scores[..., 1, :], PEER_TOPK)
    cand = (s1[..., :, None] + s2[..., None, :]).reshape(b, s, PEER_HEADS, PEER_TOPK * PEER_TOPK)
    cand_id = (i1[..., :, None] * PEER_NKEYS + i2[..., None, :]).reshape(b, s, PEER_HEADS, PEER_TOPK * PEER_TOPK)
    top_s, pos = lax.top_k(cand, PEER_TOPK)
    eid = jnp.take_along_axis(cand_id, pos, axis=-1)
    gates = jax.nn.softmax(top_s, axis=-1)
    n_blocks = (b * s) // PEER_TOK_BLOCK
    hb = h.reshape(n_blocks, PEER_TOK_BLOCK, d)
    eb = eid.reshape(n_blocks, PEER_TOK_BLOCK, PEER_HEADS * PEER_TOPK)
    gb = gates.reshape(n_blocks, PEER_TOK_BLOCK, PEER_HEADS * PEER_TOPK)

    def block(args):
        hx, ex, gx = args
        u = jnp.take(expert_u, ex, axis=0)
        v = jnp.take(expert_v, ex, axis=0)
        act = jax.nn.gelu(jnp.einsum('td,ted->te', hx, u).astype(f32), approximate=False)
        return jnp.einsum('te,ted->td', (gx * act).astype(h.dtype), v).astype(h.dtype)

    out = lax.map(block, (hb, eb, gb))
    return out.reshape(b, s, d)


def setup_inputs(seed: int = 0) -> dict:
    key = jax.random.key(seed)
    ks = jax.random.split(key, 32)
    L = DEPTH
    f32 = jnp.float32

    def nrm(k, shape, scale):
        return jax.random.normal(k, shape, f32) * scale

    x = nrm(ks[0], (BATCH, SEQ, D_MODEL), 1.0)
    c = nrm(ks[1], (BATCH, D_MODEL), 1.0)
    w_ada = nrm(ks[2], (L, D_MODEL, 6 * D_MODEL), D_MODEL ** -0.5)
    b_ada = nrm(ks[3], (L, 6 * D_MODEL), 0.02)
    norm1_g = 1.0 + nrm(ks[4], (L, D_MODEL), 0.02)
    w_in = nrm(ks[5], (L, D_MODEL, IN_WIDTH), D_MODEL ** -0.5)
    conv_w = nrm(ks[6], (L, SSD_CONV, CONV_CH), SSD_CONV ** -0.5)
    conv_b = nrm(ks[7], (L, CONV_CH), 0.02)
    dt0 = jnp.exp(jax.random.uniform(ks[8], (L, SSD_HEADS), f32, math.log(DT_MIN), math.log(DT_MAX)))
    dt_bias = dt0 + jnp.log(-jnp.expm1(-dt0))
    a_log = jnp.log(jax.random.uniform(ks[9], (L, SSD_HEADS), f32, 1.0, 16.0))
    d_ssd = 1.0 + nrm(ks[10], (L, SSD_HEADS), 0.02)
    norm_ssd_g = 1.0 + nrm(ks[11], (L, SSD_WIDTH), 0.02)
    s5_a_re = -0.5 + nrm(ks[12], (L, S5_GROUPS, S5_STATE), 0.01)
    s5_a_im = jnp.pi * jnp.arange(S5_STATE, dtype=f32) + nrm(ks[13], (L, S5_GROUPS, S5_STATE), 0.01)
    s5_log_dt = jax.random.uniform(ks[14], (L, S5_GROUPS), f32, math.log(DT_MIN), math.log(DT_MAX))
    s5_b_re = nrm(ks[15], (L, S5_GROUPS, S5_STATE, S5_GROUP_CH), (2 * S5_GROUP_CH) ** -0.5)
    s5_b_im = nrm(ks[16], (L, S5_GROUPS, S5_STATE, S5_GROUP_CH), (2 * S5_GROUP_CH) ** -0.5)
    s5_c_re = nrm(ks[17], (L, S5_GROUPS, S5_GROUP_CH, S5_STATE), (2 * S5_STATE) ** -0.5)
    s5_c_im = nrm(ks[18], (L, S5_GROUPS, S5_GROUP_CH, S5_STATE), (2 * S5_STATE) ** -0.5)
    s5_d = nrm(ks[19], (L, S5_GROUPS, S5_GROUP_CH), 1.0)
    glu_w = nrm(ks[20], (L, S5_GROUPS, S5_GROUP_CH, S5_GROUP_CH), S5_GROUP_CH ** -0.5)
    glu_b = nrm(ks[21], (L, S5_GROUPS, S5_GROUP_CH), 0.02)
    norm_s5_g = 1.0 + nrm(ks[22], (L, S5_WIDTH), 0.02)
    w_out = nrm(ks[23], (L, MIX_WIDTH, D_MODEL), MIX_WIDTH ** -0.5)
    norm2_g = 1.0 + nrm(ks[24], (L, D_MODEL), 0.02)
    w_query = nrm(ks[25], (L, D_MODEL, PEER_HEADS * PEER_KEY_DIM), D_MODEL ** -0.5)
    sub_keys = nrm(ks[26], (L, PEER_HEADS, 2, PEER_NKEYS, PEER_HALF), PEER_HALF ** -0.5)
    expert_u = nrm(ks[27], (L, PEER_EXPERTS, D_MODEL), D_MODEL ** -0.5)
    expert_v = nrm(ks[28], (L, PEER_EXPERTS, D_MODEL), PEER_HEADS ** -0.5)
    norm_f_g = 1.0 + nrm(ks[29], (D_MODEL,), 0.02)
    return {'x': x, 'c': c, 'w_ada': w_ada, 'b_ada': b_ada, 'norm1_g': norm1_g, 'w_in': w_in,
            'conv_w': conv_w, 'conv_b': conv_b, 'dt_bias': dt_bias, 'a_log': a_log, 'd_ssd': d_ssd,
            'norm_ssd_g': norm_ssd_g, 's5_a_re': s5_a_re, 's5_a_im': s5_a_im, 's5_log_dt': s5_log_dt,
            's5_b_re': s5_b_re, 's5_b_im': s5_b_im, 's5_c_re': s5_c_re, 's5_c_im': s5_c_im, 's5_d': s5_d,
            'glu_w': glu_w, 'glu_b': glu_b, 'norm_s5_g': norm_s5_g, 'w_out': w_out, 'norm2_g': norm2_g,
            'w_query': w_query, 'sub_keys': sub_keys, 'expert_u': expert_u, 'expert_v': expert_v,
            'norm_f_g': norm_f_g}


def reference(x, c, w_ada, b_ada, norm1_g, w_in, conv_w, conv_b, dt_bias, a_log, d_ssd, norm_ssd_g,
              s5_a_re, s5_a_im, s5_log_dt, s5_b_re, s5_b_im, s5_c_re, s5_c_im, s5_d, glu_w, glu_b,
              norm_s5_g, w_out, norm2_g, w_query, sub_keys, expert_u, expert_v, norm_f_g):
    for l in range(DEPTH):
        mod = jax.nn.silu(c) @ w_ada[l] + b_ada[l]
        shift1, scale1, gate1, shift2, scale2, gate2 = jnp.split(mod[:, None, :], 6, axis=-1)
        h = rms_norm(x, norm1_g[l]) * (1.0 + scale1) + shift1
        proj = h @ w_in[l]
        z, xbc, dt_raw, u = jnp.split(
            proj, [SSD_WIDTH, SSD_WIDTH + CONV_CH, SSD_WIDTH + CONV_CH + SSD_HEADS], axis=-1)
        y_ssd = ssd_mixer(z, xbc, dt_raw, conv_w[l], conv_b[l], dt_bias[l], a_log[l], d_ssd[l], norm_ssd_g[l])
        y_s5 = s5_mixer(u, s5_a_re[l], s5_a_im[l], s5_log_dt[l], s5_b_re[l], s5_b_im[l], s5_c_re[l],
                        s5_c_im[l], s5_d[l], glu_w[l], glu_b[l], norm_s5_g[l])
        mix = jnp.concatenate([y_ssd, y_s5], axis=-1) @ w_out[l]
        x = x + gate1 * mix
        h = rms_norm(x, norm2_g[l]) * (1.0 + scale2) + shift2
        x = x + gate2 * peer_ffn(h, w_query[l], sub_keys[l], expert_u[l], expert_v[l])
    return rms_norm(x, norm_f_g)
```

```python
import functools

import jax
import jax.numpy as jnp
from jax import lax
from jax.experimental import pallas as pl
from jax.experimental.pallas import tpu as pltpu

F32 = jnp.float32
BF16 = jnp.bfloat16
I32 = jnp.int32

D_MODEL = 1024
SSD_WIDTH = 1024
SSD_HEAD_DIM = 64
SSD_HEADS = 16
SSD_GROUPS = 4
SSD_STATE = 128
SSD_CONV = 4
SSD_CHUNK = 128
S5_WIDTH = 1024
S5_GROUP_CH = 16
S5_GROUPS = 64
S5_STATE = 64
S5_CHUNK = 16
PEER_HEADS = 8
PEER_NKEYS = 128
PEER_TOPK = 16
PEER_HALF = 128
PEER_SLOTS = PEER_HEADS * PEER_TOPK
EPS = 1e-6
LANES = 128
DT_PAD = LANES
PROJ_WIDTH = SSD_WIDTH + 2 * 1024 + S5_WIDTH + DT_PAD
COL_Z, COL_XS, COL_BC, COL_U = 0, 1, 2, 3
COL_DT = (4 * 1024) // DT_PAD
HIGHEST = lax.Precision.HIGHEST
VMEM_LIMIT = 56 * 1024 * 1024


def _silu(v):
    return v * jax.nn.sigmoid(v)


def _gelu_exact(v):
    return 0.5 * v * (1.0 + lax.erf(v * (2.0 ** -0.5)))


def _rms(v, g):
    return v * lax.rsqrt(jnp.mean(v * v, axis=-1, keepdims=True) + EPS) * g


def _mod_body(c_ref, w_ref, b_ref, o_ref):
    o_ref[...] = jnp.dot(_silu(c_ref[...]), w_ref[...], preferred_element_type=F32) + b_ref[...]


def _adaln_mod(c, w_ada, b_ada):
    bsz, d = c.shape
    n = w_ada.shape[1]
    tn = 1024
    return pl.pallas_call(
        _mod_body,
        grid=(n // tn,),
        in_specs=[pl.BlockSpec((bsz, d), lambda j: (0, 0)),
                  pl.BlockSpec((d, tn), lambda j: (0, j)),
                  pl.BlockSpec((1, tn), lambda j: (0, j))],
        out_specs=pl.BlockSpec((bsz, tn), lambda j: (0, j)),
        out_shape=jax.ShapeDtypeStruct((bsz, n), F32),
        name="adaln_mod",
    )(c, w_ada, b_ada.reshape(1, n))


def _inproj_body(x_ref, sc_ref, sh_ref, g_ref, w_ref, o_ref):
    h = _rms(x_ref[...], g_ref[...]) * (1.0 + sc_ref[0]) + sh_ref[0]
    o_ref[...] = jnp.dot(h.astype(BF16), w_ref[...], preferred_element_type=F32)


def _in_proj(x2d, scale1, shift1, g1, w_cat, seq):
    t, d = x2d.shape
    n = w_cat.shape[1]
    tm = 256
    per_b = seq // tm
    return pl.pallas_call(
        _inproj_body,
        grid=(t // tm,),
        in_specs=[pl.BlockSpec((tm, d), lambda i: (i, 0)),
                  pl.BlockSpec((1, 1, d), lambda i: (i // per_b, 0, 0)),
                  pl.BlockSpec((1, 1, d), lambda i: (i // per_b, 0, 0)),
                  pl.BlockSpec((1, d), lambda i: (0, 0)),
                  pl.BlockSpec((d, n), lambda i: (0, 0))],
        out_specs=pl.BlockSpec((tm, n), lambda i: (i, 0)),
        out_shape=jax.ShapeDtypeStruct((t, n), F32),
        compiler_params=pltpu.CompilerParams(dimension_semantics=("parallel",),
                                             vmem_limit_bytes=VMEM_LIMIT),
        name="in_proj",
    )(x2d, scale1, shift1, g1, w_cat)


HALO = 8
SSD_PAIRS = SSD_HEADS // 2


def _ssd_body(z_ref, xs_ref, bc_ref, dt_ref, cw_ref, cb_ref, dtb_ref, a_ref, d_ref, ng_ref,
              o_ref, win_ref, act_ref, state_ref, y_ref):
    c = pl.program_id(1)
    L = SSD_CHUNK

    @pl.when(c == 0)
    def _():
        win_ref[0:HALO, :] = jnp.zeros((HALO, 2048), F32)
        state_ref[...] = jnp.zeros(state_ref.shape, F32)

    @pl.when(c > 0)
    def _():
        win_ref[0:HALO, :] = win_ref[L:L + HALO, :]

    win_ref[HALO:HALO + L, 0:1024] = xs_ref[...]
    win_ref[HALO:HALO + L, 1024:2048] = bc_ref[...]

    for sl in range(8):
        cols = slice(sl * 256, (sl + 1) * 256)
        acc = jnp.broadcast_to(cb_ref[:, cols], (L, 256))
        for k in range(SSD_CONV):
            r0 = HALO - (SSD_CONV - 1) + k
            acc = acc + win_ref[r0:r0 + L, cols] * cw_ref[k:k + 1, cols]
        act_ref[:, cols] = _silu(acc)

    dtv = dt_ref[...] + dtb_ref[...]
    dt = jnp.maximum(dtv, 0.0) + jnp.log1p(jnp.exp(-jnp.abs(dtv)))
    da = dt * a_ref[...]
    row = lax.broadcasted_iota(I32, (L, L), 0)
    col = lax.broadcasted_iota(I32, (L, L), 1)
    causal = row >= col
    tril = jnp.where(causal, 1.0, 0.0).astype(F32)
    cs = jnp.dot(tril, da, precision=HIGHEST, preferred_element_type=F32)
    cs_t = cs.T
    dec_t = jnp.exp(cs_t[:, L - 1:L] - cs_t)
    ecs = jnp.exp(cs)
    lane = lax.broadcasted_iota(I32, (L, LANES), 1)
    first = lane < SSD_HEAD_DIM

    for g in range(SSD_GROUPS):
        bg = act_ref[:, 1024 + g * 128:1024 + (g + 1) * 128]
        cg = act_ref[:, 1536 + g * 128:1536 + (g + 1) * 128]
        bg_t = bg.T
        cg_b = cg.astype(BF16)
        cb = jnp.dot(cg_b, bg_t.astype(BF16), preferred_element_type=F32)
        for r in range(2):
            p = g * 2 + r
            h0, h1 = 2 * p, 2 * p + 1
            x2 = act_ref[:, p * 128:(p + 1) * 128]
            dt2 = jnp.where(first, dt[:, h0:h0 + 1], dt[:, h1:h1 + 1])
            xdt = (x2 * dt2).astype(BF16)
            ydiag, snew = [], []
            for h in (h0, h1):
                diff = cs[:, h:h + 1] - cs_t[h:h + 1, :]
                lmat = jnp.where(causal, jnp.exp(jnp.where(causal, diff, 0.0)), 0.0)
                ydiag.append(jnp.dot((cb * lmat).astype(BF16), xdt, preferred_element_type=F32))
                bw = (bg_t * dec_t[h:h + 1, :]).astype(BF16)
                snew.append(jnp.dot(bw, xdt, preferred_element_type=F32))
            prev = state_ref[p]
            yoff = jnp.dot(cg_b, prev.astype(BF16), preferred_element_type=F32)
            yoff = yoff * jnp.where(first, ecs[:, h0:h0 + 1], ecs[:, h1:h1 + 1])
            cdec = jnp.where(first[0:1, :], ecs[L - 1:L, h0:h0 + 1], ecs[L - 1:L, h1:h1 + 1])
            state_ref[p] = prev * cdec + jnp.where(first, snew[0], snew[1])
            y2 = jnp.where(first, ydiag[0], ydiag[1]) + yoff + d_ref[:, p * 128:(p + 1) * 128] * x2
            y_ref[:, p * 128:(p + 1) * 128] = y2

    gw = SSD_WIDTH // SSD_GROUPS
    for g in range(SSD_GROUPS):
        cols = slice(g * gw, (g + 1) * gw)
        yg = y_ref[:, cols] * _silu(z_ref[:, cols])
        o_ref[:, cols] = _rms(yg, ng_ref[:, cols])


def _ssd_mixer(proj, conv_w, conv_b, dt_bias_p, a_p, d_exp, norm_g, bsz, seq):
    t = proj.shape[0]
    L = SSD_CHUNK
    nc = seq // L
    row = lambda b, c: b * nc + c
    full = lambda shape: pl.BlockSpec(shape, lambda b, c: (0, 0))
    return pl.pallas_call(
        _ssd_body,
        grid=(bsz, nc),
        in_specs=[pl.BlockSpec((L, 1024), lambda b, c: (row(b, c), COL_Z)),
                  pl.BlockSpec((L, 1024), lambda b, c: (row(b, c), COL_XS)),
                  pl.BlockSpec((L, 1024), lambda b, c: (row(b, c), COL_BC)),
                  pl.BlockSpec((L, DT_PAD), lambda b, c: (row(b, c), COL_DT)),
                  full((SSD_CONV, 2048)), full((1, 2048)), full((1, DT_PAD)), full((1, DT_PAD)),
                  full((1, 1024)), full((1, 1024))],
        out_specs=pl.BlockSpec((L, 1024), lambda b, c: (row(b, c), 0)),
        out_shape=jax.ShapeDtypeStruct((t, SSD_WIDTH), F32),
        scratch_shapes=[pltpu.VMEM((L + HALO, 2048), F32),
                        pltpu.VMEM((L, 2048), F32),
                        pltpu.VMEM((SSD_PAIRS, SSD_STATE, LANES), F32),
                        pltpu.VMEM((L, 1024), F32)],
        compiler_params=pltpu.CompilerParams(dimension_semantics=("parallel", "arbitrary"),
                                             vmem_limit_bytes=VMEM_LIMIT),
        name="ssd_mixer",
    )(proj, proj, proj, proj, conv_w, conv_b, dt_bias_p, a_p, d_exp, norm_g)


def _s5_prepare(a_re, a_im, log_dt, b_re, b_im, c_re, c_im, d_s5, glu_w, glu_b):
    G, P, H, C = S5_GROUPS, S5_STATE, S5_GROUP_CH, S5_CHUNK
    lr, li = a_re.astype(F32), a_im.astype(F32)
    dt = jnp.exp(log_dt.astype(F32))[:, None]
    mag = jnp.exp(lr * dt)
    lb_re, lb_im = mag * jnp.cos(li * dt), mag * jnp.sin(li * dt)
    den = lr * lr + li * li
    coef_re = ((lb_re - 1.0) * lr + lb_im * li) / den
    coef_im = (lb_im * lr - (lb_re - 1.0) * li) / den
    bb_re = coef_re[..., None] * b_re - coef_im[..., None] * b_im
    bb_im = coef_re[..., None] * b_im + coef_im[..., None] * b_re
    tau = jnp.arange(C + 1, dtype=F32)[:, None, None]
    pm = jnp.exp(lr * dt * tau)
    pw_re, pw_im = pm * jnp.cos(li * dt * tau), pm * jnp.sin(li * dt * tau)
    m_re = pw_re[..., None] * bb_re - pw_im[..., None] * bb_im
    m_im = pw_re[..., None] * bb_im + pw_im[..., None] * bb_re
    e_re = c_re * pw_re[:, :, None, :] - c_im * pw_im[:, :, None, :]
    e_im = c_re * pw_im[:, :, None, :] + c_im * pw_re[:, :, None, :]
    kern = (jnp.einsum('ghp,tgpk->tghk', c_re, m_re[:C], precision=HIGHEST)
            - jnp.einsum('ghp,tgpk->tghk', c_im, m_im[:C], precision=HIGHEST))
    j_in = jnp.arange(C)[:, None]
    j_out = jnp.arange(C)[None, :]
    lag = j_out - j_in
    toep = jnp.where((lag >= 0)[:, :, None, None, None], kern[jnp.clip(lag, 0, C - 1)], 0.0)
    toep = toep.transpose(2, 0, 4, 1, 3).reshape(G, C * H, C * H)
    rev = jnp.arange(C - 1, -1, -1)
    wst_re = m_re[rev].transpose(1, 0, 3, 2).reshape(G, C * H, P)
    wst_im = m_im[rev].transpose(1, 0, 3, 2).reshape(G, C * H, P)
    wout_re = e_re[1:].transpose(1, 3, 0, 2).reshape(G, P, C * H)
    wout_im = (-e_im[1:]).transpose(1, 3, 0, 2).reshape(G, P, C * H)
    lam_re, lam_im = pw_re[C][:, None, :], pw_im[C][:, None, :]
    dvec = jnp.tile(d_s5.astype(F32), (1, C))[:, None, :]
    eye = jnp.eye(C, dtype=F32)
    wglu = (eye[None, :, None, :, None] * glu_w.astype(F32)[:, None, :, None, :]).reshape(G, C * H, C * H)
    bglu = jnp.tile(glu_b.astype(F32), (1, C))[:, None, :]
    return (toep.astype(BF16), wst_re.astype(BF16), wst_im.astype(BF16), lam_re, lam_im,
            wout_re.astype(BF16), wout_im.astype(BF16), dvec, wglu.astype(BF16), bglu)


def _s5_body(nb, x_ref, toep_ref, wsr_ref, wsi_ref, lr_ref, li_ref, wor_ref, woi_ref, d_ref,
             wg_ref, bg_ref, o_ref, sr_ref, si_ref, pr_ref, pi_ref):
    x = x_ref[0]
    xb = x.astype(BF16)
    sr_ref[...] = jnp.dot(xb, wsr_ref[0], preferred_element_type=F32)
    si_ref[...] = jnp.dot(xb, wsi_ref[0], preferred_element_type=F32)
    lam_r = jnp.broadcast_to(lr_ref[0], (nb, S5_STATE))
    lam_i = jnp.broadcast_to(li_ref[0], (nb, S5_STATE))
    n_chunks = x.shape[0] // nb

    def step(c, carry):
        xr, xi = carry
        rows = pl.ds(pl.multiple_of(c * nb, nb), nb)
        pr_ref[rows, :] = xr
        pi_ref[rows, :] = xi
        nr = lam_r * xr - lam_i * xi + sr_ref[rows, :]
        ni = lam_r * xi + lam_i * xr + si_ref[rows, :]
        return nr, ni

    zero = jnp.zeros((nb, S5_STATE), F32)
    lax.fori_loop(0, n_chunks, step, (zero, zero))
    y = (jnp.dot(xb, toep_ref[0], preferred_element_type=F32)
         + jnp.dot(pr_ref[...].astype(BF16), wor_ref[0], preferred_element_type=F32)
         + jnp.dot(pi_ref[...].astype(BF16), woi_ref[0], preferred_element_type=F32)
         + d_ref[0] * x)
    v = _gelu_exact(y)
    gate = jax.nn.sigmoid(jnp.dot(v.astype(BF16), wg_ref[0], preferred_element_type=F32) + bg_ref[0])
    o_ref[0] = v * gate


def _s5_mixer(xg, prep, nb):
    G, R, W = xg.shape
    toep, wsr, wsi, lam_re, lam_im, wor, woi, dvec, wglu, bglu = prep
    P = S5_STATE
    per_g = lambda shape: pl.BlockSpec((1,) + shape, lambda g: (g, 0, 0))
    return pl.pallas_call(
        functools.partial(_s5_body, nb),
        grid=(G,),
        in_specs=[per_g((R, W)), per_g((W, W)), per_g((W, P)), per_g((W, P)), per_g((1, P)), per_g((1, P)),
                  per_g((P, W)), per_g((P, W)), per_g((1, W)), per_g((W, W)), per_g((1, W))],
        out_specs=per_g((R, W)),
        out_shape=jax.ShapeDtypeStruct((G, R, W), F32),
        scratch_shapes=[pltpu.VMEM((R, P), F32)] * 4,
        compiler_params=pltpu.CompilerParams(dimension_semantics=("parallel",),
                                             vmem_limit_bytes=VMEM_LIMIT),
        name="s5_mixer",
    )(xg, toep, wsr, wsi, lam_re, lam_im, wor, woi, dvec, wglu, bglu)


def _outproj_body(ys_ref, s5_ref, x_ref, g1_ref, sc2_ref, sh2_ref, ns5_ref, n2_ref, wo_ref, wq_ref,
                  x1_ref, h2_ref, q_ref):
    a = ys_ref[...].astype(BF16)
    b = _rms(s5_ref[...], ns5_ref[...]).astype(BF16)
    mix = (jnp.dot(a, wo_ref[0:SSD_WIDTH, :], preferred_element_type=F32)
           + jnp.dot(b, wo_ref[SSD_WIDTH:, :], preferred_element_type=F32))
    x1 = x_ref[...] + g1_ref[0] * mix
    x1_ref[...] = x1
    h2 = _rms(x1, n2_ref[...]) * (1.0 + sc2_ref[0]) + sh2_ref[0]
    h2_ref[...] = h2
    q = jnp.dot(h2.astype(BF16), wq_ref[...], preferred_element_type=F32)
    for hc in range(2 * PEER_HEADS):
        q_ref[hc] = q[:, hc * PEER_HALF:(hc + 1) * PEER_HALF]


def _out_proj(yssd, s5o, x2d, gate1, scale2, shift2, ns5, n2, w_out, w_query, seq):
    t, d = x2d.shape
    nhc = w_query.shape[1] // PEER_HALF
    tm = 256
    per_b = seq // tm
    rowblk = lambda w: pl.BlockSpec((tm, w), lambda i: (i, 0))
    modblk = pl.BlockSpec((1, 1, d), lambda i: (i // per_b, 0, 0))
    full = lambda shape: pl.BlockSpec(shape, lambda i: (0, 0))
    return pl.pallas_call(
        _outproj_body,
        grid=(t // tm,),
        in_specs=[rowblk(d), rowblk(d), rowblk(d), modblk, modblk, modblk, full((1, d)), full((1, d)),
                  full(w_out.shape), full(w_query.shape)],
        out_specs=[rowblk(d), rowblk(d), pl.BlockSpec((nhc, tm, PEER_HALF), lambda i: (0, i, 0))],
        out_shape=[jax.ShapeDtypeStruct((t, d), F32), jax.ShapeDtypeStruct((t, d), F32),
                   jax.ShapeDtypeStruct((nhc, t, PEER_HALF), F32)],
        compiler_params=pltpu.CompilerParams(dimension_semantics=("parallel",),
                                             vmem_limit_bytes=VMEM_LIMIT),
        name="out_proj_query",
    )(yssd, s5o, x2d, gate1, scale2, shift2, ns5, n2, w_out, w_query)


def _topk_rows(s, n_out):
    n = s.shape[0]
    iota = lax.broadcasted_iota(I32, s.shape, 0)
    vals, idxs = [], []
    for _ in range(n_out):
        m = jnp.max(s, axis=0, keepdims=True)
        am = jnp.min(jnp.where(s == m, iota, n), axis=0, keepdims=True)
        vals.append(m)
        idxs.append(am)
        s = jnp.where(iota == am, -jnp.inf, s)
    return jnp.concatenate(vals, axis=0), jnp.concatenate(idxs, axis=0)


def _route_body(q_ref, keys_ref, eid_ref, gate_ref, ts_ref, ti_ref):
    K = PEER_TOPK

    def half(hc, _):
        st = lax.dot_general(keys_ref[hc], q_ref[hc].astype(BF16), (((1,), (1,)), ((), ())),
                             preferred_element_type=F32)
        v, i = _topk_rows(st, K)
        ts_ref[hc] = v
        ti_ref[hc] = i
        return 0

    lax.fori_loop(0, 2 * PEER_HEADS, half, 0)

    def head(h, _):
        s1, i1 = ts_ref[2 * h], ti_ref[2 * h]
        s2, i2 = ts_ref[2 * h + 1], ti_ref[2 * h + 1]
        cand = jnp.concatenate([s1[a:a + 1, :] + s2 for a in range(K)], axis=0)
        cid = jnp.concatenate([i1[a:a + 1, :] * PEER_NKEYS + i2 for a in range(K)], axis=0)
        top_s, pos = _topk_rows(cand, K)
        iota = lax.broadcasted_iota(I32, cand.shape, 0)
        ids = [jnp.max(jnp.where(iota == pos[k:k + 1, :], cid, -1), axis=0, keepdims=True) for k in range(K)]
        e = jnp.exp(top_s - top_s[0:1, :])
        rows = pl.ds(pl.multiple_of(h * K, K), K)
        gate_ref[rows, :] = e / jnp.sum(e, axis=0, keepdims=True)
        eid_ref[rows, :] = jnp.concatenate(ids, axis=0)
        return 0

    lax.fori_loop(0, PEER_HEADS, head, 0)


def _peer_route(q, keys):
    nhc, t, dh = q.shape
    tm = 128
    return pl.pallas_call(
        _route_body,
        grid=(t // tm,),
        in_specs=[pl.BlockSpec((nhc, tm, dh), lambda i: (0, i, 0)),
                  pl.BlockSpec(keys.shape, lambda i: (0, 0, 0))],
        out_specs=[pl.BlockSpec((PEER_SLOTS, tm), lambda i: (0, i)),
                   pl.BlockSpec((PEER_SLOTS, tm), lambda i: (0, i))],
        out_shape=[jax.ShapeDtypeStruct((PEER_SLOTS, t), I32), jax.ShapeDtypeStruct((PEER_SLOTS, t), F32)],
        scratch_shapes=[pltpu.VMEM((2 * PEER_HEADS, PEER_TOPK, tm), F32),
                        pltpu.VMEM((2 * PEER_HEADS, PEER_TOPK, tm), I32)],
        compiler_params=pltpu.CompilerParams(dimension_semantics=("parallel",)),
        name="peer_route",
    )(q, keys)


PEER_TOK_TILE = 128
PEER_SUB = 8


def _peer_body(eid_ref, gate_ref, h_ref, x1_ref, g2_ref, nf_ref, uv_ref, o_ref, buf_ref, po_ref, sem):
    n_sub = PEER_TOK_TILE // PEER_SUB
    lane = lax.broadcasted_iota(I32, (PEER_SLOTS, LANES), 1)

    def issue(sub, slot):
        def tok(tl, _):
            t = sub * PEER_SUB + tl

            def row(k, _):
                e = eid_ref[t, k]
                pltpu.make_async_copy(uv_ref.at[pl.ds(e, 1), :], buf_ref.at[slot, tl, pl.ds(k, 1), :],
                                      sem.at[slot]).start()
                return 0

            return lax.fori_loop(0, PEER_SLOTS, row, 0, unroll=8)

        lax.fori_loop(0, PEER_SUB, tok, 0)

    def wait(slot):
        def tok(tl, _):
            pltpu.make_async_copy(uv_ref.at[pl.ds(0, PEER_SLOTS), :], buf_ref.at[slot, tl], sem.at[slot]).wait()
            return 0

        lax.fori_loop(0, PEER_SUB, tok, 0)

    issue(0, 0)

    def sub_step(sub, act):
        slot = sub % 2

        @pl.when(sub + 1 < n_sub)
        def _():
            issue(sub + 1, 1 - slot)

        wait(slot)

        def dots(tl, a):
            t = sub * PEER_SUB + tl
            u = buf_ref[slot, tl, :, 0:D_MODEL]
            col = jnp.sum(u * h_ref[pl.ds(t, 1), :], axis=1, keepdims=True)
            return jnp.where(lane == t, col, a)

        act = lax.fori_loop(0, PEER_SUB, dots, act)
        w = gate_ref[...] * _gelu_exact(act)

        def outs(tl, _):
            t = sub * PEER_SUB + tl
            wcol = jnp.sum(jnp.where(lane == t, w, 0.0), axis=1, keepdims=True)
            v = buf_ref[slot, tl, :, D_MODEL:2 * D_MODEL]
            po_ref[pl.ds(t, 1), :] = jnp.sum(v * wcol, axis=0, keepdims=True)
            return 0

        lax.fori_loop(0, PEER_SUB, outs, 0)
        return act

    lax.fori_loop(0, n_sub, sub_step, jnp.zeros((PEER_SLOTS, LANES), F32))
    x2 = x1_ref[...] + g2_ref[0] * po_ref[...]
    o_ref[...] = _rms(x2, nf_ref[...])


def _peer_experts(eid, gates_t, h2, x1, gate2, nf, uv, seq):
    t, d = h2.shape
    tt = PEER_TOK_TILE
    per_b = seq // tt
    return pl.pallas_call(
        _peer_body,
        grid=(t // tt,),
        in_specs=[pl.BlockSpec((tt, PEER_SLOTS), lambda i: (i, 0), memory_space=pltpu.SMEM),
                  pl.BlockSpec((PEER_SLOTS, tt), lambda i: (0, i)),
                  pl.BlockSpec((tt, d), lambda i: (i, 0)),
                  pl.BlockSpec((tt, d), lambda i: (i, 0)),
                  pl.BlockSpec((1, 1, d), lambda i: (i // per_b, 0, 0)),
                  pl.BlockSpec((1, d), lambda i: (0, 0)),
                  pl.BlockSpec(memory_space=pl.ANY)],
        out_specs=pl.BlockSpec((tt, d), lambda i: (i, 0)),
        out_shape=jax.ShapeDtypeStruct((t, d), F32),
        scratch_shapes=[pltpu.VMEM((2, PEER_SUB, PEER_SLOTS, 2 * D_MODEL), F32),
                        pltpu.VMEM((tt, d), F32),
                        pltpu.SemaphoreType.DMA((2,))],
        compiler_params=pltpu.CompilerParams(dimension_semantics=("arbitrary",),
                                             vmem_limit_bytes=VMEM_LIMIT),
        name="peer_experts",
    )(eid, gates_t, h2, x1, gate2, nf, uv)


def kernel(x, c, w_ada, b_ada, norm1_g, w_in, conv_w, conv_b, dt_bias, a_log, d_ssd, norm_ssd_g, s5_a_re, s5_a_im, s5_log_dt, s5_b_re, s5_b_im, s5_c_re, s5_c_im, s5_d, glu_w, glu_b, norm_s5_g, w_out, norm2_g, w_query, sub_keys, expert_u, expert_v, norm_f_g):
    bsz, seq, d = x.shape
    t = bsz * seq
    depth = w_ada.shape[0]
    xt = x.reshape(t, d)
    for l in range(depth):
        mod = _adaln_mod(c, w_ada[l], b_ada[l])
        shift1, scale1, gate1, shift2, scale2, gate2 = [m.reshape(bsz, 1, d) for m in jnp.split(mod, 6, axis=-1)]

        wz, wxbc, wdt, wu = jnp.split(w_in[l], [SSD_WIDTH, SSD_WIDTH + 2048, SSD_WIDTH + 2048 + SSD_HEADS], axis=1)
        w_cat = jnp.concatenate([wz, wxbc, wu, jnp.pad(wdt, ((0, 0), (0, DT_PAD - SSD_HEADS)))], axis=1).astype(BF16)
        proj = _in_proj(xt, scale1, shift1, norm1_g[l].reshape(1, d), w_cat, seq)

        pad_h = lambda v: jnp.pad(v.astype(F32), (0, DT_PAD - SSD_HEADS)).reshape(1, DT_PAD)
        yssd = _ssd_mixer(proj, conv_w[l], conv_b[l].reshape(1, -1), pad_h(dt_bias[l]),
                          pad_h(-jnp.exp(a_log[l].astype(F32))),
                          jnp.repeat(d_ssd[l].astype(F32), SSD_HEAD_DIM).reshape(1, SSD_WIDTH),
                          norm_ssd_g[l].reshape(1, SSD_WIDTH), bsz, seq)

        nc5 = seq // S5_CHUNK
        u = proj[:, COL_U * 1024:(COL_U + 1) * 1024].reshape(bsz, nc5, S5_CHUNK, S5_GROUPS, S5_GROUP_CH)
        xg = u.transpose(3, 1, 0, 2, 4).reshape(S5_GROUPS, nc5 * bsz, S5_CHUNK * S5_GROUP_CH)
        prep = _s5_prepare(s5_a_re[l], s5_a_im[l], s5_log_dt[l], s5_b_re[l], s5_b_im[l], s5_c_re[l],
                           s5_c_im[l], s5_d[l], glu_w[l], glu_b[l])
        og = _s5_mixer(xg, prep, bsz)
        s5o = og.reshape(S5_GROUPS, nc5, bsz, S5_CHUNK, S5_GROUP_CH).transpose(2, 1, 3, 0, 4).reshape(t, S5_WIDTH)

        x1, h2, q = _out_proj(yssd, s5o, xt, gate1, scale2, shift2, norm_s5_g[l].reshape(1, -1),
                              norm2_g[l].reshape(1, d), w_out[l].astype(BF16), w_query[l].astype(BF16), seq)

        keys = sub_keys[l].reshape(2 * PEER_HEADS, PEER_NKEYS, PEER_HALF).astype(BF16)
        eid_t, gates_t = _peer_route(q, keys)
        uv = jnp.concatenate([expert_u[l], expert_v[l]], axis=1)
        last = l == depth - 1
        assert last, "the final RMSNorm is fused into the last layer's PEER kernel"
        xt = _peer_experts(eid_t.T, gates_t, h2, x1, gate2, norm_f_g.reshape(1, d), uv, seq)
    return xt.reshape(bsz, seq, d)
```

```python
import functools

import jax
import jax.numpy as jnp
from jax import lax
from jax.experimental import pallas as pl
from jax.experimental.pallas import tpu as pltpu

F32 = jnp.float32
BF16 = jnp.bfloat16
I32 = jnp.int32

D_MODEL = 1024
SSD_WIDTH = 1024
SSD_HEAD_DIM = 64
SSD_HEADS = 16
SSD_GROUPS = 4
SSD_STATE = 128
SSD_CONV = 4
SSD_CHUNK = 128
S5_WIDTH = 1024
S5_GROUP_CH = 16
S5_GROUPS = 64
S5_STATE = 64
S5_CHUNK = 16
PEER_HEADS = 8
PEER_NKEYS = 128
PEER_TOPK = 16
PEER_HALF = 128
PEER_SLOTS = PEER_HEADS * PEER_TOPK
EPS = 1e-6
LANES = 128
DT_PAD = LANES
PROJ_WIDTH = SSD_WIDTH + 2 * 1024 + S5_WIDTH + DT_PAD
COL_Z, COL_XS, COL_BC, COL_U = 0, 1, 2, 3
COL_DT = (4 * 1024) // DT_PAD
HIGHEST = lax.Precision.HIGHEST
VMEM_LIMIT = 56 * 1024 * 1024


def _silu(v):
    return v * jax.nn.sigmoid(v)


def _gelu_exact(v):
    return 0.5 * v * (1.0 + lax.erf(v * (2.0 ** -0.5)))


def _rms(v, g):
    return v * lax.rsqrt(jnp.mean(v * v, axis=-1, keepdims=True) + EPS) * g


def _mod_body(c_ref, w_ref, b_ref, o_ref):
    o_ref[...] = jnp.dot(_silu(c_ref[...]), w_ref[...], preferred_element_type=F32) + b_ref[...]


def _adaln_mod(c, w_ada, b_ada):
    bsz, d = c.shape
    n = w_ada.shape[1]
    tn = 1024
    return pl.pallas_call(
        _mod_body,
        grid=(n // tn,),
        in_specs=[pl.BlockSpec((bsz, d), lambda j: (0, 0)),
                  pl.BlockSpec((d, tn), lambda j: (0, j)),
                  pl.BlockSpec((1, tn), lambda j: (0, j))],
        out_specs=pl.BlockSpec((bsz, tn), lambda j: (0, j)),
        out_shape=jax.ShapeDtypeStruct((bsz, n), F32),
        name="adaln_mod",
    )(c, w_ada, b_ada.reshape(1, n))


def _inproj_body(x_ref, sc_ref, sh_ref, g_ref, w_ref, o_ref):
    h = _rms(x_ref[...], g_ref[...]) * (1.0 + sc_ref[0]) + sh_ref[0]
    o_ref[...] = jnp.dot(h.astype(BF16), w_ref[...], preferred_element_type=F32)


def _in_proj(x2d, scale1, shift1, g1, w_cat, seq):
    t, d = x2d.shape
    n = w_cat.shape[1]
    tm = 256
    per_b = seq // tm
    return pl.pallas_call(
        _inproj_body,
        grid=(t // tm,),
        in_specs=[pl.BlockSpec((tm, d), lambda i: (i, 0)),
                  pl.BlockSpec((1, 1, d), lambda i: (i // per_b, 0, 0)),
                  pl.BlockSpec((1, 1, d), lambda i: (i // per_b, 0, 0)),
                  pl.BlockSpec((1, d), lambda i: (0, 0)),
                  pl.BlockSpec((d, n), lambda i: (0, 0))],
        out_specs=pl.BlockSpec((tm, n), lambda i: (i, 0)),
        out_shape=jax.ShapeDtypeStruct((t, n), F32),
        compiler_params=pltpu.CompilerParams(dimension_semantics=("parallel",),
                                             vmem_limit_bytes=VMEM_LIMIT),
        name="in_proj",
    )(x2d, scale1, shift1, g1, w_cat)


HALO = 8
SSD_PAIRS = SSD_HEADS // 2


def _ssd_body(z_ref, xs_ref, bc_ref, dt_ref, cw_ref, cb_ref, dtb_ref, a_ref, d_ref, ng_ref,
              o_ref, win_ref, act_ref, state_ref, y_ref):
    c = pl.program_id(1)
    L = SSD_CHUNK

    @pl.when(c == 0)
    def _():
        win_ref[0:HALO, :] = jnp.zeros((HALO, 2048), F32)
        state_ref[...] = jnp.zeros(state_ref.shape, F32)

    @pl.when(c > 0)
    def _():
        win_ref[0:HALO, :] = win_ref[L:L + HALO, :]

    win_ref[HALO:HALO + L, 0:1024] = xs_ref[...]
    win_ref[HALO:HALO + L, 1024:2048] = bc_ref[...]

    for sl in range(8):
        cols = slice(sl * 256, (sl + 1) * 256)
        acc = jnp.broadcast_to(cb_ref[:, cols], (L, 256))
        for k in range(SSD_CONV):
            r0 = HALO - (SSD_CONV - 1) + k
            acc = acc + win_ref[r0:r0 + L, cols] * cw_ref[k:k + 1, cols]
        act_ref[:, cols] = _silu(acc)

    dtv = dt_ref[...] + dtb_ref[...]
    dt = jnp.maximum(dtv, 0.0) + jnp.log1p(jnp.exp(-jnp.abs(dtv)))
    da = dt * a_ref[...]
    row = lax.broadcasted_iota(I32, (L, L), 0)
    col = lax.broadcasted_iota(I32, (L, L), 1)
    causal = row >= col
    tril = jnp.where(causal, 1.0, 0.0).astype(F32)
    cs = jnp.dot(tril, da, precision=HIGHEST, preferred_element_type=F32)
    cs_t = cs.T
    dec_t = jnp.exp(cs_t[:, L - 1:L] - cs_t)
    ecs = jnp.exp(cs)
    lane = lax.broadcasted_iota(I32, (L, LANES), 1)
    first = lane < SSD_HEAD_DIM

    for g in range(SSD_GROUPS):
        bg = act_ref[:, 1024 + g * 128:1024 + (g + 1) * 128]
        cg = act_ref[:, 1536 + g * 128:1536 + (g + 1) * 128]
        bg_t = bg.T
        cg_b = cg.astype(BF16)
        cb = jnp.dot(cg_b, bg_t.astype(BF16), preferred_element_type=F32)
        for r in range(2):
            p = g * 2 + r
            h0, h1 = 2 * p, 2 * p + 1
            x2 = act_ref[:, p * 128:(p + 1) * 128]
            dt2 = jnp.where(first, dt[:, h0:h0 + 1], dt[:, h1:h1 + 1])
            xdt = (x2 * dt2).astype(BF16)
            ydiag, snew = [], []
            for h in (h0, h1):
                diff = cs[:, h:h + 1] - cs_t[h:h + 1, :]
                lmat = jnp.where(causal, jnp.exp(jnp.where(causal, diff, 0.0)), 0.0)
                ydiag.append(jnp.dot((cb * lmat).astype(BF16), xdt, preferred_element_type=F32))
                bw = (bg_t * dec_t[h:h + 1, :]).astype(BF16)
                snew.append(jnp.dot(bw, xdt, preferred_element_type=F32))
            prev = state_ref[p]
            yoff = jnp.dot(cg_b, prev.astype(BF16), preferred_element_type=F32)
            yoff = yoff * jnp.where(first, ecs[:, h0:h0 + 1], ecs[:, h1:h1 + 1])
            cdec = jnp.where(first[0:1, :], ecs[L - 1:L, h0:h0 + 1], ecs[L - 1:L, h1:h1 + 1])
            state_ref[p] = prev * cdec + jnp.where(first, snew[0], snew[1])
            y2 = jnp.where(first, ydiag[0], ydiag[1]) + yoff + d_ref[:, p * 128:(p + 1) * 128] * x2
            y_ref[:, p * 128:(p + 1) * 128] = y2

    gw = SSD_WIDTH // SSD_GROUPS
    for g in range(SSD_GROUPS):
        cols = slice(g * gw, (g + 1) * gw)
        yg = y_ref[:, cols] * _silu(z_ref[:, cols])
        o_ref[:, cols] = _rms(yg, ng_ref[:, cols])


def _ssd_mixer(proj, conv_w, conv_b, dt_bias_p, a_p, d_exp, norm_g, bsz, seq):
    t = proj.shape[0]
    L = SSD_CHUNK
    nc = seq // L
    row = lambda b, c: b * nc + c
    full = lambda shape: pl.BlockSpec(shape, lambda b, c: (0, 0))
    return pl.pallas_call(
        _ssd_body,
        grid=(bsz, nc),
        in_specs=[pl.BlockSpec((L, 1024), lambda b, c: (row(b, c), COL_Z)),
                  pl.BlockSpec((L, 1024), lambda b, c: (row(b, c), COL_XS)),
                  pl.BlockSpec((L, 1024), lambda b, c: (row(b, c), COL_BC)),
                  pl.BlockSpec((L, DT_PAD), lambda b, c: (row(b, c), COL_DT)),
                  full((SSD_CONV, 2048)), full((1, 2048)), full((1, DT_PAD)), full((1, DT_PAD)),
                  full((1, 1024)), full((1, 1024))],
        out_specs=pl.BlockSpec((L, 1024), lambda b, c: (row(b, c), 0)),
        out_shape=jax.ShapeDtypeStruct((t, SSD_WIDTH), F32),
        scratch_shapes=[pltpu.VMEM((L + HALO, 2048), F32),
                        pltpu.VMEM((L, 2048), F32),
                        pltpu.VMEM((SSD_PAIRS, SSD_STATE, LANES), F32),
                        pltpu.VMEM((L, 1024), F32)],
        compiler_params=pltpu.CompilerParams(dimension_semantics=("parallel", "arbitrary"),
                                             vmem_limit_bytes=VMEM_LIMIT),
        name="ssd_mixer",
    )(proj, proj, proj, proj, conv_w, conv_b, dt_bias_p, a_p, d_exp, norm_g)


def _s5_prepare(a_re, a_im, log_dt, b_re, b_im, c_re, c_im, d_s5, glu_w, glu_b):
    G, P, H, C = S5_GROUPS, S5_STATE, S5_GROUP_CH, S5_CHUNK
    lr, li = a_re.astype(F32), a_im.astype(F32)
    dt = jnp.exp(log_dt.astype(F32))[:, None]
    mag = jnp.exp(lr * dt)
    lb_re, lb_im = mag * jnp.cos(li * dt), mag * jnp.sin(li * dt)
    den = lr * lr + li * li
    coef_re = ((lb_re - 1.0) * lr + lb_im * li) / den
    coef_im = (lb_im * lr - (lb_re - 1.0) * li) / den
    bb_re = coef_re[..., None] * b_re - coef_im[..., None] * b_im
    bb_im = coef_re[..., None] * b_im + coef_im[..., None] * b_re
    tau = jnp.arange(C + 1, dtype=F32)[:, None, None]
    pm = jnp.exp(lr * dt * tau)
    pw_re, pw_im = pm * jnp.cos(li * dt * tau), pm * jnp.sin(li * dt * tau)
    m_re = pw_re[..., None] * bb_re - pw_im[..., None] * bb_im
    m_im = pw_re[..., None] * bb_im + pw_im[..., None] * bb_re
    e_re = c_re * pw_re[:, :, None, :] - c_im * pw_im[:, :, None, :]
    e_im = c_re * pw_im[:, :, None, :] + c_im * pw_re[:, :, None, :]
    kern = (jnp.einsum('ghp,tgpk->tghk', c_re, m_re[:C], precision=HIGHEST)
            - jnp.einsum('ghp,tgpk->tghk', c_im, m_im[:C], precision=HIGHEST))
    j_in = jnp.arange(C)[:, None]
    j_out = jnp.arange(C)[None, :]
    lag = j_out - j_in
    toep = jnp.where((lag >= 0)[:, :, None, None, None], kern[jnp.clip(lag, 0, C - 1)], 0.0)
    toep = toep.transpose(2, 0, 4, 1, 3).reshape(G, C * H, C * H)
    rev = jnp.arange(C - 1, -1, -1)
    wst_re = m_re[rev].transpose(1, 0, 3, 2).reshape(G, C * H, P)
    wst_im = m_im[rev].transpose(1, 0, 3, 2).reshape(G, C * H, P)
    wout_re = e_re[1:].transpose(1, 3, 0, 2).reshape(G, P, C * H)
    wout_im = (-e_im[1:]).transpose(1, 3, 0, 2).reshape(G, P, C * H)
    lam_re, lam_im = pw_re[C][:, None, :], pw_im[C][:, None, :]
    dvec = jnp.tile(d_s5.astype(F32), (1, C))[:, None, :]
    eye = jnp.eye(C, dtype=F32)
    wglu = (eye[None, :, None, :, None] * glu_w.astype(F32)[:, None, :, None, :]).reshape(G, C * H, C * H)
    bglu = jnp.tile(glu_b.astype(F32), (1, C))[:, None, :]
    return (toep.astype(BF16), wst_re.astype(BF16), wst_im.astype(BF16), lam_re, lam_im,
            wout_re.astype(BF16), wout_im.astype(BF16), dvec, wglu.astype(BF16), bglu)


def _s5_body(nb, x_ref, toep_ref, wsr_ref, wsi_ref, lr_ref, li_ref, wor_ref, woi_ref, d_ref,
             wg_ref, bg_ref, o_ref, sr_ref, si_ref, pr_ref, pi_ref):
    x = x_ref[0]
    xb = x.astype(BF16)
    sr_ref[...] = jnp.dot(xb, wsr_ref[0], preferred_element_type=F32)
    si_ref[...] = jnp.dot(xb, wsi_ref[0], preferred_element_type=F32)
    lam_r = jnp.broadcast_to(lr_ref[0], (nb, S5_STATE))
    lam_i = jnp.broadcast_to(li_ref[0], (nb, S5_STATE))
    n_chunks = x.shape[0] // nb

    def step(c, carry):
        xr, xi = carry
        rows = pl.ds(pl.multiple_of(c * nb, nb), nb)
        pr_ref[rows, :] = xr
        pi_ref[rows, :] = xi
        nr = lam_r * xr - lam_i * xi + sr_ref[rows, :]
        ni = lam_r * xi + lam_i * xr + si_ref[rows, :]
        return nr, ni

    zero = jnp.zeros((nb, S5_STATE), F32)
    lax.fori_loop(0, n_chunks, step, (zero, zero))
    y = (jnp.dot(xb, toep_ref[0], preferred_element_type=F32)
         + jnp.dot(pr_ref[...].astype(BF16), wor_ref[0], preferred_element_type=F32)
         + jnp.dot(pi_ref[...].astype(BF16), woi_ref[0], preferred_element_type=F32)
         + d_ref[0] * x)
    v = _gelu_exact(y)
    gate = jax.nn.sigmoid(jnp.dot(v.astype(BF16), wg_ref[0], preferred_element_type=F32) + bg_ref[0])
    o_ref[0] = v * gate


def _s5_mixer(xg, prep, nb):
    G, R, W = xg.shape
    toep, wsr, wsi, lam_re, lam_im, wor, woi, dvec, wglu, bglu = prep
    P = S5_STATE
    per_g = lambda shape: pl.BlockSpec((1,) + shape, lambda g: (g, 0, 0))
    return pl.pallas_call(
        functools.partial(_s5_body, nb),
        grid=(G,),
        in_specs=[per_g((R, W)), per_g((W, W)), per_g((W, P)), per_g((W, P)), per_g((1, P)), per_g((1, P)),
                  per_g((P, W)), per_g((P, W)), per_g((1, W)), per_g((W, W)), per_g((1, W))],
        out_specs=per_g((R, W)),
        out_shape=jax.ShapeDtypeStruct((G, R, W), F32),
        scratch_shapes=[pltpu.VMEM((R, P), F32)] * 4,
        compiler_params=pltpu.CompilerParams(dimension_semantics=("parallel",),
                                             vmem_limit_bytes=VMEM_LIMIT),
        name="s5_mixer",
    )(xg, toep, wsr, wsi, lam_re, lam_im, wor, woi, dvec, wglu, bglu)


def _outproj_body(ys_ref, s5_ref, x_ref, g1_ref, sc2_ref, sh2_ref, ns5_ref, n2_ref, wo_ref, wq_ref,
                  x1_ref, h2_ref, q_ref):
    a = ys_ref[...].astype(BF16)
    b = _rms(s5_ref[...], ns5_ref[...]).astype(BF16)
    mix = (jnp.dot(a, wo_ref[0:SSD_WIDTH, :], preferred_element_type=F32)
           + jnp.dot(b, wo_ref[SSD_WIDTH:, :], preferred_element_type=F32))
    x1 = x_ref[...] + g1_ref[0] * mix
    x1_ref[...] = x1
    h2 = _rms(x1, n2_ref[...]) * (1.0 + sc2_ref[0]) + sh2_ref[0]
    h2_ref[...] = h2
    q = jnp.dot(h2.astype(BF16), wq_ref[...], preferred_element_type=F32)
    for hc in range(2 * PEER_HEADS):
        q_ref[hc] = q[:, hc * PEER_HALF:(hc + 1) * PEER_HALF]


def _out_proj(yssd, s5o, x2d, gate1, scale2, shift2, ns5, n2, w_out, w_query, seq):
    t, d = x2d.shape
    nhc = w_query.shape[1] // PEER_HALF
    tm = 256
    per_b = seq // tm
    rowblk = lambda w: pl.BlockSpec((tm, w), lambda i: (i, 0))
    modblk = pl.BlockSpec((1, 1, d), lambda i: (i // per_b, 0, 0))
    full = lambda shape: pl.BlockSpec(shape, lambda i: (0, 0))
    return pl.pallas_call(
        _outproj_body,
        grid=(t // tm,),
        in_specs=[rowblk(d), rowblk(d), rowblk(d), modblk, modblk, modblk, full((1, d)), full((1, d)),
                  full(w_out.shape), full(w_query.shape)],
        out_specs=[rowblk(d), rowblk(d), pl.BlockSpec((nhc, tm, PEER_HALF), lambda i: (0, i, 0))],
        out_shape=[jax.ShapeDtypeStruct((t, d), F32), jax.ShapeDtypeStruct((t, d), F32),
                   jax.ShapeDtypeStruct((nhc, t, PEER_HALF), F32)],
        compiler_params=pltpu.CompilerParams(dimension_semantics=("parallel",),
                                             vmem_limit_bytes=VMEM_LIMIT),
        name="out_proj_query",
    )(yssd, s5o, x2d, gate1, scale2, shift2, ns5, n2, w_out, w_query)


def _topk_rows(s, n_out):
    n = s.shape[0]
    iota = lax.broadcasted_iota(I32, s.shape, 0)
    vals, idxs = [], []
    for _ in range(n_out):
        m = jnp.max(s, axis=0, keepdims=True)
        am = jnp.min(jnp.where(s == m, iota, n), axis=0, keepdims=True)
        vals.append(m)
        idxs.append(am)
        s = jnp.where(iota == am, -jnp.inf, s)
    return jnp.concatenate(vals, axis=0), jnp.concatenate(idxs, axis=0)


def _route_body(q_ref, keys_ref, eid_ref, gate_ref, ts_ref, ti_ref):
    K = PEER_TOPK

    def half(hc, _):
        st = lax.dot_general(keys_ref[hc], q_ref[hc].astype(BF16), (((1,), (1,)), ((), ())),
                             preferred_element_type=F32)
        v, i = _topk_rows(st, K)
        ts_ref[hc] = v
        ti_ref[hc] = i
        return 0

    lax.fori_loop(0, 2 * PEER_HEADS, half, 0)

    def head(h, _):
        s1, i1 = ts_ref[2 * h], ti_ref[2 * h]
        s2, i2 = ts_ref[2 * h + 1], ti_ref[2 * h + 1]
        cs = [s1[0:1, :] + s2] + [s1[a:a + 1, :] + s2[0:8, :] for a in range(1, 8)] + [s1[8:K, :] + s2[0:1, :]]
        ci = ([i1[0:1, :] * PEER_NKEYS + i2] + [i1[a:a + 1, :] * PEER_NKEYS + i2[0:8, :] for a in range(1, 8)]
              + [i1[8:K, :] * PEER_NKEYS + i2[0:1, :]])
        cand = jnp.concatenate(cs, axis=0)
        cid = jnp.concatenate(ci, axis=0)
        top_s, pos = _topk_rows(cand, K)
        iota = lax.broadcasted_iota(I32, cand.shape, 0)
        ids = [jnp.max(jnp.where(iota == pos[k:k + 1, :], cid, -1), axis=0, keepdims=True) for k in range(K)]
        e = jnp.exp(top_s - top_s[0:1, :])
        rows = pl.ds(pl.multiple_of(h * K, K), K)
        gate_ref[rows, :] = e / jnp.sum(e, axis=0, keepdims=True)
        eid_ref[rows, :] = jnp.concatenate(ids, axis=0)
        return 0

    lax.fori_loop(0, PEER_HEADS, head, 0)


def _peer_route(q, keys):
    nhc, t, dh = q.shape
    tm = 128
    return pl.pallas_call(
        _route_body,
        grid=(t // tm,),
        in_specs=[pl.BlockSpec((nhc, tm, dh), lambda i: (0, i, 0)),
                  pl.BlockSpec(keys.shape, lambda i: (0, 0, 0))],
        out_specs=[pl.BlockSpec((PEER_SLOTS, tm), lambda i: (0, i)),
                   pl.BlockSpec((PEER_SLOTS, tm), lambda i: (0, i))],
        out_shape=[jax.ShapeDtypeStruct((PEER_SLOTS, t), I32), jax.ShapeDtypeStruct((PEER_SLOTS, t), F32)],
        scratch_shapes=[pltpu.VMEM((2 * PEER_HEADS, PEER_TOPK, tm), F32),
                        pltpu.VMEM((2 * PEER_HEADS, PEER_TOPK, tm), I32)],
        compiler_params=pltpu.CompilerParams(dimension_semantics=("parallel",)),
        name="peer_route",
    )(q, keys)


SUBLANES = 8
PEER_TOK_TILE = 256
PEER_RING = 16
PEER_AHEAD = 8
PEER_JGROUPS = PEER_SLOTS // SUBLANES
SLAB_ROWS = 2 * SUBLANES


def _sublane_sums(p, sub):
    m4, m2, m1 = sub < 4, (sub & 2) == 0, (sub & 1) == 0
    z = [jnp.where(m4, p[a], p[a + 4]) + pltpu.roll(jnp.where(m4, p[a + 4], p[a]), 4, 0) for a in range(4)]
    y = [jnp.where(m2, z[a] + pltpu.roll(z[a], 6, 0), z[a + 2] + pltpu.roll(z[a + 2], 2, 0)) for a in range(2)]
    return jnp.where(m1, y[0] + pltpu.roll(y[0], 7, 0), y[1] + pltpu.roll(y[1], 1, 0))


def _peer_body(eid_ref, gate_ref, h_ref, x1_ref, g2_ref, nf_ref, uv_ref, o_ref, buf_ref, po_ref, wb_ref, sem):
    n_blocks = PEER_TOK_TILE // PEER_RING
    sub = lax.broadcasted_iota(I32, (SUBLANES, LANES), 0)
    lane = lax.broadcasted_iota(I32, (SUBLANES, LANES), 1)

    def issue_rows(t, slot, k0, k1):
        for k in range(k0, k1):
            e = eid_ref[t, k]
            pltpu.make_async_copy(uv_ref.at[e], buf_ref.at[slot, k], sem.at[slot]).start(priority=k % 2)

    def wait_rows(slot):
        pltpu.make_async_copy(uv_ref.at[pl.ds(0, PEER_SLOTS)], buf_ref.at[slot], sem.at[slot]).wait()

    for t0 in range(PEER_AHEAD):
        issue_rows(t0, t0, 0, PEER_SLOTS)

    def token(t, slot, prefetch):
        nslot = (slot + PEER_AHEAD) % PEER_RING
        per = PEER_SLOTS // 2 // PEER_JGROUPS
        wait_rows(slot)
        h_t = h_ref[t]
        dense = jnp.zeros((SUBLANES, LANES), F32)
        for j in range(PEER_JGROUPS):
            prods = [buf_ref[slot, SUBLANES * j + i, 0:SUBLANES, :] * h_t for i in range(SUBLANES)]
            col = jnp.sum(_sublane_sums(prods, sub), axis=1, keepdims=True)
            dense = jnp.where(lane == j, col, dense)
            if prefetch:
                issue_rows(t + PEER_AHEAD, nslot, j * per, (j + 1) * per)
        w = gate_ref[t] * _gelu_exact(dense)
        for j in range(PEER_JGROUPS):
            wb_ref[SUBLANES * j:SUBLANES * (j + 1), :] = jnp.broadcast_to(w[:, j:j + 1], (SUBLANES, LANES))
        accs = [jnp.zeros((SUBLANES, LANES), F32) for _ in range(4)]
        for k in range(PEER_SLOTS):
            wk = jnp.broadcast_to(wb_ref[k:k + 1, :], (SUBLANES, LANES))
            accs[k % 4] = accs[k % 4] + buf_ref[slot, k, SUBLANES:SLAB_ROWS, :] * wk
            if prefetch and k % SUBLANES == SUBLANES - 1:
                j = k // SUBLANES
                issue_rows(t + PEER_AHEAD, nslot, PEER_SLOTS // 2 + j * per, PEER_SLOTS // 2 + (j + 1) * per)
        po_ref[t] = (accs[0] + accs[1]) + (accs[2] + accs[3])

    def block(b, _):
        for slot in range(PEER_RING):
            token(b * PEER_RING + slot, slot, True)
        return 0

    lax.fori_loop(0, n_blocks - 1, block, 0)
    for slot in range(PEER_RING):
        token((n_blocks - 1) * PEER_RING + slot, slot, slot + PEER_AHEAD < PEER_RING)

    x2 = x1_ref[...] + g2_ref[...] * po_ref[...]
    ss = jnp.sum(jnp.sum(x2 * x2, axis=2, keepdims=True), axis=1, keepdims=True)
    o_ref[...] = x2 * lax.rsqrt(ss * (1.0 / D_MODEL) + EPS) * nf_ref[...]


def _peer_experts(eid, gates3, h3, x13, gate2, nf, uv3, seq):
    t = h3.shape[0]
    tt = PEER_TOK_TILE
    per_b = seq // tt
    tokblk = pl.BlockSpec((tt, SUBLANES, LANES), lambda i: (i, 0, 0))
    return pl.pallas_call(
        _peer_body,
        grid=(t // tt,),
        in_specs=[pl.BlockSpec((tt, PEER_SLOTS), lambda i: (i, 0), memory_space=pltpu.SMEM),
                  tokblk, tokblk, tokblk,
                  pl.BlockSpec((1, SUBLANES, LANES), lambda i: (i // per_b, 0, 0)),
                  pl.BlockSpec((1, SUBLANES, LANES), lambda i: (0, 0, 0)),
                  pl.BlockSpec(memory_space=pl.ANY)],
        out_specs=tokblk,
        out_shape=jax.ShapeDtypeStruct((t, SUBLANES, LANES), F32),
        scratch_shapes=[pltpu.VMEM((PEER_RING, PEER_SLOTS, SLAB_ROWS, LANES), F32),
                        pltpu.VMEM((tt, SUBLANES, LANES), F32),
                        pltpu.VMEM((PEER_SLOTS, LANES), F32),
                        pltpu.SemaphoreType.DMA((PEER_RING,))],
        compiler_params=pltpu.CompilerParams(dimension_semantics=("arbitrary",),
                                             vmem_limit_bytes=VMEM_LIMIT),
        name="peer_experts",
    )(eid, gates3, h3, x13, gate2, nf, uv3)


def kernel(x, c, w_ada, b_ada, norm1_g, w_in, conv_w, conv_b, dt_bias, a_log, d_ssd, norm_ssd_g, s5_a_re, s5_a_im, s5_log_dt, s5_b_re, s5_b_im, s5_c_re, s5_c_im, s5_d, glu_w, glu_b, norm_s5_g, w_out, norm2_g, w_query, sub_keys, expert_u, expert_v, norm_f_g):
    bsz, seq, d = x.shape
    t = bsz * seq
    depth = w_ada.shape[0]
    xt = x.reshape(t, d)
    for l in range(depth):
        mod = _adaln_mod(c, w_ada[l], b_ada[l])
        shift1, scale1, gate1, shift2, scale2, gate2 = [m.reshape(bsz, 1, d) for m in jnp.split(mod, 6, axis=-1)]

        wz, wxbc, wdt, wu = jnp.split(w_in[l], [SSD_WIDTH, SSD_WIDTH + 2048, SSD_WIDTH + 2048 + SSD_HEADS], axis=1)
        w_cat = jnp.concatenate([wz, wxbc, wu, jnp.pad(wdt, ((0, 0), (0, DT_PAD - SSD_HEADS)))], axis=1).astype(BF16)
        proj = _in_proj(xt, scale1, shift1, norm1_g[l].reshape(1, d), w_cat, seq)

        pad_h = lambda v: jnp.pad(v.astype(F32), (0, DT_PAD - SSD_HEADS)).reshape(1, DT_PAD)
        yssd = _ssd_mixer(proj, conv_w[l], conv_b[l].reshape(1, -1), pad_h(dt_bias[l]),
                          pad_h(-jnp.exp(a_log[l].astype(F32))),
                          jnp.repeat(d_ssd[l].astype(F32), SSD_HEAD_DIM).reshape(1, SSD_WIDTH),
                          norm_ssd_g[l].reshape(1, SSD_WIDTH), bsz, seq)

        nc5 = seq // S5_CHUNK
        u = proj[:, COL_U * 1024:(COL_U + 1) * 1024].reshape(bsz, nc5, S5_CHUNK, S5_GROUPS, S5_GROUP_CH)
        xg = u.transpose(3, 1, 0, 2, 4).reshape(S5_GROUPS, nc5 * bsz, S5_CHUNK * S5_GROUP_CH)
        prep = _s5_prepare(s5_a_re[l], s5_a_im[l], s5_log_dt[l], s5_b_re[l], s5_b_im[l], s5_c_re[l],
                           s5_c_im[l], s5_d[l], glu_w[l], glu_b[l])
        og = _s5_mixer(xg, prep, bsz)
        s5o = og.reshape(S5_GROUPS, nc5, bsz, S5_CHUNK, S5_GROUP_CH).transpose(2, 1, 3, 0, 4).reshape(t, S5_WIDTH)

        x1, h2, q = _out_proj(yssd, s5o, xt, gate1, scale2, shift2, norm_s5_g[l].reshape(1, -1),
                              norm2_g[l].reshape(1, d), w_out[l].astype(BF16), w_query[l].astype(BF16), seq)

        keys = sub_keys[l].reshape(2 * PEER_HEADS, PEER_NKEYS, PEER_HALF).astype(BF16)
        eid_t, gates_t = _peer_route(q, keys)
        n_exp = expert_u.shape[1]
        tile3 = lambda v: v.reshape(-1, SUBLANES, LANES)
        uv3 = jnp.concatenate([expert_u[l].reshape(n_exp, SUBLANES, LANES),
                               expert_v[l].reshape(n_exp, SUBLANES, LANES)], axis=1)
        gates3 = jnp.pad(gates_t.reshape(PEER_JGROUPS, SUBLANES, t).transpose(2, 1, 0),
                         ((0, 0), (0, 0), (0, LANES - PEER_JGROUPS)))
        last = l == depth - 1
        assert last, "the final RMSNorm is fused into the last layer's PEER kernel"
        xt = _peer_experts(eid_t.T, gates3, tile3(h2), tile3(x1), tile3(gate2), tile3(norm_f_g), uv3, seq)
        xt = xt.reshape(t, d)
    return xt.reshape(bsz, seq, d)
```

```python
import functools

import jax
import jax.numpy as jnp
from jax import lax
from jax.experimental import pallas as pl
from jax.experimental.pallas import tpu as pltpu

F32 = jnp.float32
BF16 = jnp.bfloat16
I32 = jnp.int32

D_MODEL = 1024
SSD_WIDTH = 1024
SSD_HEAD_DIM = 64
SSD_HEADS = 16
SSD_GROUPS = 4
SSD_STATE = 128
SSD_CONV = 4
SSD_CHUNK = 128
S5_WIDTH = 1024
S5_GROUP_CH = 16
S5_GROUPS = 64
S5_STATE = 64
S5_CHUNK = 16
PEER_HEADS = 8
PEER_NKEYS = 128
PEER_TOPK = 16
PEER_HALF = 128
PEER_SLOTS = PEER_HEADS * PEER_TOPK
EPS = 1e-6
LANES = 128
DT_PAD = LANES
PROJ_WIDTH = SSD_WIDTH + 2 * 1024 + S5_WIDTH + DT_PAD
COL_Z, COL_XS, COL_BC, COL_U = 0, 1, 2, 3
COL_DT = (4 * 1024) // DT_PAD
HIGHEST = lax.Precision.HIGHEST
VMEM_LIMIT = 56 * 1024 * 1024


def _silu(v):
    return v * jax.nn.sigmoid(v)


def _gelu_exact(v):
    return 0.5 * v * (1.0 + lax.erf(v * (2.0 ** -0.5)))


def _rms(v, g):
    return v * lax.rsqrt(jnp.mean(v * v, axis=-1, keepdims=True) + EPS) * g


def _mod_body(c_ref, w_ref, b_ref, o_ref):
    o_ref[...] = jnp.dot(_silu(c_ref[...]), w_ref[...], preferred_element_type=F32) + b_ref[...]


def _adaln_mod(c, w_ada, b_ada):
    bsz, d = c.shape
    n = w_ada.shape[1]
    tn = 1024
    return pl.pallas_call(
        _mod_body,
        grid=(n // tn,),
        in_specs=[pl.BlockSpec((bsz, d), lambda j: (0, 0)),
                  pl.BlockSpec((d, tn), lambda j: (0, j)),
                  pl.BlockSpec((1, tn), lambda j: (0, j))],
        out_specs=pl.BlockSpec((bsz, tn), lambda j: (0, j)),
        out_shape=jax.ShapeDtypeStruct((bsz, n), F32),
        name="adaln_mod",
    )(c, w_ada, b_ada.reshape(1, n))


def _inproj_body(x_ref, sc_ref, sh_ref, g_ref, w_ref, o_ref):
    h = _rms(x_ref[...], g_ref[...]) * (1.0 + sc_ref[0]) + sh_ref[0]
    o_ref[...] = jnp.dot(h.astype(BF16), w_ref[...], preferred_element_type=F32)


def _in_proj(x2d, scale1, shift1, g1, w_cat, seq):
    t, d = x2d.shape
    n = w_cat.shape[1]
    tm = 256
    per_b = seq // tm
    return pl.pallas_call(
        _inproj_body,
        grid=(t // tm,),
        in_specs=[pl.BlockSpec((tm, d), lambda i: (i, 0)),
                  pl.BlockSpec((1, 1, d), lambda i: (i // per_b, 0, 0)),
                  pl.BlockSpec((1, 1, d), lambda i: (i // per_b, 0, 0)),
                  pl.BlockSpec((1, d), lambda i: (0, 0)),
                  pl.BlockSpec((d, n), lambda i: (0, 0))],
        out_specs=pl.BlockSpec((tm, n), lambda i: (i, 0)),
        out_shape=jax.ShapeDtypeStruct((t, n), F32),
        compiler_params=pltpu.CompilerParams(dimension_semantics=("parallel",),
                                             vmem_limit_bytes=VMEM_LIMIT),
        name="in_proj",
    )(x2d, scale1, shift1, g1, w_cat)


HALO = 8
SSD_PAIRS = SSD_HEADS // 2


def _ssd_body(z_ref, xs_ref, bc_ref, dt_ref, cw_ref, cb_ref, dtb_ref, a_ref, d_ref, ng_ref,
              o_ref, win_ref, act_ref, state_ref, y_ref):
    c = pl.program_id(1)
    L = SSD_CHUNK

    @pl.when(c == 0)
    def _():
        win_ref[0:HALO, :] = jnp.zeros((HALO, 2048), F32)
        state_ref[...] = jnp.zeros(state_ref.shape, F32)

    @pl.when(c > 0)
    def _():
        win_ref[0:HALO, :] = win_ref[L:L + HALO, :]

    win_ref[HALO:HALO + L, 0:1024] = xs_ref[...]
    win_ref[HALO:HALO + L, 1024:2048] = bc_ref[...]

    for sl in range(8):
        cols = slice(sl * 256, (sl + 1) * 256)
        acc = jnp.broadcast_to(cb_ref[:, cols], (L, 256))
        for k in range(SSD_CONV):
            r0 = HALO - (SSD_CONV - 1) + k
            acc = acc + win_ref[r0:r0 + L, cols] * cw_ref[k:k + 1, cols]
        act_ref[:, cols] = _silu(acc)

    dtv = dt_ref[...] + dtb_ref[...]
    dt = jnp.maximum(dtv, 0.0) + jnp.log1p(jnp.exp(-jnp.abs(dtv)))
    da = dt * a_ref[...]
    row = lax.broadcasted_iota(I32, (L, L), 0)
    col = lax.broadcasted_iota(I32, (L, L), 1)
    causal = row >= col
    tril = jnp.where(causal, 1.0, 0.0).astype(F32)
    cs = jnp.dot(tril, da, precision=HIGHEST, preferred_element_type=F32)
    cs_t = cs.T
    dec_t = jnp.exp(cs_t[:, L - 1:L] - cs_t)
    ecs = jnp.exp(cs)
    lane = lax.broadcasted_iota(I32, (L, LANES), 1)
    first = lane < SSD_HEAD_DIM

    for g in range(SSD_GROUPS):
        bg = act_ref[:, 1024 + g * 128:1024 + (g + 1) * 128]
        cg = act_ref[:, 1536 + g * 128:1536 + (g + 1) * 128]
        bg_t = bg.T
        cg_b = cg.astype(BF16)
        cb = jnp.dot(cg_b, bg_t.astype(BF16), preferred_element_type=F32)
        for r in range(2):
            p = g * 2 + r
            h0, h1 = 2 * p, 2 * p + 1
            x2 = act_ref[:, p * 128:(p + 1) * 128]
            dt2 = jnp.where(first, dt[:, h0:h0 + 1], dt[:, h1:h1 + 1])
            xdt = (x2 * dt2).astype(BF16)
            ydiag, snew = [], []
            for h in (h0, h1):
                diff = cs[:, h:h + 1] - cs_t[h:h + 1, :]
                lmat = jnp.where(causal, jnp.exp(jnp.where(causal, diff, 0.0)), 0.0)
                ydiag.append(jnp.dot((cb * lmat).astype(BF16), xdt, preferred_element_type=F32))
                bw = (bg_t * dec_t[h:h + 1, :]).astype(BF16)
                snew.append(jnp.dot(bw, xdt, preferred_element_type=F32))
            prev = state_ref[p]
            yoff = jnp.dot(cg_b, prev.astype(BF16), preferred_element_type=F32)
            yoff = yoff * jnp.where(first, ecs[:, h0:h0 + 1], ecs[:, h1:h1 + 1])
            cdec = jnp.where(first[0:1, :], ecs[L - 1:L, h0:h0 + 1], ecs[L - 1:L, h1:h1 + 1])
            state_ref[p] = prev * cdec + jnp.where(first, snew[0], snew[1])
            y2 = jnp.where(first, ydiag[0], ydiag[1]) + yoff + d_ref[:, p * 128:(p + 1) * 128] * x2
            y_ref[:, p * 128:(p + 1) * 128] = y2

    gw = SSD_WIDTH // SSD_GROUPS
    for g in range(SSD_GROUPS):
        cols = slice(g * gw, (g + 1) * gw)
        yg = y_ref[:, cols] * _silu(z_ref[:, cols])
        o_ref[:, cols] = _rms(yg, ng_ref[:, cols])


def _ssd_mixer(proj, conv_w, conv_b, dt_bias_p, a_p, d_exp, norm_g, bsz, seq):
    t = proj.shape[0]
    L = SSD_CHUNK
    nc = seq // L
    row = lambda b, c: b * nc + c
    full = lambda shape: pl.BlockSpec(shape, lambda b, c: (0, 0))
    return pl.pallas_call(
        _ssd_body,
        grid=(bsz, nc),
        in_specs=[pl.BlockSpec((L, 1024), lambda b, c: (row(b, c), COL_Z)),
                  pl.BlockSpec((L, 1024), lambda b, c: (row(b, c), COL_XS)),
                  pl.BlockSpec((L, 1024), lambda b, c: (row(b, c), COL_BC)),
                  pl.BlockSpec((L, DT_PAD), lambda b, c: (row(b, c), COL_DT)),
                  full((SSD_CONV, 2048)), full((1, 2048)), full((1, DT_PAD)), full((1, DT_PAD)),
                  full((1, 1024)), full((1, 1024))],
        out_specs=pl.BlockSpec((L, 1024), lambda b, c: (row(b, c), 0)),
        out_shape=jax.ShapeDtypeStruct((t, SSD_WIDTH), F32),
        scratch_shapes=[pltpu.VMEM((L + HALO, 2048), F32),
                        pltpu.VMEM((L, 2048), F32),
                        pltpu.VMEM((SSD_PAIRS, SSD_STATE, LANES), F32),
                        pltpu.VMEM((L, 1024), F32)],
        compiler_params=pltpu.CompilerParams(dimension_semantics=("parallel", "arbitrary"),
                                             vmem_limit_bytes=VMEM_LIMIT),
        name="ssd_mixer",
    )(proj, proj, proj, proj, conv_w, conv_b, dt_bias_p, a_p, d_exp, norm_g)


def _s5_cols_to_hg(w, axis):
    shape = w.shape
    split = shape[:axis] + (S5_GROUPS, S5_GROUP_CH) + shape[axis + 1:]
    return jnp.swapaxes(w.reshape(split), axis, axis + 1).reshape(shape)


def _s5_prepare(a_re, a_im, log_dt, b_re, b_im, c_re, c_im, d_s5, glu_w, glu_b):
    G, P, H, C = S5_GROUPS, S5_STATE, S5_GROUP_CH, S5_CHUNK
    lr, li = a_re.astype(F32), a_im.astype(F32)
    dt = jnp.exp(log_dt.astype(F32))[:, None]
    mag = jnp.exp(lr * dt)
    lb_re, lb_im = mag * jnp.cos(li * dt), mag * jnp.sin(li * dt)
    den = lr * lr + li * li
    coef_re = ((lb_re - 1.0) * lr + lb_im * li) / den
    coef_im = (lb_im * lr - (lb_re - 1.0) * li) / den
    bb_re = coef_re[..., None] * b_re - coef_im[..., None] * b_im
    bb_im = coef_re[..., None] * b_im + coef_im[..., None] * b_re
    tau = jnp.arange(C + 1, dtype=F32)[:, None, None]
    pm = jnp.exp(lr * dt * tau)
    pw_re, pw_im = pm * jnp.cos(li * dt * tau), pm * jnp.sin(li * dt * tau)
    m_re = pw_re[..., None] * bb_re - pw_im[..., None] * bb_im
    m_im = pw_re[..., None] * bb_im + pw_im[..., None] * bb_re
    e_re = c_re * pw_re[:, :, None, :] - c_im * pw_im[:, :, None, :]
    e_im = c_re * pw_im[:, :, None, :] + c_im * pw_re[:, :, None, :]
    kern = (jnp.einsum('ghp,tgpk->tghk', c_re, m_re[:C], precision=HIGHEST)
            - jnp.einsum('ghp,tgpk->tghk', c_im, m_im[:C], precision=HIGHEST))
    j_in = jnp.arange(C)[:, None]
    j_out = jnp.arange(C)[None, :]
    lag = j_out - j_in
    toep = jnp.where((lag >= 0)[:, :, None, None, None], kern[jnp.clip(lag, 0, C - 1)], 0.0)
    toep = toep.transpose(2, 0, 4, 1, 3).reshape(G, C * H, C * H)
    rev = jnp.arange(C - 1, -1, -1)
    wst_re = m_re[rev].transpose(1, 0, 3, 2).reshape(G, C * H, P)
    wst_im = m_im[rev].transpose(1, 0, 3, 2).reshape(G, C * H, P)
    wout_re = e_re[1:].transpose(1, 3, 0, 2).reshape(G, P, C * H)
    wout_im = (-e_im[1:]).transpose(1, 3, 0, 2).reshape(G, P, C * H)
    padp = lambda v, axis: jnp.pad(v, [(0, LANES - P) if a == axis else (0, 0) for a in range(v.ndim)])
    lam_re, lam_im = padp(pw_re[C][:, None, :], 2), padp(pw_im[C][:, None, :], 2)
    dvec = jnp.tile(d_s5.astype(F32), (1, C))[:, None, :]
    eye = jnp.eye(C, dtype=F32)
    wglu = (eye[None, :, None, :, None] * glu_w.astype(F32)[:, None, :, None, :]).reshape(G, C * H, C * H)
    bglu = jnp.tile(glu_b.astype(F32), (1, C))[:, None, :]
    return (toep.astype(BF16), padp(wst_re, 2).astype(BF16), padp(wst_im, 2).astype(BF16), lam_re, lam_im,
            padp(wout_re, 1).astype(BF16), padp(wout_im, 1).astype(BF16), dvec, wglu.astype(BF16), bglu)


def _s5_body(nb, x_ref, toep_ref, wsr_ref, wsi_ref, lr_ref, li_ref, wor_ref, woi_ref, d_ref,
             wg_ref, bg_ref, o_ref, sr_ref, si_ref, pr_ref, pi_ref):
    x = x_ref[0]
    xb = x.astype(BF16)
    sr_ref[...] = jnp.dot(xb, wsr_ref[0], preferred_element_type=F32)
    si_ref[...] = jnp.dot(xb, wsi_ref[0], preferred_element_type=F32)
    lam_r = jnp.broadcast_to(lr_ref[0], (nb, LANES))
    lam_i = jnp.broadcast_to(li_ref[0], (nb, LANES))
    n_chunks = x.shape[0] // nb

    def step(c, carry):
        xr, xi = carry
        rows = pl.ds(c, nb, stride=n_chunks)
        pr_ref[rows, :] = xr
        pi_ref[rows, :] = xi
        nr = lam_r * xr - lam_i * xi + sr_ref[rows, :]
        ni = lam_r * xi + lam_i * xr + si_ref[rows, :]
        return nr, ni

    zero = jnp.zeros((nb, LANES), F32)
    lax.fori_loop(0, n_chunks, step, (zero, zero))
    y = (jnp.dot(xb, toep_ref[0], preferred_element_type=F32)
         + jnp.dot(pr_ref[...].astype(BF16), wor_ref[0], preferred_element_type=F32)
         + jnp.dot(pi_ref[...].astype(BF16), woi_ref[0], preferred_element_type=F32)
         + d_ref[0] * x)
    v = _gelu_exact(y)
    gate = jax.nn.sigmoid(jnp.dot(v.astype(BF16), wg_ref[0], preferred_element_type=F32) + bg_ref[0])
    o_ref[0] = v * gate


def _s5_mixer(xg, prep, nb):
    G, R, W = xg.shape
    toep, wsr, wsi, lam_re, lam_im, wor, woi, dvec, wglu, bglu = prep
    P = LANES
    per_g = lambda shape: pl.BlockSpec((1,) + shape, lambda g: (g, 0, 0))
    return pl.pallas_call(
        functools.partial(_s5_body, nb),
        grid=(G,),
        in_specs=[per_g((R, W)), per_g((W, W)), per_g((W, P)), per_g((W, P)), per_g((1, P)), per_g((1, P)),
                  per_g((P, W)), per_g((P, W)), per_g((1, W)), per_g((W, W)), per_g((1, W))],
        out_specs=per_g((R, W)),
        out_shape=jax.ShapeDtypeStruct((G, R, W), F32),
        scratch_shapes=[pltpu.VMEM((R, P), F32)] * 4,
        compiler_params=pltpu.CompilerParams(dimension_semantics=("parallel",),
                                             vmem_limit_bytes=VMEM_LIMIT),
        name="s5_mixer",
    )(xg, toep, wsr, wsi, lam_re, lam_im, wor, woi, dvec, wglu, bglu)


def _outproj_body(ys_ref, s5_ref, x_ref, g1_ref, sc2_ref, sh2_ref, ns5_ref, n2_ref, wo_ref, wq_ref,
                  x1_ref, h2_ref, q_ref):
    a = ys_ref[...].astype(BF16)
    b = _rms(s5_ref[...], ns5_ref[...]).astype(BF16)
    mix = (jnp.dot(a, wo_ref[0:SSD_WIDTH, :], preferred_element_type=F32)
           + jnp.dot(b, wo_ref[SSD_WIDTH:, :], preferred_element_type=F32))
    x1 = x_ref[...] + g1_ref[0] * mix
    x1_ref[...] = x1
    h2 = _rms(x1, n2_ref[...]) * (1.0 + sc2_ref[0]) + sh2_ref[0]
    h2_ref[...] = h2
    q = jnp.dot(h2.astype(BF16), wq_ref[...], preferred_element_type=F32)
    for hc in range(2 * PEER_HEADS):
        q_ref[hc] = q[:, hc * PEER_HALF:(hc + 1) * PEER_HALF]


def _out_proj(yssd, s5o, x2d, gate1, scale2, shift2, ns5, n2, w_out, w_query, seq):
    t, d = x2d.shape
    nhc = w_query.shape[1] // PEER_HALF
    tm = 256
    per_b = seq // tm
    rowblk = lambda w: pl.BlockSpec((tm, w), lambda i: (i, 0))
    modblk = pl.BlockSpec((1, 1, d), lambda i: (i // per_b, 0, 0))
    full = lambda shape: pl.BlockSpec(shape, lambda i: (0, 0))
    return pl.pallas_call(
        _outproj_body,
        grid=(t // tm,),
        in_specs=[rowblk(d), rowblk(d), rowblk(d), modblk, modblk, modblk, full((1, d)), full((1, d)),
                  full(w_out.shape), full(w_query.shape)],
        out_specs=[rowblk(d), rowblk(d), pl.BlockSpec((nhc, tm, PEER_HALF), lambda i: (0, i, 0))],
        out_shape=[jax.ShapeDtypeStruct((t, d), F32), jax.ShapeDtypeStruct((t, d), F32),
                   jax.ShapeDtypeStruct((nhc, t, PEER_HALF), F32)],
        compiler_params=pltpu.CompilerParams(dimension_semantics=("parallel",),
                                             vmem_limit_bytes=VMEM_LIMIT),
        name="out_proj_query",
    )(yssd, s5o, x2d, gate1, scale2, shift2, ns5, n2, w_out, w_query)


def _topk_rows(s, n_out, payload=None):
    n = s.shape[0]
    iota = lax.broadcasted_iota(I32, s.shape, 0)
    vals, picks = [], []
    for _ in range(n_out):
        m = jnp.max(s, axis=0, keepdims=True)
        am = jnp.min(jnp.where(s == m, iota, n), axis=0, keepdims=True)
        hit = iota == am
        vals.append(m)
        picks.append(am if payload is None else jnp.max(jnp.where(hit, payload, -1), axis=0, keepdims=True))
        s = jnp.where(hit, -jnp.inf, s)
    return jnp.concatenate(vals, axis=0), jnp.concatenate(picks, axis=0)


def _route_body(q_ref, keys_ref, eid_ref, gate_ref, ts_ref, ti_ref):
    K = PEER_TOPK

    def half(hc, _):
        st = lax.dot_general(keys_ref[hc], q_ref[hc].astype(BF16), (((1,), (1,)), ((), ())),
                             preferred_element_type=F32)
        v, i = _topk_rows(st, K)
        ts_ref[hc] = v
        ti_ref[hc] = i
        return 0

    lax.fori_loop(0, 2 * PEER_HEADS, half, 0, unroll=2)

    def head(h, _):
        s1, i1 = ts_ref[2 * h], ti_ref[2 * h]
        s2, i2 = ts_ref[2 * h + 1], ti_ref[2 * h + 1]
        cs = [s1[0:1, :] + s2] + [s1[a:a + 1, :] + s2[0:8, :] for a in range(1, 8)] + [s1[8:K, :] + s2[0:1, :]]
        ci = ([i1[0:1, :] * PEER_NKEYS + i2] + [i1[a:a + 1, :] * PEER_NKEYS + i2[0:8, :] for a in range(1, 8)]
              + [i1[8:K, :] * PEER_NKEYS + i2[0:1, :]])
        cand = jnp.concatenate(cs, axis=0)
        cid = jnp.concatenate(ci, axis=0)
        top_s, ids = _topk_rows(cand, K, payload=cid)
        e = jnp.exp(top_s - top_s[0:1, :])
        rows = pl.ds(pl.multiple_of(h * K, K), K)
        gate_ref[0, rows, :] = e / jnp.sum(e, axis=0, keepdims=True)
        eid_ref[rows, :] = ids
        return 0

    lax.fori_loop(0, PEER_HEADS, head, 0, unroll=2)


def _peer_route(q, keys):
    nhc, t, dh = q.shape
    tm = 128
    return pl.pallas_call(
        _route_body,
        grid=(t // tm,),
        in_specs=[pl.BlockSpec((nhc, tm, dh), lambda i: (0, i, 0)),
                  pl.BlockSpec(keys.shape, lambda i: (0, 0, 0))],
        out_specs=[pl.BlockSpec((PEER_SLOTS, tm), lambda i: (0, i)),
                   pl.BlockSpec((1, PEER_SLOTS, tm), lambda i: (i, 0, 0))],
        out_shape=[jax.ShapeDtypeStruct((PEER_SLOTS, t), I32),
                   jax.ShapeDtypeStruct((t // tm, PEER_SLOTS, tm), F32)],
        scratch_shapes=[pltpu.VMEM((2 * PEER_HEADS, PEER_TOPK, tm), F32),
                        pltpu.VMEM((2 * PEER_HEADS, PEER_TOPK, tm), I32)],
        compiler_params=pltpu.CompilerParams(dimension_semantics=("parallel",)),
        name="peer_route",
    )(q, keys)


SUBLANES = 8
PEER_TOK_TILE = 512
PEER_RING = 16
PEER_AHEAD = 8
PEER_JGROUPS = PEER_SLOTS // SUBLANES
SLAB_ROWS = 2 * SUBLANES


def _sublane_sums(p, sub):
    m4, m2, m1 = sub < 4, (sub & 2) == 0, (sub & 1) == 0
    z = jnp.where(m4, p[0:4], p[4:8]) + pltpu.roll(jnp.where(m4, p[4:8], p[0:4]), 4, 1)
    y = jnp.where(m2, z[0:2] + pltpu.roll(z[0:2], 6, 1), z[2:4] + pltpu.roll(z[2:4], 2, 1))
    return jnp.where(m1, y[0] + pltpu.roll(y[0], 7, 0), y[1] + pltpu.roll(y[1], 1, 0))


def _peer_body(eid_ref, gate_ref, h_ref, x1_ref, g2_ref, nf_ref, uv_ref, o_ref, buf_ref, po_ref, wb_ref, sem):
    n_blocks = PEER_TOK_TILE // PEER_RING
    sub = lax.broadcasted_iota(I32, (SUBLANES, LANES), 0)
    lane = lax.broadcasted_iota(I32, (SUBLANES, LANES), 1)
    lane_s = lax.broadcasted_iota(I32, (PEER_SLOTS, LANES), 1)

    def issue_rows(t, slot, k0, k1):
        for k in range(k0, k1):
            e = eid_ref[t, k]
            pltpu.make_async_copy(uv_ref.at[e], buf_ref.at[slot, k], sem.at[slot]).start(priority=k % 2)

    def wait_rows(slot):
        pltpu.make_async_copy(uv_ref.at[pl.ds(0, PEER_SLOTS)], buf_ref.at[slot], sem.at[slot]).wait()

    def prologue(t, _):
        issue_rows(t, t, 0, PEER_SLOTS)
        return 0

    lax.fori_loop(0, PEER_AHEAD, prologue, 0)

    def token(t, slot, h_rows):
        t_next = (t + PEER_AHEAD) % PEER_TOK_TILE
        nslot = (slot + PEER_AHEAD) % PEER_RING
        per = PEER_SLOTS // 2 // PEER_JGROUPS
        wait_rows(slot)
        r = slot % SUBLANES
        h_t = jnp.concatenate([h_rows[r:r + 1, LANES * s:LANES * (s + 1)] for s in range(SUBLANES)], axis=0)
        dense = jnp.zeros((SUBLANES, LANES), F32)
        for j in range(PEER_JGROUPS):
            prods = buf_ref[slot, SUBLANES * j:SUBLANES * (j + 1), 0:SUBLANES, :] * h_t[None]
            col = jnp.sum(_sublane_sums(prods, sub), axis=1, keepdims=True)
            dense = jnp.where(lane == j, col, dense)
            issue_rows(t_next, nslot, j * per, (j + 1) * per)
        act = _gelu_exact(dense)
        gcol = jnp.sum(jnp.where(lane_s == t % LANES, gate_ref[t // LANES], 0.0), axis=1, keepdims=True)
        for j in range(PEER_JGROUPS):
            rows = slice(SUBLANES * j, SUBLANES * (j + 1))
            wb_ref[rows, :] = jnp.broadcast_to(act[:, j:j + 1] * gcol[rows, :], (SUBLANES, LANES))

        accs = [jnp.zeros((SUBLANES, LANES), F32) for _ in range(4)]
        for k in range(PEER_SLOTS):
            wk = jnp.broadcast_to(wb_ref[k:k + 1, :], (SUBLANES, LANES))
            accs[k % 4] = accs[k % 4] + buf_ref[slot, k, SUBLANES:SLAB_ROWS, :] * wk
            if k % SUBLANES == SUBLANES - 1:
                j = k // SUBLANES
                issue_rows(t_next, nslot, PEER_SLOTS // 2 + j * per, PEER_SLOTS // 2 + (j + 1) * per)
        out = (accs[0] + accs[1]) + (accs[2] + accs[3])
        return jnp.concatenate([out[s:s + 1, :] for s in range(SUBLANES)], axis=1)

    def block(b, _):
        base = pl.multiple_of(b * PEER_RING, PEER_RING)
        rows = []
        for slot in range(PEER_RING):
            if slot % SUBLANES == 0:
                h_rows = h_ref[pl.ds(pl.multiple_of(base + slot, SUBLANES), SUBLANES), :]
            rows.append(token(base + slot, slot, h_rows))
        po_ref[pl.ds(base, PEER_RING), :] = jnp.concatenate(rows, axis=0)
        return 0

    lax.fori_loop(0, n_blocks, block, 0)
    for slot in range(PEER_AHEAD):
        wait_rows(slot)

    x2 = x1_ref[...] + g2_ref[0] * po_ref[...]
    o_ref[...] = _rms(x2, nf_ref[...])


def _peer_experts(eid, gates_b, h2, x1, gate2, nf, uv3, seq):
    t, d = h2.shape
    tt = PEER_TOK_TILE
    per_b = seq // tt
    tokblk = pl.BlockSpec((tt, d), lambda i: (i, 0))
    return pl.pallas_call(
        _peer_body,
        grid=(t // tt,),
        in_specs=[pl.BlockSpec((tt, PEER_SLOTS), lambda i: (i, 0), memory_space=pltpu.SMEM),
                  pl.BlockSpec((tt // LANES, PEER_SLOTS, LANES), lambda i: (i, 0, 0)),
                  tokblk, tokblk,
                  pl.BlockSpec((1, 1, d), lambda i: (i // per_b, 0, 0)),
                  pl.BlockSpec((1, d), lambda i: (0, 0)),
                  pl.BlockSpec(memory_space=pl.ANY)],
        out_specs=tokblk,
        out_shape=jax.ShapeDtypeStruct((t, d), F32),
        scratch_shapes=[pltpu.VMEM((PEER_RING, PEER_SLOTS, SLAB_ROWS, LANES), F32),
                        pltpu.VMEM((tt, d), F32),
                        pltpu.VMEM((PEER_SLOTS, LANES), F32),
                        pltpu.SemaphoreType.DMA((PEER_RING,))],
        compiler_params=pltpu.CompilerParams(dimension_semantics=("arbitrary",),
                                             vmem_limit_bytes=VMEM_LIMIT),
        name="peer_experts",
    )(eid, gates_b, h2, x1, gate2, nf, uv3)


def kernel(x, c, w_ada, b_ada, norm1_g, w_in, conv_w, conv_b, dt_bias, a_log, d_ssd, norm_ssd_g, s5_a_re, s5_a_im, s5_log_dt, s5_b_re, s5_b_im, s5_c_re, s5_c_im, s5_d, glu_w, glu_b, norm_s5_g, w_out, norm2_g, w_query, sub_keys, expert_u, expert_v, norm_f_g):
    bsz, seq, d = x.shape
    t = bsz * seq
    depth = w_ada.shape[0]
    xt = x.reshape(t, d)
    for l in range(depth):
        mod = _adaln_mod(c, w_ada[l], b_ada[l])
        shift1, scale1, gate1, shift2, scale2, gate2 = [m.reshape(bsz, 1, d) for m in jnp.split(mod, 6, axis=-1)]

        wz, wxbc, wdt, wu = jnp.split(w_in[l], [SSD_WIDTH, SSD_WIDTH + 2048, SSD_WIDTH + 2048 + SSD_HEADS], axis=1)
        wu = _s5_cols_to_hg(wu, axis=1)
        w_cat = jnp.concatenate([wz, wxbc, wu, jnp.pad(wdt, ((0, 0), (0, DT_PAD - SSD_HEADS)))], axis=1).astype(BF16)
        proj = _in_proj(xt, scale1, shift1, norm1_g[l].reshape(1, d), w_cat, seq)

        pad_h = lambda v: jnp.pad(v.astype(F32), (0, DT_PAD - SSD_HEADS)).reshape(1, DT_PAD)
        yssd = _ssd_mixer(proj, conv_w[l], conv_b[l].reshape(1, -1), pad_h(dt_bias[l]),
                          pad_h(-jnp.exp(a_log[l].astype(F32))),
                          jnp.repeat(d_ssd[l].astype(F32), SSD_HEAD_DIM).reshape(1, SSD_WIDTH),
                          norm_ssd_g[l].reshape(1, SSD_WIDTH), bsz, seq)

        blk5 = S5_CHUNK * S5_GROUP_CH
        u = proj[:, COL_U * 1024:(COL_U + 1) * 1024]
        xg = u.reshape(t * S5_GROUP_CH, S5_GROUPS).T.reshape(S5_GROUPS, t // S5_CHUNK, blk5)
        prep = _s5_prepare(s5_a_re[l], s5_a_im[l], s5_log_dt[l], s5_b_re[l], s5_b_im[l], s5_c_re[l],
                           s5_c_im[l], s5_d[l], glu_w[l], glu_b[l])
        og = _s5_mixer(xg, prep, bsz)
        s5o = og.reshape(S5_GROUPS, t * S5_GROUP_CH).T.reshape(t, S5_WIDTH)

        w_o = jnp.concatenate([w_out[l][:SSD_WIDTH], _s5_cols_to_hg(w_out[l][SSD_WIDTH:], axis=0)], axis=0)
        x1, h2, q = _out_proj(yssd, s5o, xt, gate1, scale2, shift2, _s5_cols_to_hg(norm_s5_g[l], axis=0).reshape(1, -1),
                              norm2_g[l].reshape(1, d), w_o.astype(BF16), w_query[l].astype(BF16), seq)

        keys = sub_keys[l].reshape(2 * PEER_HEADS, PEER_NKEYS, PEER_HALF).astype(BF16)
        eid_t, gates_b = _peer_route(q, keys)
        n_exp = expert_u.shape[1]
        uv3 = jnp.concatenate([expert_u[l].reshape(n_exp, SUBLANES, LANES),
                               expert_v[l].reshape(n_exp, SUBLANES, LANES)], axis=1)
        last = l == depth - 1
        assert last, "the final RMSNorm is fused into the last layer's PEER kernel"
        xt = _peer_experts(eid_t.T, gates_b, h2, x1, gate2, norm_f_g.reshape(1, d), uv3, seq)
    return xt.reshape(bsz, seq, d)
```

```python
import functools

import jax
import jax.numpy as jnp
from jax import lax
from jax.experimental import pallas as pl
from jax.experimental.pallas import tpu as pltpu

F32 = jnp.float32
BF16 = jnp.bfloat16
I32 = jnp.int32

D_MODEL = 1024
SSD_WIDTH = 1024
SSD_HEAD_DIM = 64
SSD_HEADS = 16
SSD_GROUPS = 4
SSD_STATE = 128
SSD_CONV = 4
SSD_CHUNK = 128
S5_WIDTH = 1024
S5_GROUP_CH = 16
S5_GROUPS = 64
S5_STATE = 64
S5_CHUNK = 16
PEER_HEADS = 8
PEER_NKEYS = 128
PEER_TOPK = 16
PEER_HALF = 128
PEER_SLOTS = PEER_HEADS * PEER_TOPK
EPS = 1e-6
LANES = 128
DT_PAD = LANES
PROJ_WIDTH = SSD_WIDTH + 2 * 1024 + S5_WIDTH + DT_PAD
COL_Z, COL_XS, COL_BC, COL_U = 0, 1, 2, 3
COL_DT = (4 * 1024) // DT_PAD
HIGHEST = lax.Precision.HIGHEST
VMEM_LIMIT = 56 * 1024 * 1024


def _silu(v):
    return v * jax.nn.sigmoid(v)


def _gelu_exact(v):
    return 0.5 * v * (1.0 + lax.erf(v * (2.0 ** -0.5)))


def _rms(v, g):
    return v * lax.rsqrt(jnp.mean(v * v, axis=-1, keepdims=True) + EPS) * g


def _mod_body(c_ref, w_ref, b_ref, o_ref):
    o_ref[...] = jnp.dot(_silu(c_ref[...]), w_ref[...], preferred_element_type=F32) + b_ref[...]


def _adaln_mod(c, w_ada, b_ada):
    bsz, d = c.shape
    n = w_ada.shape[1]
    tn = 1024
    return pl.pallas_call(
        _mod_body,
        grid=(n // tn,),
        in_specs=[pl.BlockSpec((bsz, d), lambda j: (0, 0)),
                  pl.BlockSpec((d, tn), lambda j: (0, j)),
                  pl.BlockSpec((1, tn), lambda j: (0, j))],
        out_specs=pl.BlockSpec((bsz, tn), lambda j: (0, j)),
        out_shape=jax.ShapeDtypeStruct((bsz, n), F32),
        name="adaln_mod",
    )(c, w_ada, b_ada.reshape(1, n))


def _inproj_body(x_ref, sc_ref, sh_ref, g_ref, w_ref, o_ref):
    h = _rms(x_ref[...], g_ref[...]) * (1.0 + sc_ref[0]) + sh_ref[0]
    o_ref[...] = jnp.dot(h.astype(BF16), w_ref[...], preferred_element_type=F32)


def _in_proj(x2d, scale1, shift1, g1, w_cat, seq):
    t, d = x2d.shape
    n = w_cat.shape[1]
    tm = 256
    per_b = seq // tm
    return pl.pallas_call(
        _inproj_body,
        grid=(t // tm,),
        in_specs=[pl.BlockSpec((tm, d), lambda i: (i, 0)),
                  pl.BlockSpec((1, 1, d), lambda i: (i // per_b, 0, 0)),
                  pl.BlockSpec((1, 1, d), lambda i: (i // per_b, 0, 0)),
                  pl.BlockSpec((1, d), lambda i: (0, 0)),
                  pl.BlockSpec((d, n), lambda i: (0, 0))],
        out_specs=pl.BlockSpec((tm, n), lambda i: (i, 0)),
        out_shape=jax.ShapeDtypeStruct((t, n), F32),
        compiler_params=pltpu.CompilerParams(dimension_semantics=("parallel",),
                                             vmem_limit_bytes=VMEM_LIMIT),
        name="in_proj",
    )(x2d, scale1, shift1, g1, w_cat)


HALO = 8
SSD_PAIRS = SSD_HEADS // 2


def _ssd_body(z_ref, xs_ref, bc_ref, dt_ref, cw_ref, cb_ref, dtb_ref, a_ref, d_ref, ng_ref,
              o_ref, win_ref, act_ref, state_ref, y_ref):
    c = pl.program_id(1)
    L = SSD_CHUNK

    @pl.when(c == 0)
    def _():
        win_ref[0:HALO, :] = jnp.zeros((HALO, 2048), F32)
        state_ref[...] = jnp.zeros(state_ref.shape, F32)

    @pl.when(c > 0)
    def _():
        win_ref[0:HALO, :] = win_ref[L:L + HALO, :]

    win_ref[HALO:HALO + L, 0:1024] = xs_ref[...]
    win_ref[HALO:HALO + L, 1024:2048] = bc_ref[...]

    for sl in range(8):
        cols = slice(sl * 256, (sl + 1) * 256)
        acc = jnp.broadcast_to(cb_ref[:, cols], (L, 256))
        for k in range(SSD_CONV):
            r0 = HALO - (SSD_CONV - 1) + k
            acc = acc + win_ref[r0:r0 + L, cols] * cw_ref[k:k + 1, cols]
        act_ref[:, cols] = _silu(acc)

    dtv = dt_ref[...] + dtb_ref[...]
    dt = jnp.maximum(dtv, 0.0) + jnp.log1p(jnp.exp(-jnp.abs(dtv)))
    da = dt * a_ref[...]
    row = lax.broadcasted_iota(I32, (L, L), 0)
    col = lax.broadcasted_iota(I32, (L, L), 1)
    causal = row >= col
    tril = jnp.where(causal, 1.0, 0.0).astype(F32)
    cs = jnp.dot(tril, da, precision=HIGHEST, preferred_element_type=F32)
    cs_t = cs.T
    dec_t = jnp.exp(cs_t[:, L - 1:L] - cs_t)
    ecs = jnp.exp(cs)
    lane = lax.broadcasted_iota(I32, (L, LANES), 1)
    first = lane < SSD_HEAD_DIM

    for g in range(SSD_GROUPS):
        bg = act_ref[:, 1024 + g * 128:1024 + (g + 1) * 128]
        cg = act_ref[:, 1536 + g * 128:1536 + (g + 1) * 128]
        bg_t = bg.T
        cg_b = cg.astype(BF16)
        cb = jnp.dot(cg_b, bg_t.astype(BF16), preferred_element_type=F32)
        for r in range(2):
            p = g * 2 + r
            h0, h1 = 2 * p, 2 * p + 1
            x2 = act_ref[:, p * 128:(p + 1) * 128]
            dt2 = jnp.where(first, dt[:, h0:h0 + 1], dt[:, h1:h1 + 1])
            xdt = (x2 * dt2).astype(BF16)
            ydiag, snew = [], []
            for h in (h0, h1):
                diff = cs[:, h:h + 1] - cs_t[h:h + 1, :]
                lmat = jnp.where(causal, jnp.exp(jnp.where(causal, diff, 0.0)), 0.0)
                ydiag.append(jnp.dot((cb * lmat).astype(BF16), xdt, preferred_element_type=F32))
                bw = (bg_t * dec_t[h:h + 1, :]).astype(BF16)
                snew.append(jnp.dot(bw, xdt, preferred_element_type=F32))
            prev = state_ref[p]
            yoff = jnp.dot(cg_b, prev.astype(BF16), preferred_element_type=F32)
            yoff = yoff * jnp.where(first, ecs[:, h0:h0 + 1], ecs[:, h1:h1 + 1])
            cdec = jnp.where(first[0:1, :], ecs[L - 1:L, h0:h0 + 1], ecs[L - 1:L, h1:h1 + 1])
            state_ref[p] = prev * cdec + jnp.where(first, snew[0], snew[1])
            y2 = jnp.where(first, ydiag[0], ydiag[1]) + yoff + d_ref[:, p * 128:(p + 1) * 128] * x2
            y_ref[:, p * 128:(p + 1) * 128] = y2

    gw = SSD_WIDTH // SSD_GROUPS
    for g in range(SSD_GROUPS):
        cols = slice(g * gw, (g + 1) * gw)
        yg = y_ref[:, cols] * _silu(z_ref[:, cols])
        o_ref[:, cols] = _rms(yg, ng_ref[:, cols])


def _ssd_mixer(proj, conv_w, conv_b, dt_bias_p, a_p, d_exp, norm_g, bsz, seq):
    t = proj.shape[0]
    L = SSD_CHUNK
    nc = seq // L
    row = lambda b, c: b * nc + c
    full = lambda shape: pl.BlockSpec(shape, lambda b, c: (0, 0))
    return pl.pallas_call(
        _ssd_body,
        grid=(bsz, nc),
        in_specs=[pl.BlockSpec((L, 1024), lambda b, c: (row(b, c), COL_Z)),
                  pl.BlockSpec((L, 1024), lambda b, c: (row(b, c), COL_XS)),
                  pl.BlockSpec((L, 1024), lambda b, c: (row(b, c), COL_BC)),
                  pl.BlockSpec((L, DT_PAD), lambda b, c: (row(b, c), COL_DT)),
                  full((SSD_CONV, 2048)), full((1, 2048)), full((1, DT_PAD)), full((1, DT_PAD)),
                  full((1, 1024)), full((1, 1024))],
        out_specs=pl.BlockSpec((L, 1024), lambda b, c: (row(b, c), 0)),
        out_shape=jax.ShapeDtypeStruct((t, SSD_WIDTH), F32),
        scratch_shapes=[pltpu.VMEM((L + HALO, 2048), F32),
                        pltpu.VMEM((L, 2048), F32),
                        pltpu.VMEM((SSD_PAIRS, SSD_STATE, LANES), F32),
                        pltpu.VMEM((L, 1024), F32)],
        compiler_params=pltpu.CompilerParams(dimension_semantics=("parallel", "arbitrary"),
                                             vmem_limit_bytes=VMEM_LIMIT),
        name="ssd_mixer",
    )(proj, proj, proj, proj, conv_w, conv_b, dt_bias_p, a_p, d_exp, norm_g)


def _s5_cols_to_hg(w, axis):
    shape = w.shape
    split = shape[:axis] + (S5_GROUPS, S5_GROUP_CH) + shape[axis + 1:]
    return jnp.swapaxes(w.reshape(split), axis, axis + 1).reshape(shape)


def _s5_prepare(a_re, a_im, log_dt, b_re, b_im, c_re, c_im, d_s5, glu_w, glu_b):
    G, P, H, C = S5_GROUPS, S5_STATE, S5_GROUP_CH, S5_CHUNK
    lr, li = a_re.astype(F32), a_im.astype(F32)
    dt = jnp.exp(log_dt.astype(F32))[:, None]
    mag = jnp.exp(lr * dt)
    lb_re, lb_im = mag * jnp.cos(li * dt), mag * jnp.sin(li * dt)
    den = lr * lr + li * li
    coef_re = ((lb_re - 1.0) * lr + lb_im * li) / den
    coef_im = (lb_im * lr - (lb_re - 1.0) * li) / den
    bb_re = coef_re[..., None] * b_re - coef_im[..., None] * b_im
    bb_im = coef_re[..., None] * b_im + coef_im[..., None] * b_re
    tau = jnp.arange(C + 1, dtype=F32)[:, None, None]
    pm = jnp.exp(lr * dt * tau)
    pw_re, pw_im = pm * jnp.cos(li * dt * tau), pm * jnp.sin(li * dt * tau)
    m_re = pw_re[..., None] * bb_re - pw_im[..., None] * bb_im
    m_im = pw_re[..., None] * bb_im + pw_im[..., None] * bb_re
    e_re = c_re * pw_re[:, :, None, :] - c_im * pw_im[:, :, None, :]
    e_im = c_re * pw_im[:, :, None, :] + c_im * pw_re[:, :, None, :]
    kern = (jnp.einsum('ghp,tgpk->tghk', c_re, m_re[:C], precision=HIGHEST)
            - jnp.einsum('ghp,tgpk->tghk', c_im, m_im[:C], precision=HIGHEST))
    j_in = jnp.arange(C)[:, None]
    j_out = jnp.arange(C)[None, :]
    lag = j_out - j_in
    toep = jnp.where((lag >= 0)[:, :, None, None, None], kern[jnp.clip(lag, 0, C - 1)], 0.0)
    toep = toep.transpose(2, 0, 4, 1, 3).reshape(G, C * H, C * H)
    rev = jnp.arange(C - 1, -1, -1)
    wst_re = m_re[rev].transpose(1, 0, 3, 2).reshape(G, C * H, P)
    wst_im = m_im[rev].transpose(1, 0, 3, 2).reshape(G, C * H, P)
    wout_re = e_re[1:].transpose(1, 3, 0, 2).reshape(G, P, C * H)
    wout_im = (-e_im[1:]).transpose(1, 3, 0, 2).reshape(G, P, C * H)
    padp = lambda v, axis: jnp.pad(v, [(0, LANES - P) if a == axis else (0, 0) for a in range(v.ndim)])
    lam_re, lam_im = padp(pw_re[C][:, None, :], 2), padp(pw_im[C][:, None, :], 2)
    dvec = jnp.tile(d_s5.astype(F32), (1, C))[:, None, :]
    eye = jnp.eye(C, dtype=F32)
    wglu = (eye[None, :, None, :, None] * glu_w.astype(F32)[:, None, :, None, :]).reshape(G, C * H, C * H)
    bglu = jnp.tile(glu_b.astype(F32), (1, C))[:, None, :]
    return (toep.astype(BF16), padp(wst_re, 2).astype(BF16), padp(wst_im, 2).astype(BF16), lam_re, lam_im,
            padp(wout_re, 1).astype(BF16), padp(wout_im, 1).astype(BF16), dvec, wglu.astype(BF16), bglu)


def _s5_body(nb, x_ref, toep_ref, wsr_ref, wsi_ref, lr_ref, li_ref, wor_ref, woi_ref, d_ref,
             wg_ref, bg_ref, o_ref, sr_ref, si_ref, pr_ref, pi_ref):
    x = x_ref[0]
    xb = x.astype(BF16)
    sr_ref[...] = jnp.dot(xb, wsr_ref[0], preferred_element_type=F32)
    si_ref[...] = jnp.dot(xb, wsi_ref[0], preferred_element_type=F32)
    lam_r = jnp.broadcast_to(lr_ref[0], (nb, LANES))
    lam_i = jnp.broadcast_to(li_ref[0], (nb, LANES))
    n_chunks = x.shape[0] // nb

    def step(c, carry):
        xr, xi = carry
        rows = pl.ds(c, nb, stride=n_chunks)
        pr_ref[rows, :] = xr
        pi_ref[rows, :] = xi
        nr = lam_r * xr - lam_i * xi + sr_ref[rows, :]
        ni = lam_r * xi + lam_i * xr + si_ref[rows, :]
        return nr, ni

    zero = jnp.zeros((nb, LANES), F32)
    lax.fori_loop(0, n_chunks, step, (zero, zero))
    y = (jnp.dot(xb, toep_ref[0], preferred_element_type=F32)
         + jnp.dot(pr_ref[...].astype(BF16), wor_ref[0], preferred_element_type=F32)
         + jnp.dot(pi_ref[...].astype(BF16), woi_ref[0], preferred_element_type=F32)
         + d_ref[0] * x)
    v = _gelu_exact(y)
    gate = jax.nn.sigmoid(jnp.dot(v.astype(BF16), wg_ref[0], preferred_element_type=F32) + bg_ref[0])
    o_ref[0] = v * gate


def _s5_mixer(xg, prep, nb):
    G, R, W = xg.shape
    toep, wsr, wsi, lam_re, lam_im, wor, woi, dvec, wglu, bglu = prep
    P = LANES
    per_g = lambda shape: pl.BlockSpec((1,) + shape, lambda g: (g, 0, 0))
    return pl.pallas_call(
        functools.partial(_s5_body, nb),
        grid=(G,),
        in_specs=[per_g((R, W)), per_g((W, W)), per_g((W, P)), per_g((W, P)), per_g((1, P)), per_g((1, P)),
                  per_g((P, W)), per_g((P, W)), per_g((1, W)), per_g((W, W)), per_g((1, W))],
        out_specs=per_g((R, W)),
        out_shape=jax.ShapeDtypeStruct((G, R, W), F32),
        scratch_shapes=[pltpu.VMEM((R, P), F32)] * 4,
        compiler_params=pltpu.CompilerParams(dimension_semantics=("parallel",),
                                             vmem_limit_bytes=VMEM_LIMIT),
        name="s5_mixer",
    )(xg, toep, wsr, wsi, lam_re, lam_im, wor, woi, dvec, wglu, bglu)


def _outproj_body(ys_ref, s5_ref, x_ref, g1_ref, sc2_ref, sh2_ref, ns5_ref, n2_ref, wo_ref, wq_ref,
                  x1_ref, h2_ref, q_ref):
    a = ys_ref[...].astype(BF16)
    b = _rms(s5_ref[...], ns5_ref[...]).astype(BF16)
    mix = (jnp.dot(a, wo_ref[0:SSD_WIDTH, :], preferred_element_type=F32)
           + jnp.dot(b, wo_ref[SSD_WIDTH:, :], preferred_element_type=F32))
    x1 = x_ref[...] + g1_ref[0] * mix
    x1_ref[...] = x1
    h2 = _rms(x1, n2_ref[...]) * (1.0 + sc2_ref[0]) + sh2_ref[0]
    h2_ref[...] = h2
    q = jnp.dot(h2.astype(BF16), wq_ref[...], preferred_element_type=F32)
    for hc in range(2 * PEER_HEADS):
        q_ref[hc] = q[:, hc * PEER_HALF:(hc + 1) * PEER_HALF]


def _out_proj(yssd, s5o, x2d, gate1, scale2, shift2, ns5, n2, w_out, w_query, seq):
    t, d = x2d.shape
    nhc = w_query.shape[1] // PEER_HALF
    tm = 256
    per_b = seq // tm
    rowblk = lambda w: pl.BlockSpec((tm, w), lambda i: (i, 0))
    modblk = pl.BlockSpec((1, 1, d), lambda i: (i // per_b, 0, 0))
    full = lambda shape: pl.BlockSpec(shape, lambda i: (0, 0))
    return pl.pallas_call(
        _outproj_body,
        grid=(t // tm,),
        in_specs=[rowblk(d), rowblk(d), rowblk(d), modblk, modblk, modblk, full((1, d)), full((1, d)),
                  full(w_out.shape), full(w_query.shape)],
        out_specs=[rowblk(d), rowblk(d), pl.BlockSpec((nhc, tm, PEER_HALF), lambda i: (0, i, 0))],
        out_shape=[jax.ShapeDtypeStruct((t, d), F32), jax.ShapeDtypeStruct((t, d), F32),
                   jax.ShapeDtypeStruct((nhc, t, PEER_HALF), F32)],
        compiler_params=pltpu.CompilerParams(dimension_semantics=("parallel",),
                                             vmem_limit_bytes=VMEM_LIMIT),
        name="out_proj_query",
    )(yssd, s5o, x2d, gate1, scale2, shift2, ns5, n2, w_out, w_query)


def _topk_rows(s, n_out, payload=None):
    n = s.shape[0]
    iota = lax.broadcasted_iota(I32, s.shape, 0)
    vals, picks = [], []
    for _ in range(n_out):
        m = jnp.max(s, axis=0, keepdims=True)
        am = jnp.min(jnp.where(s == m, iota, n), axis=0, keepdims=True)
        hit = iota == am
        vals.append(m)
        picks.append(am if payload is None else jnp.max(jnp.where(hit, payload, -1), axis=0, keepdims=True))
        s = jnp.where(hit, -jnp.inf, s)
    return jnp.concatenate(vals, axis=0), jnp.concatenate(picks, axis=0)


def _route_half(q_blk, key_blk):
    st = lax.dot_general(key_blk, q_blk.astype(BF16), (((1,), (1,)), ((), ())), preferred_element_type=F32)
    return _topk_rows(st, PEER_TOPK)


def _route_head(s1, i1, s2, i2):
    K = PEER_TOPK
    cs = [s1[0:1, :] + s2] + [s1[a:a + 1, :] + s2[0:8, :] for a in range(1, 8)] + [s1[8:K, :] + s2[0:1, :]]
    ci = ([i1[0:1, :] * PEER_NKEYS + i2] + [i1[a:a + 1, :] * PEER_NKEYS + i2[0:8, :] for a in range(1, 8)]
          + [i1[8:K, :] * PEER_NKEYS + i2[0:1, :]])
    top_s, ids = _topk_rows(jnp.concatenate(cs, axis=0), K, payload=jnp.concatenate(ci, axis=0))
    e = jnp.exp(top_s - top_s[0:1, :])
    return e / jnp.sum(e, axis=0, keepdims=True), ids


SUBLANES = 8
PEER_TOK_TILE = 512
PEER_RING = 16
PEER_AHEAD = 8
PEER_JGROUPS = PEER_SLOTS // SUBLANES
SLAB_ROWS = 2 * SUBLANES


def _sublane_sums(p, sub):
    m4, m2, m1 = sub < 4, (sub & 2) == 0, (sub & 1) == 0
    z = jnp.where(m4, p[0:4], p[4:8]) + pltpu.roll(jnp.where(m4, p[4:8], p[0:4]), 4, 1)
    y = jnp.where(m2, z[0:2] + pltpu.roll(z[0:2], 6, 1), z[2:4] + pltpu.roll(z[2:4], 2, 1))
    return jnp.where(m1, y[0] + pltpu.roll(y[0], 7, 0), y[1] + pltpu.roll(y[1], 1, 0))


def _peer_body(q_ref, keys_ref, h_ref, x1_ref, g2_ref, nf_ref, uv_ref, o_ref,
               buf_ref, po_ref, wb_ref, ts_ref, ti_ref, gates_ref, eidv_ref, eidt_ref, eid_smem, sem, sem_eid):
    i = pl.program_id(0)
    n_tiles = pl.num_programs(0) - 1
    wslot = i % 2
    rslot = 1 - wslot
    n_blocks = PEER_TOK_TILE // PEER_RING
    assert PEER_TOK_TILE // LANES * PEER_HEADS == n_blocks
    sub = lax.broadcasted_iota(I32, (SUBLANES, LANES), 0)
    lane = lax.broadcasted_iota(I32, (SUBLANES, LANES), 1)
    lane_s = lax.broadcasted_iota(I32, (PEER_SLOTS, LANES), 1)

    def route_unit(b):
        rb, h = b // PEER_HEADS, b % PEER_HEADS
        toks = pl.ds(pl.multiple_of(rb * LANES, LANES), LANES)
        for c in range(2):
            hc = 2 * h + c
            v, k = _route_half(q_ref[hc, toks, :], keys_ref[hc])
            ts_ref[c] = v
            ti_ref[c] = k
        gates, ids = _route_head(ts_ref[0], ti_ref[0], ts_ref[1], ti_ref[1])
        rows = pl.ds(pl.multiple_of(h * PEER_TOPK, PEER_TOPK), PEER_TOPK)
        gates_ref[wslot, rb, rows, :] = gates
        eidv_ref[rows, :] = ids

        @pl.when(h == PEER_HEADS - 1)
        def _():
            eidt_ref[...] = eidv_ref[...].astype(F32).T.astype(I32)
            cp = pltpu.make_async_copy(eidt_ref, eid_smem.at[wslot, toks, :], sem_eid)
            cp.start()
            cp.wait()

    def issue_rows(t, slot, k0, k1):
        for k in range(k0, k1):
            e = eid_smem[rslot, t, k]
            pltpu.make_async_copy(uv_ref.at[e], buf_ref.at[slot, k], sem.at[slot]).start(priority=k % 2)

    def wait_rows(slot):
        pltpu.make_async_copy(uv_ref.at[pl.ds(0, PEER_SLOTS)], buf_ref.at[slot], sem.at[slot]).wait()

    def token(t, slot, h_rows):
        t_next = (t + PEER_AHEAD) % PEER_TOK_TILE
        nslot = (slot + PEER_AHEAD) % PEER_RING
        per = PEER_SLOTS // 2 // PEER_JGROUPS
        wait_rows(slot)
        r = slot % SUBLANES
        h_t = jnp.concatenate([h_rows[r:r + 1, LANES * s:LANES * (s + 1)] for s in range(SUBLANES)], axis=0)
        dense = jnp.zeros((SUBLANES, LANES), F32)
        for j in range(PEER_JGROUPS):
            prods = buf_ref[slot, SUBLANES * j:SUBLANES * (j + 1), 0:SUBLANES, :] * h_t[None]
            col = jnp.sum(_sublane_sums(prods, sub), axis=1, keepdims=True)
            dense = jnp.where(lane == j, col, dense)
            issue_rows(t_next, nslot, j * per, (j + 1) * per)
        act = _gelu_exact(dense)
        gblk = gates_ref[rslot, t // LANES]
        gcol = jnp.sum(jnp.where(lane_s == t % LANES, gblk, 0.0), axis=1, keepdims=True)
        for j in range(PEER_JGROUPS):
            rows = slice(SUBLANES * j, SUBLANES * (j + 1))
            wb_ref[rows, :] = jnp.broadcast_to(act[:, j:j + 1] * gcol[rows, :], (SUBLANES, LANES))

        accs = [jnp.zeros((SUBLANES, LANES), F32) for _ in range(4)]
        for k in range(PEER_SLOTS):
            wk = jnp.broadcast_to(wb_ref[k:k + 1, :], (SUBLANES, LANES))
            accs[k % 4] = accs[k % 4] + buf_ref[slot, k, SUBLANES:SLAB_ROWS, :] * wk
            if k % SUBLANES == SUBLANES - 1:
                j = k // SUBLANES
                issue_rows(t_next, nslot, PEER_SLOTS // 2 + j * per, PEER_SLOTS // 2 + (j + 1) * per)
        out = (accs[0] + accs[1]) + (accs[2] + accs[3])
        return jnp.concatenate([out[s:s + 1, :] for s in range(SUBLANES)], axis=1)

    evaluate = i > 0
    routing = i < n_tiles

    @pl.when(evaluate)
    def _():
        def prologue(t, _):
            issue_rows(t, t, 0, PEER_SLOTS)
            return 0

        lax.fori_loop(0, PEER_AHEAD, prologue, 0)

    def block(b, _):
        @pl.when(evaluate)
        def _():
            base = pl.multiple_of(b * PEER_RING, PEER_RING)
            rows = []
            for slot in range(PEER_RING):
                if slot % SUBLANES == 0:
                    h_rows = h_ref[pl.ds(pl.multiple_of(base + slot, SUBLANES), SUBLANES), :]
                rows.append(token(base + slot, slot, h_rows))
            po_ref[pl.ds(base, PEER_RING), :] = jnp.concatenate(rows, axis=0)

        @pl.when(routing)
        def _():
            route_unit(b)

        return 0

    lax.fori_loop(0, n_blocks, block, 0)

    @pl.when(evaluate)
    def _():
        for slot in range(PEER_AHEAD):
            wait_rows(slot)
        x2 = x1_ref[...] + g2_ref[0] * po_ref[...]
        o_ref[...] = _rms(x2, nf_ref[...])


def _peer_ffn(q, keys, h2, x1, gate2, nf, uv3, seq):
    t, d = h2.shape
    nhc, _, dh = q.shape
    tt = PEER_TOK_TILE
    n_tiles = t // tt
    per_b = seq // tt
    prev = lambda i: jnp.maximum(i - 1, 0)
    tokblk = pl.BlockSpec((tt, d), lambda i: (prev(i), 0))
    return pl.pallas_call(
        _peer_body,
        grid=(n_tiles + 1,),
        in_specs=[pl.BlockSpec((nhc, tt, dh), lambda i: (0, jnp.minimum(i, n_tiles - 1), 0)),
                  pl.BlockSpec(keys.shape, lambda i: (0, 0, 0)),
                  tokblk, tokblk,
                  pl.BlockSpec((1, 1, d), lambda i: (prev(i) // per_b, 0, 0)),
                  pl.BlockSpec((1, d), lambda i: (0, 0)),
                  pl.BlockSpec(memory_space=pl.ANY)],
        out_specs=tokblk,
        out_shape=jax.ShapeDtypeStruct((t, d), F32),
        scratch_shapes=[pltpu.VMEM((PEER_RING, PEER_SLOTS, SLAB_ROWS, LANES), F32),
                        pltpu.VMEM((tt, d), F32),
                        pltpu.VMEM((PEER_SLOTS, LANES), F32),
                        pltpu.VMEM((2, PEER_TOPK, LANES), F32),
                        pltpu.VMEM((2, PEER_TOPK, LANES), I32),
                        pltpu.VMEM((2, tt // LANES, PEER_SLOTS, LANES), F32),
                        pltpu.VMEM((PEER_SLOTS, LANES), I32),
                        pltpu.VMEM((LANES, PEER_SLOTS), I32),
                        pltpu.SMEM((2, tt, PEER_SLOTS), I32),
                        pltpu.SemaphoreType.DMA((PEER_RING,)),
                        pltpu.SemaphoreType.DMA(())],
        compiler_params=pltpu.CompilerParams(dimension_semantics=("arbitrary",),
                                             vmem_limit_bytes=VMEM_LIMIT),
        name="peer_ffn",
    )(q, keys, h2, x1, gate2, nf, uv3)


def kernel(x, c, w_ada, b_ada, norm1_g, w_in, conv_w, conv_b, dt_bias, a_log, d_ssd, norm_ssd_g, s5_a_re, s5_a_im, s5_log_dt, s5_b_re, s5_b_im, s5_c_re, s5_c_im, s5_d, glu_w, glu_b, norm_s5_g, w_out, norm2_g, w_query, sub_keys, expert_u, expert_v, norm_f_g):
    bsz, seq, d = x.shape
    t = bsz * seq
    depth = w_ada.shape[0]
    xt = x.reshape(t, d)
    for l in range(depth):
        mod = _adaln_mod(c, w_ada[l], b_ada[l])
        shift1, scale1, gate1, shift2, scale2, gate2 = [m.reshape(bsz, 1, d) for m in jnp.split(mod, 6, axis=-1)]

        wz, wxbc, wdt, wu = jnp.split(w_in[l], [SSD_WIDTH, SSD_WIDTH + 2048, SSD_WIDTH + 2048 + SSD_HEADS], axis=1)
        wu = _s5_cols_to_hg(wu, axis=1)
        w_cat = jnp.concatenate([wz, wxbc, wu, jnp.pad(wdt, ((0, 0), (0, DT_PAD - SSD_HEADS)))], axis=1).astype(BF16)
        proj = _in_proj(xt, scale1, shift1, norm1_g[l].reshape(1, d), w_cat, seq)

        pad_h = lambda v: jnp.pad(v.astype(F32), (0, DT_PAD - SSD_HEADS)).reshape(1, DT_PAD)
        yssd = _ssd_mixer(proj, conv_w[l], conv_b[l].reshape(1, -1), pad_h(dt_bias[l]),
                          pad_h(-jnp.exp(a_log[l].astype(F32))),
                          jnp.repeat(d_ssd[l].astype(F32), SSD_HEAD_DIM).reshape(1, SSD_WIDTH),
                          norm_ssd_g[l].reshape(1, SSD_WIDTH), bsz, seq)

        blk5 = S5_CHUNK * S5_GROUP_CH
        u = proj[:, COL_U * 1024:(COL_U + 1) * 1024]
        xg = u.reshape(t * S5_GROUP_CH, S5_GROUPS).T.reshape(S5_GROUPS, t // S5_CHUNK, blk5)
        prep = _s5_prepare(s5_a_re[l], s5_a_im[l], s5_log_dt[l], s5_b_re[l], s5_b_im[l], s5_c_re[l],
                           s5_c_im[l], s5_d[l], glu_w[l], glu_b[l])
        og = _s5_mixer(xg, prep, bsz)
        s5o = og.reshape(S5_GROUPS, t * S5_GROUP_CH).T.reshape(t, S5_WIDTH)

        w_o = jnp.concatenate([w_out[l][:SSD_WIDTH], _s5_cols_to_hg(w_out[l][SSD_WIDTH:], axis=0)], axis=0)
        x1, h2, q = _out_proj(yssd, s5o, xt, gate1, scale2, shift2, _s5_cols_to_hg(norm_s5_g[l], axis=0).reshape(1, -1),
                              norm2_g[l].reshape(1, d), w_o.astype(BF16), w_query[l].astype(BF16), seq)

        keys = sub_keys[l].reshape(2 * PEER_HEADS, PEER_NKEYS, PEER_HALF).astype(BF16)
        n_exp = expert_u.shape[1]
        uv3 = jnp.concatenate([expert_u[l].reshape(n_exp, SUBLANES, LANES),
                               expert_v[l].reshape(n_exp, SUBLANES, LANES)], axis=1)
        last = l == depth - 1
        assert last, "the final RMSNorm is fused into the last layer's PEER kernel"
        xt = _peer_ffn(q, keys, h2, x1, gate2, norm_f_g.reshape(1, d), uv3, seq)
    return xt.reshape(bsz, seq, d)
```

```python
import functools

import jax
import jax.numpy as jnp
from jax import lax
from jax.experimental import pallas as pl
from jax.experimental.pallas import tpu as pltpu

F32 = jnp.float32
BF16 = jnp.bfloat16
I32 = jnp.int32

D_MODEL = 1024
SSD_WIDTH = 1024
SSD_HEAD_DIM = 64
SSD_HEADS = 16
SSD_GROUPS = 4
SSD_STATE = 128
SSD_CONV = 4
SSD_CHUNK = 128
S5_WIDTH = 1024
S5_GROUP_CH = 16
S5_GROUPS = 64
S5_STATE = 64
S5_CHUNK = 16
PEER_HEADS = 8
PEER_NKEYS = 128
PEER_TOPK = 16
PEER_HALF = 128
PEER_SLOTS = PEER_HEADS * PEER_TOPK
EPS = 1e-6
LANES = 128
DT_PAD = LANES
PROJ_WIDTH = SSD_WIDTH + 2 * 1024 + S5_WIDTH + DT_PAD
COL_Z, COL_XS, COL_BC, COL_U = 0, 1, 2, 3
COL_DT = (4 * 1024) // DT_PAD
HIGHEST = lax.Precision.HIGHEST
VMEM_LIMIT = 56 * 1024 * 1024


def _silu(v):
    return v * jax.nn.sigmoid(v)


def _gelu_exact(v):
    return 0.5 * v * (1.0 + lax.erf(v * (2.0 ** -0.5)))


def _rms(v, g):
    return v * lax.rsqrt(jnp.mean(v * v, axis=-1, keepdims=True) + EPS) * g


def _mod_body(c_ref, w_ref, b_ref, o_ref):
    o_ref[...] = jnp.dot(_silu(c_ref[...]), w_ref[...], preferred_element_type=F32) + b_ref[...]


def _adaln_mod(c, w_ada, b_ada):
    bsz, d = c.shape
    n = w_ada.shape[1]
    tn = 1024
    return pl.pallas_call(
        _mod_body,
        grid=(n // tn,),
        in_specs=[pl.BlockSpec((bsz, d), lambda j: (0, 0)),
                  pl.BlockSpec((d, tn), lambda j: (0, j)),
                  pl.BlockSpec((1, tn), lambda j: (0, j))],
        out_specs=pl.BlockSpec((bsz, tn), lambda j: (0, j)),
        out_shape=jax.ShapeDtypeStruct((bsz, n), F32),
        name="adaln_mod",
    )(c, w_ada, b_ada.reshape(1, n))


def _inproj_body(x_ref, sc_ref, sh_ref, g_ref, w_ref, o_ref):
    h = _rms(x_ref[...], g_ref[...]) * (1.0 + sc_ref[0]) + sh_ref[0]
    o_ref[...] = jnp.dot(h.astype(BF16), w_ref[...], preferred_element_type=F32)


def _in_proj(x2d, scale1, shift1, g1, w_cat, seq):
    t, d = x2d.shape
    n = w_cat.shape[1]
    tm = 256
    per_b = seq // tm
    return pl.pallas_call(
        _inproj_body,
        grid=(t // tm,),
        in_specs=[pl.BlockSpec((tm, d), lambda i: (i, 0)),
                  pl.BlockSpec((1, 1, d), lambda i: (i // per_b, 0, 0)),
                  pl.BlockSpec((1, 1, d), lambda i: (i // per_b, 0, 0)),
                  pl.BlockSpec((1, d), lambda i: (0, 0)),
                  pl.BlockSpec((d, n), lambda i: (0, 0))],
        out_specs=pl.BlockSpec((tm, n), lambda i: (i, 0)),
        out_shape=jax.ShapeDtypeStruct((t, n), F32),
        compiler_params=pltpu.CompilerParams(dimension_semantics=("parallel",),
                                             vmem_limit_bytes=VMEM_LIMIT),
        name="in_proj",
    )(x2d, scale1, shift1, g1, w_cat)


HALO = 8
SSD_PAIRS = SSD_HEADS // 2


def _ssd_body(z_ref, xs_ref, bc_ref, dt_ref, cw_ref, cb_ref, dtb_ref, a_ref, d_ref, ng_ref,
              o_ref, win_ref, act_ref, state_ref, y_ref):
    c = pl.program_id(1)
    L = SSD_CHUNK

    @pl.when(c == 0)
    def _():
        win_ref[0:HALO, :] = jnp.zeros((HALO, 2048), F32)
        state_ref[...] = jnp.zeros(state_ref.shape, F32)

    @pl.when(c > 0)
    def _():
        win_ref[0:HALO, :] = win_ref[L:L + HALO, :]

    win_ref[HALO:HALO + L, 0:1024] = xs_ref[...]
    win_ref[HALO:HALO + L, 1024:2048] = bc_ref[...]

    for sl in range(8):
        cols = slice(sl * 256, (sl + 1) * 256)
        acc = jnp.broadcast_to(cb_ref[:, cols], (L, 256))
        for k in range(SSD_CONV):
            r0 = HALO - (SSD_CONV - 1) + k
            acc = acc + win_ref[r0:r0 + L, cols] * cw_ref[k:k + 1, cols]
        act_ref[:, cols] = _silu(acc)

    dtv = dt_ref[...] + dtb_ref[...]
    dt = jnp.maximum(dtv, 0.0) + jnp.log1p(jnp.exp(-jnp.abs(dtv)))
    da = dt * a_ref[...]
    row = lax.broadcasted_iota(I32, (L, L), 0)
    col = lax.broadcasted_iota(I32, (L, L), 1)
    causal = row >= col
    tril = jnp.where(causal, 1.0, 0.0).astype(F32)
    cs = jnp.dot(tril, da, precision=HIGHEST, preferred_element_type=F32)
    cs_t = cs.T
    dec_t = jnp.exp(cs_t[:, L - 1:L] - cs_t)
    ecs = jnp.exp(cs)
    lane = lax.broadcasted_iota(I32, (L, LANES), 1)
    first = lane < SSD_HEAD_DIM

    for g in range(SSD_GROUPS):
        bg = act_ref[:, 1024 + g * 128:1024 + (g + 1) * 128]
        cg = act_ref[:, 1536 + g * 128:1536 + (g + 1) * 128]
        bg_t = bg.T
        cg_b = cg.astype(BF16)
        cb = jnp.dot(cg_b, bg_t.astype(BF16), preferred_element_type=F32)
        for r in range(2):
            p = g * 2 + r
            h0, h1 = 2 * p, 2 * p + 1
            x2 = act_ref[:, p * 128:(p + 1) * 128]
            dt2 = jnp.where(first, dt[:, h0:h0 + 1], dt[:, h1:h1 + 1])
            xdt = (x2 * dt2).astype(BF16)
            ydiag, snew = [], []
            for h in (h0, h1):
                diff = cs[:, h:h + 1] - cs_t[h:h + 1, :]
                lmat = jnp.where(causal, jnp.exp(jnp.where(causal, diff, 0.0)), 0.0)
                ydiag.append(jnp.dot((cb * lmat).astype(BF16), xdt, preferred_element_type=F32))
                bw = (bg_t * dec_t[h:h + 1, :]).astype(BF16)
                snew.append(jnp.dot(bw, xdt, preferred_element_type=F32))
            prev = state_ref[p]
            yoff = jnp.dot(cg_b, prev.astype(BF16), preferred_element_type=F32)
            yoff = yoff * jnp.where(first, ecs[:, h0:h0 + 1], ecs[:, h1:h1 + 1])
            cdec = jnp.where(first[0:1, :], ecs[L - 1:L, h0:h0 + 1], ecs[L - 1:L, h1:h1 + 1])
            state_ref[p] = prev * cdec + jnp.where(first, snew[0], snew[1])
            y2 = jnp.where(first, ydiag[0], ydiag[1]) + yoff + d_ref[:, p * 128:(p + 1) * 128] * x2
            y_ref[:, p * 128:(p + 1) * 128] = y2

    gw = SSD_WIDTH // SSD_GROUPS
    for g in range(SSD_GROUPS):
        cols = slice(g * gw, (g + 1) * gw)
        yg = y_ref[:, cols] * _silu(z_ref[:, cols])
        o_ref[:, cols] = _rms(yg, ng_ref[:, cols])


def _ssd_mixer(proj, conv_w, conv_b, dt_bias_p, a_p, d_exp, norm_g, bsz, seq):
    t = proj.shape[0]
    L = SSD_CHUNK
    nc = seq // L
    row = lambda b, c: b * nc + c
    full = lambda shape: pl.BlockSpec(shape, lambda b, c: (0, 0))
    return pl.pallas_call(
        _ssd_body,
        grid=(bsz, nc),
        in_specs=[pl.BlockSpec((L, 1024), lambda b, c: (row(b, c), COL_Z)),
                  pl.BlockSpec((L, 1024), lambda b, c: (row(b, c), COL_XS)),
                  pl.BlockSpec((L, 1024), lambda b, c: (row(b, c), COL_BC)),
                  pl.BlockSpec((L, DT_PAD), lambda b, c: (row(b, c), COL_DT)),
                  full((SSD_CONV, 2048)), full((1, 2048)), full((1, DT_PAD)), full((1, DT_PAD)),
                  full((1, 1024)), full((1, 1024))],
        out_specs=pl.BlockSpec((L, 1024), lambda b, c: (row(b, c), 0)),
        out_shape=jax.ShapeDtypeStruct((t, SSD_WIDTH), F32),
        scratch_shapes=[pltpu.VMEM((L + HALO, 2048), F32),
                        pltpu.VMEM((L, 2048), F32),
                        pltpu.VMEM((SSD_PAIRS, SSD_STATE, LANES), F32),
                        pltpu.VMEM((L, 1024), F32)],
        compiler_params=pltpu.CompilerParams(dimension_semantics=("parallel", "arbitrary"),
                                             vmem_limit_bytes=VMEM_LIMIT),
        name="ssd_mixer",
    )(proj, proj, proj, proj, conv_w, conv_b, dt_bias_p, a_p, d_exp, norm_g)


def _s5_cols_to_hg(w, axis):
    shape = w.shape
    split = shape[:axis] + (S5_GROUPS, S5_GROUP_CH) + shape[axis + 1:]
    return jnp.swapaxes(w.reshape(split), axis, axis + 1).reshape(shape)


def _s5_prepare(a_re, a_im, log_dt, b_re, b_im, c_re, c_im, d_s5, glu_w, glu_b):
    G, P, H, C = S5_GROUPS, S5_STATE, S5_GROUP_CH, S5_CHUNK
    lr, li = a_re.astype(F32), a_im.astype(F32)
    dt = jnp.exp(log_dt.astype(F32))[:, None]
    mag = jnp.exp(lr * dt)
    lb_re, lb_im = mag * jnp.cos(li * dt), mag * jnp.sin(li * dt)
    den = lr * lr + li * li
    coef_re = ((lb_re - 1.0) * lr + lb_im * li) / den
    coef_im = (lb_im * lr - (lb_re - 1.0) * li) / den
    bb_re = coef_re[..., None] * b_re - coef_im[..., None] * b_im
    bb_im = coef_re[..., None] * b_im + coef_im[..., None] * b_re
    tau = jnp.arange(C + 1, dtype=F32)[:, None, None]
    pm = jnp.exp(lr * dt * tau)
    pw_re, pw_im = pm * jnp.cos(li * dt * tau), pm * jnp.sin(li * dt * tau)
    m_re = pw_re[..., None] * bb_re - pw_im[..., None] * bb_im
    m_im = pw_re[..., None] * bb_im + pw_im[..., None] * bb_re
    e_re = c_re * pw_re[:, :, None, :] - c_im * pw_im[:, :, None, :]
    e_im = c_re * pw_im[:, :, None, :] + c_im * pw_re[:, :, None, :]
    kern = (jnp.einsum('ghp,tgpk->tghk', c_re, m_re[:C], precision=HIGHEST)
            - jnp.einsum('ghp,tgpk->tghk', c_im, m_im[:C], precision=HIGHEST))
    j_in = jnp.arange(C)[:, None]
    j_out = jnp.arange(C)[None, :]
    lag = j_out - j_in
    toep = jnp.where((lag >= 0)[:, :, None, None, None], kern[jnp.clip(lag, 0, C - 1)], 0.0)
    toep = toep.transpose(2, 0, 4, 1, 3).reshape(G, C * H, C * H)
    rev = jnp.arange(C - 1, -1, -1)
    wst_re = m_re[rev].transpose(1, 0, 3, 2).reshape(G, C * H, P)
    wst_im = m_im[rev].transpose(1, 0, 3, 2).reshape(G, C * H, P)
    wout_re = e_re[1:].transpose(1, 3, 0, 2).reshape(G, P, C * H)
    wout_im = (-e_im[1:]).transpose(1, 3, 0, 2).reshape(G, P, C * H)
    padp = lambda v, axis: jnp.pad(v, [(0, LANES - P) if a == axis else (0, 0) for a in range(v.ndim)])
    lam_re, lam_im = padp(pw_re[C][:, None, :], 2), padp(pw_im[C][:, None, :], 2)
    dvec = jnp.tile(d_s5.astype(F32), (1, C))[:, None, :]
    eye = jnp.eye(C, dtype=F32)
    wglu = (eye[None, :, None, :, None] * glu_w.astype(F32)[:, None, :, None, :]).reshape(G, C * H, C * H)
    bglu = jnp.tile(glu_b.astype(F32), (1, C))[:, None, :]
    return (toep.astype(BF16), padp(wst_re, 2).astype(BF16), padp(wst_im, 2).astype(BF16), lam_re, lam_im,
            padp(wout_re, 1).astype(BF16), padp(wout_im, 1).astype(BF16), dvec, wglu.astype(BF16), bglu)


def _s5_body(nb, x_ref, toep_ref, wsr_ref, wsi_ref, lr_ref, li_ref, wor_ref, woi_ref, d_ref,
             wg_ref, bg_ref, o_ref, sr_ref, si_ref, pr_ref, pi_ref):
    x = x_ref[0]
    xb = x.astype(BF16)
    sr_ref[...] = jnp.dot(xb, wsr_ref[0], preferred_element_type=F32)
    si_ref[...] = jnp.dot(xb, wsi_ref[0], preferred_element_type=F32)
    lam_r = jnp.broadcast_to(lr_ref[0], (nb, LANES))
    lam_i = jnp.broadcast_to(li_ref[0], (nb, LANES))
    n_chunks = x.shape[0] // nb

    def step(c, carry):
        xr, xi = carry
        rows = pl.ds(c, nb, stride=n_chunks)
        pr_ref[rows, :] = xr
        pi_ref[rows, :] = xi
        nr = lam_r * xr - lam_i * xi + sr_ref[rows, :]
        ni = lam_r * xi + lam_i * xr + si_ref[rows, :]
        return nr, ni

    zero = jnp.zeros((nb, LANES), F32)
    lax.fori_loop(0, n_chunks, step, (zero, zero))
    y = (jnp.dot(xb, toep_ref[0], preferred_element_type=F32)
         + jnp.dot(pr_ref[...].astype(BF16), wor_ref[0], preferred_element_type=F32)
         + jnp.dot(pi_ref[...].astype(BF16), woi_ref[0], preferred_element_type=F32)
         + d_ref[0] * x)
    v = _gelu_exact(y)
    gate = jax.nn.sigmoid(jnp.dot(v.astype(BF16), wg_ref[0], preferred_element_type=F32) + bg_ref[0])
    o_ref[0] = v * gate


def _s5_mixer(xg, prep, nb):
    G, R, W = xg.shape
    toep, wsr, wsi, lam_re, lam_im, wor, woi, dvec, wglu, bglu = prep
    P = LANES
    per_g = lambda shape: pl.BlockSpec((1,) + shape, lambda g: (g, 0, 0))
    return pl.pallas_call(
        functools.partial(_s5_body, nb),
        grid=(G,),
        in_specs=[per_g((R, W)), per_g((W, W)), per_g((W, P)), per_g((W, P)), per_g((1, P)), per_g((1, P)),
                  per_g((P, W)), per_g((P, W)), per_g((1, W)), per_g((W, W)), per_g((1, W))],
        out_specs=per_g((R, W)),
        out_shape=jax.ShapeDtypeStruct((G, R, W), F32),
        scratch_shapes=[pltpu.VMEM((R, P), F32)] * 4,
        compiler_params=pltpu.CompilerParams(dimension_semantics=("parallel",),
                                             vmem_limit_bytes=VMEM_LIMIT),
        name="s5_mixer",
    )(xg, toep, wsr, wsi, lam_re, lam_im, wor, woi, dvec, wglu, bglu)


def _outproj_body(ys_ref, s5_ref, x_ref, g1_ref, sc2_ref, sh2_ref, ns5_ref, n2_ref, wo_ref, wq_ref,
                  x1_ref, h2_ref, q_ref):
    a = ys_ref[...].astype(BF16)
    b = _rms(s5_ref[...], ns5_ref[...]).astype(BF16)
    mix = (jnp.dot(a, wo_ref[0:SSD_WIDTH, :], preferred_element_type=F32)
           + jnp.dot(b, wo_ref[SSD_WIDTH:, :], preferred_element_type=F32))
    x1 = x_ref[...] + g1_ref[0] * mix
    x1_ref[...] = x1
    h2 = _rms(x1, n2_ref[...]) * (1.0 + sc2_ref[0]) + sh2_ref[0]
    h2_ref[...] = h2
    q = jnp.dot(h2.astype(BF16), wq_ref[...], preferred_element_type=F32)
    for hc in range(2 * PEER_HEADS):
        q_ref[hc] = q[:, hc * PEER_HALF:(hc + 1) * PEER_HALF]


def _out_proj(yssd, s5o, x2d, gate1, scale2, shift2, ns5, n2, w_out, w_query, seq):
    t, d = x2d.shape
    nhc = w_query.shape[1] // PEER_HALF
    tm = 256
    per_b = seq // tm
    rowblk = lambda w: pl.BlockSpec((tm, w), lambda i: (i, 0))
    modblk = pl.BlockSpec((1, 1, d), lambda i: (i // per_b, 0, 0))
    full = lambda shape: pl.BlockSpec(shape, lambda i: (0, 0))
    return pl.pallas_call(
        _outproj_body,
        grid=(t // tm,),
        in_specs=[rowblk(d), rowblk(d), rowblk(d), modblk, modblk, modblk, full((1, d)), full((1, d)),
                  full(w_out.shape), full(w_query.shape)],
        out_specs=[rowblk(d), rowblk(d), pl.BlockSpec((nhc, tm, PEER_HALF), lambda i: (0, i, 0))],
        out_shape=[jax.ShapeDtypeStruct((t, d), F32), jax.ShapeDtypeStruct((t, d), F32),
                   jax.ShapeDtypeStruct((nhc, t, PEER_HALF), F32)],
        compiler_params=pltpu.CompilerParams(dimension_semantics=("parallel",),
                                             vmem_limit_bytes=VMEM_LIMIT),
        name="out_proj_query",
    )(yssd, s5o, x2d, gate1, scale2, shift2, ns5, n2, w_out, w_query)


def _topk_rows(s, n_out, payload=None):
    n = s.shape[0]
    iota = lax.broadcasted_iota(I32, s.shape, 0)
    vals, picks = [], []
    for _ in range(n_out):
        m = jnp.max(s, axis=0, keepdims=True)
        am = jnp.min(jnp.where(s == m, iota, n), axis=0, keepdims=True)
        hit = iota == am
        vals.append(m)
        picks.append(am if payload is None else jnp.max(jnp.where(hit, payload, -1), axis=0, keepdims=True))
        s = jnp.where(hit, -jnp.inf, s)
    return jnp.concatenate(vals, axis=0), jnp.concatenate(picks, axis=0)


def _route_body(q_ref, keys_ref, eid_ref, gate_ref, ts_ref, ti_ref):
    K = PEER_TOPK

    def half(hc, _):
        st = lax.dot_general(keys_ref[hc], q_ref[hc].astype(BF16), (((1,), (1,)), ((), ())),
                             preferred_element_type=F32)
        v, i = _topk_rows(st, K)
        ts_ref[hc] = v
        ti_ref[hc] = i
        return 0

    lax.fori_loop(0, 2 * PEER_HEADS, half, 0, unroll=2)

    def head(h, _):
        s1, i1 = ts_ref[2 * h], ti_ref[2 * h]
        s2, i2 = ts_ref[2 * h + 1], ti_ref[2 * h + 1]
        cs = [s1[0:1, :] + s2] + [s1[a:a + 1, :] + s2[0:8, :] for a in range(1, 8)] + [s1[8:K, :] + s2[0:1, :]]
        ci = ([i1[0:1, :] * PEER_NKEYS + i2] + [i1[a:a + 1, :] * PEER_NKEYS + i2[0:8, :] for a in range(1, 8)]
              + [i1[8:K, :] * PEER_NKEYS + i2[0:1, :]])
        cand = jnp.concatenate(cs, axis=0)
        cid = jnp.concatenate(ci, axis=0)
        top_s, ids = _topk_rows(cand, K, payload=cid)
        e = jnp.exp(top_s - top_s[0:1, :])
        rows = pl.ds(pl.multiple_of(h * K, K), K)
        gate_ref[0, rows, :] = e / jnp.sum(e, axis=0, keepdims=True)
        eid_ref[rows, :] = ids
        return 0

    lax.fori_loop(0, PEER_HEADS, head, 0, unroll=2)


def _peer_route(q, keys):
    nhc, t, dh = q.shape
    tm = 128
    return pl.pallas_call(
        _route_body,
        grid=(t // tm,),
        in_specs=[pl.BlockSpec((nhc, tm, dh), lambda i: (0, i, 0)),
                  pl.BlockSpec(keys.shape, lambda i: (0, 0, 0))],
        out_specs=[pl.BlockSpec((PEER_SLOTS, tm), lambda i: (0, i)),
                   pl.BlockSpec((1, PEER_SLOTS, tm), lambda i: (i, 0, 0))],
        out_shape=[jax.ShapeDtypeStruct((PEER_SLOTS, t), I32),
                   jax.ShapeDtypeStruct((t // tm, PEER_SLOTS, tm), F32)],
        scratch_shapes=[pltpu.VMEM((2 * PEER_HEADS, PEER_TOPK, tm), F32),
                        pltpu.VMEM((2 * PEER_HEADS, PEER_TOPK, tm), I32)],
        compiler_params=pltpu.CompilerParams(dimension_semantics=("parallel",)),
        name="peer_route",
    )(q, keys)


SUBLANES = 8
PEER_TOK_TILE = 512
PEER_RING = 16
PEER_AHEAD = 14
PEER_JGROUPS = PEER_SLOTS // SUBLANES
SLAB_ROWS = 2 * SUBLANES


def _sublane_sums(p, sub):
    m4, m2, m1 = sub < 4, (sub & 2) == 0, (sub & 1) == 0
    z = jnp.where(m4, p[0:4], p[4:8]) + pltpu.roll(jnp.where(m4, p[4:8], p[0:4]), 4, 1)
    y = jnp.where(m2, z[0:2] + pltpu.roll(z[0:2], 6, 1), z[2:4] + pltpu.roll(z[2:4], 2, 1))
    return jnp.where(m1, y[0] + pltpu.roll(y[0], 7, 0), y[1] + pltpu.roll(y[1], 1, 0))


def _peer_body(eid_ref, gate_ref, h_ref, x1_ref, g2_ref, nf_ref, uv_ref, o_ref, buf_ref, po_ref, wb_ref, sem):
    n_blocks = PEER_TOK_TILE // PEER_RING
    sub = lax.broadcasted_iota(I32, (SUBLANES, LANES), 0)
    lane = lax.broadcasted_iota(I32, (SUBLANES, LANES), 1)
    lane_s = lax.broadcasted_iota(I32, (PEER_SLOTS, LANES), 1)

    def issue_rows(t, slot, k0, k1):
        for k in range(k0, k1):
            e = eid_ref[t, k]
            pltpu.make_async_copy(uv_ref.at[e], buf_ref.at[slot, k], sem.at[slot]).start(priority=k % 2)

    def wait_rows(slot):
        pltpu.make_async_copy(uv_ref.at[pl.ds(0, PEER_SLOTS)], buf_ref.at[slot], sem.at[slot]).wait()

    def prologue(t, _):
        issue_rows(t, t, 0, PEER_SLOTS)
        return 0

    lax.fori_loop(0, PEER_AHEAD, prologue, 0)

    def token(t, slot, h_rows):
        t_next = (t + PEER_AHEAD) % PEER_TOK_TILE
        nslot = (slot + PEER_AHEAD) % PEER_RING
        per = PEER_SLOTS // 2 // PEER_JGROUPS
        wait_rows(slot)
        r = slot % SUBLANES
        h_t = jnp.concatenate([h_rows[r:r + 1, LANES * s:LANES * (s + 1)] for s in range(SUBLANES)], axis=0)
        dense = jnp.zeros((SUBLANES, LANES), F32)
        for j in range(PEER_JGROUPS):
            prods = buf_ref[slot, SUBLANES * j:SUBLANES * (j + 1), 0:SUBLANES, :] * h_t[None]
            col = jnp.sum(_sublane_sums(prods, sub), axis=1, keepdims=True)
            dense = jnp.where(lane == j, col, dense)
            issue_rows(t_next, nslot, j * per, (j + 1) * per)
        act = _gelu_exact(dense)
        gcol = jnp.sum(jnp.where(lane_s == t % LANES, gate_ref[t // LANES], 0.0), axis=1, keepdims=True)
        for j in range(PEER_JGROUPS):
            rows = slice(SUBLANES * j, SUBLANES * (j + 1))
            wb_ref[rows, :] = jnp.broadcast_to(act[:, j:j + 1] * gcol[rows, :], (SUBLANES, LANES))

        accs = [jnp.zeros((SUBLANES, LANES), F32) for _ in range(4)]
        for k in range(PEER_SLOTS):
            wk = jnp.broadcast_to(wb_ref[k:k + 1, :], (SUBLANES, LANES))
            accs[k % 4] = accs[k % 4] + buf_ref[slot, k, SUBLANES:SLAB_ROWS, :] * wk
            if k % SUBLANES == SUBLANES - 1:
                j = k // SUBLANES
                issue_rows(t_next, nslot, PEER_SLOTS // 2 + j * per, PEER_SLOTS // 2 + (j + 1) * per)
        out = (accs[0] + accs[1]) + (accs[2] + accs[3])
        return jnp.concatenate([out[s:s + 1, :] for s in range(SUBLANES)], axis=1)

    def block(b, _):
        base = pl.multiple_of(b * PEER_RING, PEER_RING)
        rows = []
        for slot in range(PEER_RING):
            if slot % SUBLANES == 0:
                h_rows = h_ref[pl.ds(pl.multiple_of(base + slot, SUBLANES), SUBLANES), :]
            rows.append(token(base + slot, slot, h_rows))
        po_ref[pl.ds(base, PEER_RING), :] = jnp.concatenate(rows, axis=0)
        return 0

    lax.fori_loop(0, n_blocks, block, 0)
    for slot in range(PEER_AHEAD):
        wait_rows(slot)

    x2 = x1_ref[...] + g2_ref[0] * po_ref[...]
    o_ref[...] = _rms(x2, nf_ref[...])


def _peer_experts(eid, gates_b, h2, x1, gate2, nf, uv3, seq):
    t, d = h2.shape
    tt = PEER_TOK_TILE
    per_b = seq // tt
    tokblk = pl.BlockSpec((tt, d), lambda i: (i, 0))
    return pl.pallas_call(
        _peer_body,
        grid=(t // tt,),
        in_specs=[pl.BlockSpec((tt, PEER_SLOTS), lambda i: (i, 0), memory_space=pltpu.SMEM),
                  pl.BlockSpec((tt // LANES, PEER_SLOTS, LANES), lambda i: (i, 0, 0)),
                  tokblk, tokblk,
                  pl.BlockSpec((1, 1, d), lambda i: (i // per_b, 0, 0)),
                  pl.BlockSpec((1, d), lambda i: (0, 0)),
                  pl.BlockSpec(memory_space=pl.ANY)],
        out_specs=tokblk,
        out_shape=jax.ShapeDtypeStruct((t, d), F32),
        scratch_shapes=[pltpu.VMEM((PEER_RING, PEER_SLOTS, SLAB_ROWS, LANES), F32),
                        pltpu.VMEM((tt, d), F32),
                        pltpu.VMEM((PEER_SLOTS, LANES), F32),
                        pltpu.SemaphoreType.DMA((PEER_RING,))],
        compiler_params=pltpu.CompilerParams(dimension_semantics=("arbitrary",),
                                             vmem_limit_bytes=VMEM_LIMIT),
        name="peer_experts",
    )(eid, gates_b, h2, x1, gate2, nf, uv3)


def kernel(x, c, w_ada, b_ada, norm1_g, w_in, conv_w, conv_b, dt_bias, a_log, d_ssd, norm_ssd_g, s5_a_re, s5_a_im, s5_log_dt, s5_b_re, s5_b_im, s5_c_re, s5_c_im, s5_d, glu_w, glu_b, norm_s5_g, w_out, norm2_g, w_query, sub_keys, expert_u, expert_v, norm_f_g):
    bsz, seq, d = x.shape
    t = bsz * seq
    depth = w_ada.shape[0]
    xt = x.reshape(t, d)
    for l in range(depth):
        mod = _adaln_mod(c, w_ada[l], b_ada[l])
        shift1, scale1, gate1, shift2, scale2, gate2 = [m.reshape(bsz, 1, d) for m in jnp.split(mod, 6, axis=-1)]

        wz, wxbc, wdt, wu = jnp.split(w_in[l], [SSD_WIDTH, SSD_WIDTH + 2048, SSD_WIDTH + 2048 + SSD_HEADS], axis=1)
        wu = _s5_cols_to_hg(wu, axis=1)
        w_cat = jnp.concatenate([wz, wxbc, wu, jnp.pad(wdt, ((0, 0), (0, DT_PAD - SSD_HEADS)))], axis=1).astype(BF16)
        proj = _in_proj(xt, scale1, shift1, norm1_g[l].reshape(1, d), w_cat, seq)

        pad_h = lambda v: jnp.pad(v.astype(F32), (0, DT_PAD - SSD_HEADS)).reshape(1, DT_PAD)
        yssd = _ssd_mixer(proj, conv_w[l], conv_b[l].reshape(1, -1), pad_h(dt_bias[l]),
                          pad_h(-jnp.exp(a_log[l].astype(F32))),
                          jnp.repeat(d_ssd[l].astype(F32), SSD_HEAD_DIM).reshape(1, SSD_WIDTH),
                          norm_ssd_g[l].reshape(1, SSD_WIDTH), bsz, seq)

        blk5 = S5_CHUNK * S5_GROUP_CH
        u = proj[:, COL_U * 1024:(COL_U + 1) * 1024]
        xg = u.reshape(t * S5_GROUP_CH, S5_GROUPS).T.reshape(S5_GROUPS, t // S5_CHUNK, blk5)
        prep = _s5_prepare(s5_a_re[l], s5_a_im[l], s5_log_dt[l], s5_b_re[l], s5_b_im[l], s5_c_re[l],
                           s5_c_im[l], s5_d[l], glu_w[l], glu_b[l])
        og = _s5_mixer(xg, prep, bsz)
        s5o = og.reshape(S5_GROUPS, t * S5_GROUP_CH).T.reshape(t, S5_WIDTH)

        w_o = jnp.concatenate([w_out[l][:SSD_WIDTH], _s5_cols_to_hg(w_out[l][SSD_WIDTH:], axis=0)], axis=0)
        x1, h2, q = _out_proj(yssd, s5o, xt, gate1, scale2, shift2, _s5_cols_to_hg(norm_s5_g[l], axis=0).reshape(1, -1),
                              norm2_g[l].reshape(1, d), w_o.astype(BF16), w_query[l].astype(BF16), seq)

        keys = sub_keys[l].reshape(2 * PEER_HEADS, PEER_NKEYS, PEER_HALF).astype(BF16)
        eid_t, gates_b = _peer_route(q, keys)
        n_exp = expert_u.shape[1]
        uv3 = jnp.concatenate([expert_u[l].reshape(n_exp, SUBLANES, LANES),
                               expert_v[l].reshape(n_exp, SUBLANES, LANES)], axis=1)
        last = l == depth - 1
        assert last, "the final RMSNorm is fused into the last layer's PEER kernel"
        xt = _peer_experts(eid_t.T, gates_b, h2, x1, gate2, norm_f_g.reshape(1, d), uv3, seq)
    return xt.reshape(bsz, seq, d)
```

```python
import functools

import jax
import jax.numpy as jnp
from jax import lax
from jax.experimental import pallas as pl
from jax.experimental.pallas import tpu as pltpu

F32 = jnp.float32
BF16 = jnp.bfloat16
I32 = jnp.int32

D_MODEL = 1024
SSD_WIDTH = 1024
SSD_HEAD_DIM = 64
SSD_HEADS = 16
SSD_GROUPS = 4
SSD_STATE = 128
SSD_CONV = 4
SSD_CHUNK = 128
S5_WIDTH = 1024
S5_GROUP_CH = 16
S5_GROUPS = 64
S5_STATE = 64
S5_CHUNK = 16
PEER_HEADS = 8
PEER_NKEYS = 128
PEER_TOPK = 16
PEER_HALF = 128
PEER_SLOTS = PEER_HEADS * PEER_TOPK
EPS = 1e-6
LANES = 128
DT_PAD = LANES
PROJ_WIDTH = SSD_WIDTH + 2 * 1024 + S5_WIDTH + DT_PAD
COL_Z, COL_XS, COL_BC, COL_U = 0, 1, 2, 3
COL_DT = (4 * 1024) // DT_PAD
HIGHEST = lax.Precision.HIGHEST
VMEM_LIMIT = 56 * 1024 * 1024


def _silu(v):
    return v * jax.nn.sigmoid(v)


def _gelu_exact(v):
    return 0.5 * v * (1.0 + lax.erf(v * (2.0 ** -0.5)))


def _rms(v, g):
    return v * lax.rsqrt(jnp.mean(v * v, axis=-1, keepdims=True) + EPS) * g


def _mod_body(c_ref, w_ref, b_ref, o_ref):
    o_ref[...] = jnp.dot(_silu(c_ref[...]), w_ref[...], preferred_element_type=F32) + b_ref[...]


def _adaln_mod(c, w_ada, b_ada):
    bsz, d = c.shape
    n = w_ada.shape[1]
    tn = 1024
    return pl.pallas_call(
        _mod_body,
        grid=(n // tn,),
        in_specs=[pl.BlockSpec((bsz, d), lambda j: (0, 0)),
                  pl.BlockSpec((d, tn), lambda j: (0, j)),
                  pl.BlockSpec((1, tn), lambda j: (0, j))],
        out_specs=pl.BlockSpec((bsz, tn), lambda j: (0, j)),
        out_shape=jax.ShapeDtypeStruct((bsz, n), F32),
        name="adaln_mod",
    )(c, w_ada, b_ada.reshape(1, n))


def _inproj_body(x_ref, sc_ref, sh_ref, g_ref, w_ref, o_ref):
    h = _rms(x_ref[...], g_ref[...]) * (1.0 + sc_ref[0]) + sh_ref[0]
    o_ref[...] = jnp.dot(h.astype(BF16), w_ref[...], preferred_element_type=F32)


def _in_proj(x2d, scale1, shift1, g1, w_cat, seq):
    t, d = x2d.shape
    n = w_cat.shape[1]
    tm = 256
    per_b = seq // tm
    return pl.pallas_call(
        _inproj_body,
        grid=(t // tm,),
        in_specs=[pl.BlockSpec((tm, d), lambda i: (i, 0)),
                  pl.BlockSpec((1, 1, d), lambda i: (i // per_b, 0, 0)),
                  pl.BlockSpec((1, 1, d), lambda i: (i // per_b, 0, 0)),
                  pl.BlockSpec((1, d), lambda i: (0, 0)),
                  pl.BlockSpec((d, n), lambda i: (0, 0))],
        out_specs=pl.BlockSpec((tm, n), lambda i: (i, 0)),
        out_shape=jax.ShapeDtypeStruct((t, n), F32),
        compiler_params=pltpu.CompilerParams(dimension_semantics=("parallel",),
                                             vmem_limit_bytes=VMEM_LIMIT),
        name="in_proj",
    )(x2d, scale1, shift1, g1, w_cat)


HALO = 8
SSD_PAIRS = SSD_HEADS // 2


def _ssd_body(z_ref, xs_ref, bc_ref, dt_ref, cw_ref, cb_ref, dtb_ref, a_ref, d_ref, ng_ref,
              o_ref, win_ref, act_ref, state_ref, y_ref):
    c = pl.program_id(1)
    L = SSD_CHUNK

    @pl.when(c == 0)
    def _():
        win_ref[0:HALO, :] = jnp.zeros((HALO, 2048), F32)
        state_ref[...] = jnp.zeros(state_ref.shape, F32)

    @pl.when(c > 0)
    def _():
        win_ref[0:HALO, :] = win_ref[L:L + HALO, :]

    win_ref[HALO:HALO + L, 0:1024] = xs_ref[...]
    win_ref[HALO:HALO + L, 1024:2048] = bc_ref[...]

    for sl in range(8):
        cols = slice(sl * 256, (sl + 1) * 256)
        acc = jnp.broadcast_to(cb_ref[:, cols], (L, 256))
        for k in range(SSD_CONV):
            r0 = HALO - (SSD_CONV - 1) + k
            acc = acc + win_ref[r0:r0 + L, cols] * cw_ref[k:k + 1, cols]
        act_ref[:, cols] = _silu(acc)

    dtv = dt_ref[...] + dtb_ref[...]
    dt = jnp.maximum(dtv, 0.0) + jnp.log1p(jnp.exp(-jnp.abs(dtv)))
    da = dt * a_ref[...]
    row = lax.broadcasted_iota(I32, (L, L), 0)
    col = lax.broadcasted_iota(I32, (L, L), 1)
    causal = row >= col
    tril = jnp.where(causal, 1.0, 0.0).astype(F32)
    cs = jnp.dot(tril, da, precision=HIGHEST, preferred_element_type=F32)
    cs_t = cs.T
    dec_t = jnp.exp(cs_t[:, L - 1:L] - cs_t)
    ecs = jnp.exp(cs)
    lane = lax.broadcasted_iota(I32, (L, LANES), 1)
    first = lane < SSD_HEAD_DIM

    for g in range(SSD_GROUPS):
        bg = act_ref[:, 1024 + g * 128:1024 + (g + 1) * 128]
        cg = act_ref[:, 1536 + g * 128:1536 + (g + 1) * 128]
        bg_t = bg.T
        cg_b = cg.astype(BF16)
        cb = jnp.dot(cg_b, bg_t.astype(BF16), preferred_element_type=F32)
        for r in range(2):
            p = g * 2 + r
            h0, h1 = 2 * p, 2 * p + 1
            x2 = act_ref[:, p * 128:(p + 1) * 128]
            dt2 = jnp.where(first, dt[:, h0:h0 + 1], dt[:, h1:h1 + 1])
            xdt = (x2 * dt2).astype(BF16)
            ydiag, snew = [], []
            for h in (h0, h1):
                diff = cs[:, h:h + 1] - cs_t[h:h + 1, :]
                lmat = jnp.where(causal, jnp.exp(jnp.where(causal, diff, 0.0)), 0.0)
                ydiag.append(jnp.dot((cb * lmat).astype(BF16), xdt, preferred_element_type=F32))
                bw = (bg_t * dec_t[h:h + 1, :]).astype(BF16)
                snew.append(jnp.dot(bw, xdt, preferred_element_type=F32))
            prev = state_ref[p]
            yoff = jnp.dot(cg_b, prev.astype(BF16), preferred_element_type=F32)
            yoff = yoff * jnp.where(first, ecs[:, h0:h0 + 1], ecs[:, h1:h1 + 1])
            cdec = jnp.where(first[0:1, :], ecs[L - 1:L, h0:h0 + 1], ecs[L - 1:L, h1:h1 + 1])
            state_ref[p] = prev * cdec + jnp.where(first, snew[0], snew[1])
            y2 = jnp.where(first, ydiag[0], ydiag[1]) + yoff + d_ref[:, p * 128:(p + 1) * 128] * x2
            y_ref[:, p * 128:(p + 1) * 128] = y2

    gw = SSD_WIDTH // SSD_GROUPS
    for g in range(SSD_GROUPS):
        cols = slice(g * gw, (g + 1) * gw)
        yg = y_ref[:, cols] * _silu(z_ref[:, cols])
        o_ref[:, cols] = _rms(yg, ng_ref[:, cols])


def _ssd_mixer(proj, conv_w, conv_b, dt_bias_p, a_p, d_exp, norm_g, bsz, seq):
    t = proj.shape[0]
    L = SSD_CHUNK
    nc = seq // L
    row = lambda b, c: b * nc + c
    full = lambda shape: pl.BlockSpec(shape, lambda b, c: (0, 0))
    return pl.pallas_call(
        _ssd_body,
        grid=(bsz, nc),
        in_specs=[pl.BlockSpec((L, 1024), lambda b, c: (row(b, c), COL_Z)),
                  pl.BlockSpec((L, 1024), lambda b, c: (row(b, c), COL_XS)),
                  pl.BlockSpec((L, 1024), lambda b, c: (row(b, c), COL_BC)),
                  pl.BlockSpec((L, DT_PAD), lambda b, c: (row(b, c), COL_DT)),
                  full((SSD_CONV, 2048)), full((1, 2048)), full((1, DT_PAD)), full((1, DT_PAD)),
                  full((1, 1024)), full((1, 1024))],
        out_specs=pl.BlockSpec((L, 1024), lambda b, c: (row(b, c), 0)),
        out_shape=jax.ShapeDtypeStruct((t, SSD_WIDTH), F32),
        scratch_shapes=[pltpu.VMEM((L + HALO, 2048), F32),
                        pltpu.VMEM((L, 2048), F32),
                        pltpu.VMEM((SSD_PAIRS, SSD_STATE, LANES), F32),
                        pltpu.VMEM((L, 1024), F32)],
        compiler_params=pltpu.CompilerParams(dimension_semantics=("parallel", "arbitrary"),
                                             vmem_limit_bytes=VMEM_LIMIT),
        name="ssd_mixer",
    )(proj, proj, proj, proj, conv_w, conv_b, dt_bias_p, a_p, d_exp, norm_g)


def _s5_cols_to_hg(w, axis):
    shape = w.shape
    split = shape[:axis] + (S5_GROUPS, S5_GROUP_CH) + shape[axis + 1:]
    return jnp.swapaxes(w.reshape(split), axis, axis + 1).reshape(shape)


def _s5_prepare(a_re, a_im, log_dt, b_re, b_im, c_re, c_im, d_s5, glu_w, glu_b):
    G, P, H, C = S5_GROUPS, S5_STATE, S5_GROUP_CH, S5_CHUNK
    lr, li = a_re.astype(F32), a_im.astype(F32)
    dt = jnp.exp(log_dt.astype(F32))[:, None]
    mag = jnp.exp(lr * dt)
    lb_re, lb_im = mag * jnp.cos(li * dt), mag * jnp.sin(li * dt)
    den = lr * lr + li * li
    coef_re = ((lb_re - 1.0) * lr + lb_im * li) / den
    coef_im = (lb_im * lr - (lb_re - 1.0) * li) / den
    bb_re = coef_re[..., None] * b_re - coef_im[..., None] * b_im
    bb_im = coef_re[..., None] * b_im + coef_im[..., None] * b_re
    tau = jnp.arange(C + 1, dtype=F32)[:, None, None]
    pm = jnp.exp(lr * dt * tau)
    pw_re, pw_im = pm * jnp.cos(li * dt * tau), pm * jnp.sin(li * dt * tau)
    m_re = pw_re[..., None] * bb_re - pw_im[..., None] * bb_im
    m_im = pw_re[..., None] * bb_im + pw_im[..., None] * bb_re
    e_re = c_re * pw_re[:, :, None, :] - c_im * pw_im[:, :, None, :]
    e_im = c_re * pw_im[:, :, None, :] + c_im * pw_re[:, :, None, :]
    kern = (jnp.einsum('ghp,tgpk->tghk', c_re, m_re[:C], precision=HIGHEST)
            - jnp.einsum('ghp,tgpk->tghk', c_im, m_im[:C], precision=HIGHEST))
    j_in = jnp.arange(C)[:, None]
    j_out = jnp.arange(C)[None, :]
    lag = j_out - j_in
    toep = jnp.where((lag >= 0)[:, :, None, None, None], kern[jnp.clip(lag, 0, C - 1)], 0.0)
    toep = toep.transpose(2, 0, 4, 1, 3).reshape(G, C * H, C * H)
    rev = jnp.arange(C - 1, -1, -1)
    wst_re = m_re[rev].transpose(1, 0, 3, 2).reshape(G, C * H, P)
    wst_im = m_im[rev].transpose(1, 0, 3, 2).reshape(G, C * H, P)
    wout_re = e_re[1:].transpose(1, 3, 0, 2).reshape(G, P, C * H)
    wout_im = (-e_im[1:]).transpose(1, 3, 0, 2).reshape(G, P, C * H)
    padp = lambda v, axis: jnp.pad(v, [(0, LANES - P) if a == axis else (0, 0) for a in range(v.ndim)])
    lam_re, lam_im = padp(pw_re[C][:, None, :], 2), padp(pw_im[C][:, None, :], 2)
    dvec = jnp.tile(d_s5.astype(F32), (1, C))[:, None, :]
    eye = jnp.eye(C, dtype=F32)
    wglu = (eye[None, :, None, :, None] * glu_w.astype(F32)[:, None, :, None, :]).reshape(G, C * H, C * H)
    bglu = jnp.tile(glu_b.astype(F32), (1, C))[:, None, :]
    return (toep.astype(BF16), padp(wst_re, 2).astype(BF16), padp(wst_im, 2).astype(BF16), lam_re, lam_im,
            padp(wout_re, 1).astype(BF16), padp(wout_im, 1).astype(BF16), dvec, wglu.astype(BF16), bglu)


def _s5_body(nb, x_ref, toep_ref, wsr_ref, wsi_ref, lr_ref, li_ref, wor_ref, woi_ref, d_ref,
             wg_ref, bg_ref, o_ref, sr_ref, si_ref, pr_ref, pi_ref):
    x = x_ref[0]
    xb = x.astype(BF16)
    sr_ref[...] = jnp.dot(xb, wsr_ref[0], preferred_element_type=F32)
    si_ref[...] = jnp.dot(xb, wsi_ref[0], preferred_element_type=F32)
    lam_r = jnp.broadcast_to(lr_ref[0], (nb, LANES))
    lam_i = jnp.broadcast_to(li_ref[0], (nb, LANES))
    n_chunks = x.shape[0] // nb

    def step(c, carry):
        xr, xi = carry
        rows = pl.ds(c, nb, stride=n_chunks)
        pr_ref[rows, :] = xr
        pi_ref[rows, :] = xi
        nr = lam_r * xr - lam_i * xi + sr_ref[rows, :]
        ni = lam_r * xi + lam_i * xr + si_ref[rows, :]
        return nr, ni

    zero = jnp.zeros((nb, LANES), F32)
    lax.fori_loop(0, n_chunks, step, (zero, zero))
    y = (jnp.dot(xb, toep_ref[0], preferred_element_type=F32)
         + jnp.dot(pr_ref[...].astype(BF16), wor_ref[0], preferred_element_type=F32)
         + jnp.dot(pi_ref[...].astype(BF16), woi_ref[0], preferred_element_type=F32)
         + d_ref[0] * x)
    v = _gelu_exact(y)
    gate = jax.nn.sigmoid(jnp.dot(v.astype(BF16), wg_ref[0], preferred_element_type=F32) + bg_ref[0])
    o_ref[0] = v * gate


def _s5_mixer(xg, prep, nb):
    G, R, W = xg.shape
    toep, wsr, wsi, lam_re, lam_im, wor, woi, dvec, wglu, bglu = prep
    P = LANES
    per_g = lambda shape: pl.BlockSpec((1,) + shape, lambda g: (g, 0, 0))
    return pl.pallas_call(
        functools.partial(_s5_body, nb),
        grid=(G,),
        in_specs=[per_g((R, W)), per_g((W, W)), per_g((W, P)), per_g((W, P)), per_g((1, P)), per_g((1, P)),
                  per_g((P, W)), per_g((P, W)), per_g((1, W)), per_g((W, W)), per_g((1, W))],
        out_specs=per_g((R, W)),
        out_shape=jax.ShapeDtypeStruct((G, R, W), F32),
        scratch_shapes=[pltpu.VMEM((R, P), F32)] * 4,
        compiler_params=pltpu.CompilerParams(dimension_semantics=("parallel",),
                                             vmem_limit_bytes=VMEM_LIMIT),
        name="s5_mixer",
    )(xg, toep, wsr, wsi, lam_re, lam_im, wor, woi, dvec, wglu, bglu)


def _outproj_body(ys_ref, s5_ref, x_ref, g1_ref, sc2_ref, sh2_ref, ns5_ref, n2_ref, wo_ref, wq_ref,
                  x1_ref, h2_ref, q_ref):
    a = ys_ref[...].astype(BF16)
    b = _rms(s5_ref[...], ns5_ref[...]).astype(BF16)
    mix = (jnp.dot(a, wo_ref[0:SSD_WIDTH, :], preferred_element_type=F32)
           + jnp.dot(b, wo_ref[SSD_WIDTH:, :], preferred_element_type=F32))
    x1 = x_ref[...] + g1_ref[0] * mix
    x1_ref[...] = x1
    h2 = _rms(x1, n2_ref[...]) * (1.0 + sc2_ref[0]) + sh2_ref[0]
    h2_ref[...] = h2
    q = jnp.dot(h2.astype(BF16), wq_ref[...], preferred_element_type=F32)
    for hc in range(2 * PEER_HEADS):
        q_ref[hc] = q[:, hc * PEER_HALF:(hc + 1) * PEER_HALF]


def _out_proj(yssd, s5o, x2d, gate1, scale2, shift2, ns5, n2, w_out, w_query, seq):
    t, d = x2d.shape
    nhc = w_query.shape[1] // PEER_HALF
    tm = 256
    per_b = seq // tm
    rowblk = lambda w: pl.BlockSpec((tm, w), lambda i: (i, 0))
    modblk = pl.BlockSpec((1, 1, d), lambda i: (i // per_b, 0, 0))
    full = lambda shape: pl.BlockSpec(shape, lambda i: (0, 0))
    return pl.pallas_call(
        _outproj_body,
        grid=(t // tm,),
        in_specs=[rowblk(d), rowblk(d), rowblk(d), modblk, modblk, modblk, full((1, d)), full((1, d)),
                  full(w_out.shape), full(w_query.shape)],
        out_specs=[rowblk(d), rowblk(d), pl.BlockSpec((nhc, tm, PEER_HALF), lambda i: (0, i, 0))],
        out_shape=[jax.ShapeDtypeStruct((t, d), F32), jax.ShapeDtypeStruct((t, d), F32),
                   jax.ShapeDtypeStruct((nhc, t, PEER_HALF), F32)],
        compiler_params=pltpu.CompilerParams(dimension_semantics=("parallel",),
                                             vmem_limit_bytes=VMEM_LIMIT),
        name="out_proj_query",
    )(yssd, s5o, x2d, gate1, scale2, shift2, ns5, n2, w_out, w_query)


def _topk_rows(s, n_out, payload=None):
    n = s.shape[0]
    iota = lax.broadcasted_iota(I32, s.shape, 0)
    vals, picks = [], []
    for _ in range(n_out):
        m = jnp.max(s, axis=0, keepdims=True)
        am = jnp.min(jnp.where(s == m, iota, n), axis=0, keepdims=True)
        hit = iota == am
        vals.append(m)
        picks.append(am if payload is None else jnp.max(jnp.where(hit, payload, -1), axis=0, keepdims=True))
        s = jnp.where(hit, -jnp.inf, s)
        yield
    return jnp.concatenate(vals, axis=0), jnp.concatenate(picks, axis=0)


def _route_head_steps(q1, k1, q2, k2):
    K = PEER_TOPK
    nt = (((1,), (1,)), ((), ()))
    s1, i1 = yield from _topk_rows(lax.dot_general(k1, q1.astype(BF16), nt, preferred_element_type=F32), K)
    s2, i2 = yield from _topk_rows(lax.dot_general(k2, q2.astype(BF16), nt, preferred_element_type=F32), K)
    cs = [s1[0:1, :] + s2] + [s1[a:a + 1, :] + s2[0:8, :] for a in range(1, 8)] + [s1[8:K, :] + s2[0:1, :]]
    ci = ([i1[0:1, :] * PEER_NKEYS + i2] + [i1[a:a + 1, :] * PEER_NKEYS + i2[0:8, :] for a in range(1, 8)]
          + [i1[8:K, :] * PEER_NKEYS + i2[0:1, :]])
    top_s, ids = yield from _topk_rows(jnp.concatenate(cs, axis=0), K, payload=jnp.concatenate(ci, axis=0))
    e = jnp.exp(top_s - top_s[0:1, :])
    return e / jnp.sum(e, axis=0, keepdims=True), ids


def _run(gen, steps=None):
    try:
        while steps is None or steps > 0:
            next(gen)
            steps = None if steps is None else steps - 1
    except StopIteration as done:
        return done.value
    return None


SUBLANES = 8
PEER_TOK_TILE = 512
PEER_RING = 16
PEER_AHEAD = 8
PEER_JGROUPS = PEER_SLOTS // SUBLANES
PEER_ROUTE_STEPS = 3
SLAB_ROWS = 2 * SUBLANES


def _sublane_sums(p, sub):
    m4, m2, m1 = sub < 4, (sub & 2) == 0, (sub & 1) == 0
    z = jnp.where(m4, p[0:4], p[4:8]) + pltpu.roll(jnp.where(m4, p[4:8], p[0:4]), 4, 1)
    y = jnp.where(m2, z[0:2] + pltpu.roll(z[0:2], 6, 1), z[2:4] + pltpu.roll(z[2:4], 2, 1))
    return jnp.where(m1, y[0] + pltpu.roll(y[0], 7, 0), y[1] + pltpu.roll(y[1], 1, 0))


def _peer_body(q_ref, keys_ref, h_ref, x1_ref, g2_ref, nf_ref, uv_ref, o_ref,
               buf_ref, po_ref, wb_ref, ts_ref, ti_ref, gates_ref, eidv_ref, eidt_ref, eid_smem, sem, sem_eid):
    i = pl.program_id(0)
    n_tiles = pl.num_programs(0) - 1
    wslot = i % 2
    rslot = 1 - wslot
    n_blocks = PEER_TOK_TILE // PEER_RING
    assert PEER_TOK_TILE // LANES * PEER_HEADS == n_blocks
    sub = lax.broadcasted_iota(I32, (SUBLANES, LANES), 0)
    lane = lax.broadcasted_iota(I32, (SUBLANES, LANES), 1)
    lane_s = lax.broadcasted_iota(I32, (PEER_SLOTS, LANES), 1)

    def route_unit(b):
        rb, h = b // PEER_HEADS, b % PEER_HEADS
        toks = pl.ds(pl.multiple_of(rb * LANES, LANES), LANES)
        gates, ids = yield from _route_head_steps(q_ref[2 * h, toks, :], keys_ref[2 * h],
                                                  q_ref[2 * h + 1, toks, :], keys_ref[2 * h + 1])
        rows = pl.ds(pl.multiple_of(h * PEER_TOPK, PEER_TOPK), PEER_TOPK)
        gates_ref[wslot, rb, rows, :] = gates
        eidv_ref[rows, :] = ids

    def publish_ids(b):
        rb, h = b // PEER_HEADS, b % PEER_HEADS

        @pl.when(h == PEER_HEADS - 1)
        def _():
            toks = pl.ds(pl.multiple_of(rb * LANES, LANES), LANES)
            eidt_ref[...] = eidv_ref[...].astype(F32).T.astype(I32)
            cp = pltpu.make_async_copy(eidt_ref, eid_smem.at[wslot, toks, :], sem_eid)
            cp.start()
            cp.wait()

    def issue_rows(t, slot, k0, k1):
        for k in range(k0, k1):
            e = eid_smem[rslot, t, k]
            pltpu.make_async_copy(uv_ref.at[e], buf_ref.at[slot, k], sem.at[slot]).start(priority=k % 2)

    def wait_rows(slot):
        pltpu.make_async_copy(uv_ref.at[pl.ds(0, PEER_SLOTS)], buf_ref.at[slot], sem.at[slot]).wait()

    def token(t, slot, h_rows):
        t_next = (t + PEER_AHEAD) % PEER_TOK_TILE
        nslot = (slot + PEER_AHEAD) % PEER_RING
        per = PEER_SLOTS // 2 // PEER_JGROUPS
        wait_rows(slot)
        r = slot % SUBLANES
        h_t = jnp.concatenate([h_rows[r:r + 1, LANES * s:LANES * (s + 1)] for s in range(SUBLANES)], axis=0)
        dense = jnp.zeros((SUBLANES, LANES), F32)
        for j in range(PEER_JGROUPS):
            prods = buf_ref[slot, SUBLANES * j:SUBLANES * (j + 1), 0:SUBLANES, :] * h_t[None]
            col = jnp.sum(_sublane_sums(prods, sub), axis=1, keepdims=True)
            dense = jnp.where(lane == j, col, dense)
            issue_rows(t_next, nslot, j * per, (j + 1) * per)
        act = _gelu_exact(dense)
        gblk = gates_ref[rslot, t // LANES]
        gcol = jnp.sum(jnp.where(lane_s == t % LANES, gblk, 0.0), axis=1, keepdims=True)
        for j in range(PEER_JGROUPS):
            rows = slice(SUBLANES * j, SUBLANES * (j + 1))
            wb_ref[rows, :] = jnp.broadcast_to(act[:, j:j + 1] * gcol[rows, :], (SUBLANES, LANES))

        accs = [jnp.zeros((SUBLANES, LANES), F32) for _ in range(4)]
        for k in range(PEER_SLOTS):
            wk = jnp.broadcast_to(wb_ref[k:k + 1, :], (SUBLANES, LANES))
            accs[k % 4] = accs[k % 4] + buf_ref[slot, k, SUBLANES:SLAB_ROWS, :] * wk
            if k % SUBLANES == SUBLANES - 1:
                j = k // SUBLANES
                issue_rows(t_next, nslot, PEER_SLOTS // 2 + j * per, PEER_SLOTS // 2 + (j + 1) * per)
        out = (accs[0] + accs[1]) + (accs[2] + accs[3])
        return jnp.concatenate([out[s:s + 1, :] for s in range(SUBLANES)], axis=1)

    @pl.when(i == 0)
    def _():
        def first(b, _):
            _run(route_unit(b))
            publish_ids(b)
            return 0

        lax.fori_loop(0, n_blocks, first, 0)

    @pl.when(i > 0)
    def _():
        def prologue(t, _):
            issue_rows(t, t, 0, PEER_SLOTS)
            return 0

        lax.fori_loop(0, PEER_AHEAD, prologue, 0)

        def block(b, _):
            routing = route_unit(b)
            base = pl.multiple_of(b * PEER_RING, PEER_RING)
            rows = []
            for slot in range(PEER_RING):
                if slot % SUBLANES == 0:
                    h_rows = h_ref[pl.ds(pl.multiple_of(base + slot, SUBLANES), SUBLANES), :]
                rows.append(token(base + slot, slot, h_rows))
                _run(routing, steps=PEER_ROUTE_STEPS)
            _run(routing)
            po_ref[pl.ds(base, PEER_RING), :] = jnp.concatenate(rows, axis=0)
            publish_ids(b)
            return 0

        lax.fori_loop(0, n_blocks, block, 0)
        for slot in range(PEER_AHEAD):
            wait_rows(slot)
        x2 = x1_ref[...] + g2_ref[0] * po_ref[...]
        o_ref[...] = _rms(x2, nf_ref[...])


def _peer_ffn(q, keys, h2, x1, gate2, nf, uv3, seq):
    t, d = h2.shape
    nhc, _, dh = q.shape
    tt = PEER_TOK_TILE
    n_tiles = t // tt
    per_b = seq // tt
    prev = lambda i: jnp.maximum(i - 1, 0)
    tokblk = pl.BlockSpec((tt, d), lambda i: (prev(i), 0))
    return pl.pallas_call(
        _peer_body,
        grid=(n_tiles + 1,),
        in_specs=[pl.BlockSpec((nhc, tt, dh), lambda i: (0, jnp.minimum(i, n_tiles - 1), 0)),
                  pl.BlockSpec(keys.shape, lambda i: (0, 0, 0)),
                  tokblk, tokblk,
                  pl.BlockSpec((1, 1, d), lambda i: (prev(i) // per_b, 0, 0)),
                  pl.BlockSpec((1, d), lambda i: (0, 0)),
                  pl.BlockSpec(memory_space=pl.ANY)],
        out_specs=tokblk,
        out_shape=jax.ShapeDtypeStruct((t, d), F32),
        scratch_shapes=[pltpu.VMEM((PEER_RING, PEER_SLOTS, SLAB_ROWS, LANES), F32),
                        pltpu.VMEM((tt, d), F32),
                        pltpu.VMEM((PEER_SLOTS, LANES), F32),
                        pltpu.VMEM((2, PEER_TOPK, LANES), F32),
                        pltpu.VMEM((2, PEER_TOPK, LANES), I32),
                        pltpu.VMEM((2, tt // LANES, PEER_SLOTS, LANES), F32),
                        pltpu.VMEM((PEER_SLOTS, LANES), I32),
                        pltpu.VMEM((LANES, PEER_SLOTS), I32),
                        pltpu.SMEM((2, tt, PEER_SLOTS), I32),
                        pltpu.SemaphoreType.DMA((PEER_RING,)),
                        pltpu.SemaphoreType.DMA(())],
        compiler_params=pltpu.CompilerParams(dimension_semantics=("arbitrary",),
                                             vmem_limit_bytes=VMEM_LIMIT),
        name="peer_ffn",
    )(q, keys, h2, x1, gate2, nf, uv3)


def kernel(x, c, w_ada, b_ada, norm1_g, w_in, conv_w, conv_b, dt_bias, a_log, d_ssd, norm_ssd_g, s5_a_re, s5_a_im, s5_log_dt, s5_b_re, s5_b_im, s5_c_re, s5_c_im, s5_d, glu_w, glu_b, norm_s5_g, w_out, norm2_g, w_query, sub_keys, expert_u, expert_v, norm_f_g):
    bsz, seq, d = x.shape
    t = bsz * seq
    depth = w_ada.shape[0]
    xt = x.reshape(t, d)
    for l in range(depth):
        mod = _adaln_mod(c, w_ada[l], b_ada[l])
        shift1, scale1, gate1, shift2, scale2, gate2 = [m.reshape(bsz, 1, d) for m in jnp.split(mod, 6, axis=-1)]

        wz, wxbc, wdt, wu = jnp.split(w_in[l], [SSD_WIDTH, SSD_WIDTH + 2048, SSD_WIDTH + 2048 + SSD_HEADS], axis=1)
        wu = _s5_cols_to_hg(wu, axis=1)
        w_cat = jnp.concatenate([wz, wxbc, wu, jnp.pad(wdt, ((0, 0), (0, DT_PAD - SSD_HEADS)))], axis=1).astype(BF16)
        proj = _in_proj(xt, scale1, shift1, norm1_g[l].reshape(1, d), w_cat, seq)

        pad_h = lambda v: jnp.pad(v.astype(F32), (0, DT_PAD - SSD_HEADS)).reshape(1, DT_PAD)
        yssd = _ssd_mixer(proj, conv_w[l], conv_b[l].reshape(1, -1), pad_h(dt_bias[l]),
                          pad_h(-jnp.exp(a_log[l].astype(F32))),
                          jnp.repeat(d_ssd[l].astype(F32), SSD_HEAD_DIM).reshape(1, SSD_WIDTH),
                          norm_ssd_g[l].reshape(1, SSD_WIDTH), bsz, seq)

        blk5 = S5_CHUNK * S5_GROUP_CH
        u = proj[:, COL_U * 1024:(COL_U + 1) * 1024]
        xg = u.reshape(t * S5_GROUP_CH, S5_GROUPS).T.reshape(S5_GROUPS, t // S5_CHUNK, blk5)
        prep = _s5_prepare(s5_a_re[l], s5_a_im[l], s5_log_dt[l], s5_b_re[l], s5_b_im[l], s5_c_re[l],
                           s5_c_im[l], s5_d[l], glu_w[l], glu_b[l])
        og = _s5_mixer(xg, prep, bsz)
        s5o = og.reshape(S5_GROUPS, t * S5_GROUP_CH).T.reshape(t, S5_WIDTH)

        w_o = jnp.concatenate([w_out[l][:SSD_WIDTH], _s5_cols_to_hg(w_out[l][SSD_WIDTH:], axis=0)], axis=0)
        x1, h2, q = _out_proj(yssd, s5o, xt, gate1, scale2, shift2, _s5_cols_to_hg(norm_s5_g[l], axis=0).reshape(1, -1),
                              norm2_g[l].reshape(1, d), w_o.astype(BF16), w_query[l].astype(BF16), seq)

        keys = sub_keys[l].reshape(2 * PEER_HEADS, PEER_NKEYS, PEER_HALF).astype(BF16)
        n_exp = expert_u.shape[1]
        uv3 = jnp.concatenate([expert_u[l].reshape(n_exp, SUBLANES, LANES),
                               expert_v[l].reshape(n_exp, SUBLANES, LANES)], axis=1)
        last = l == depth - 1
        assert last, "the final RMSNorm is fused into the last layer's PEER kernel"
        xt = _peer_ffn(q, keys, h2, x1, gate2, norm_f_g.reshape(1, d), uv3, seq)
    return xt.reshape(bsz, seq, d)
```

```python
import functools

import jax
import jax.numpy as jnp
from jax import lax
from jax.experimental import pallas as pl
from jax.experimental.pallas import tpu as pltpu

F32 = jnp.float32
BF16 = jnp.bfloat16
I32 = jnp.int32

D_MODEL = 1024
SSD_WIDTH = 1024
SSD_HEAD_DIM = 64
SSD_HEADS = 16
SSD_GROUPS = 4
SSD_STATE = 128
SSD_CONV = 4
SSD_CHUNK = 128
S5_WIDTH = 1024
S5_GROUP_CH = 16
S5_GROUPS = 64
S5_STATE = 64
S5_CHUNK = 16
PEER_HEADS = 8
PEER_NKEYS = 128
PEER_TOPK = 16
PEER_HALF = 128
PEER_SLOTS = PEER_HEADS * PEER_TOPK
EPS = 1e-6
LANES = 128
DT_PAD = LANES
SSD_PROJ_WIDTH = SSD_WIDTH + 2 * 1024 + DT_PAD
COL_Z, COL_XS, COL_BC = 0, 1, 2
COL_DT = (3 * 1024) // DT_PAD
HIGHEST = lax.Precision.HIGHEST
VMEM_LIMIT = 56 * 1024 * 1024


def _silu(v):
    return v * jax.nn.sigmoid(v)


def _gelu_exact(v):
    return 0.5 * v * (1.0 + lax.erf(v * (2.0 ** -0.5)))


def _rms(v, g):
    return v * lax.rsqrt(jnp.mean(v * v, axis=-1, keepdims=True) + EPS) * g


def _mod_body(c_ref, w_ref, b_ref, o_ref):
    o_ref[...] = jnp.dot(_silu(c_ref[...]), w_ref[...], preferred_element_type=F32) + b_ref[...]


def _adaln_mod(c, w_ada, b_ada):
    bsz, d = c.shape
    n = w_ada.shape[1]
    tn = 1024
    return pl.pallas_call(
        _mod_body,
        grid=(n // tn,),
        in_specs=[pl.BlockSpec((bsz, d), lambda j: (0, 0)),
                  pl.BlockSpec((d, tn), lambda j: (0, j)),
                  pl.BlockSpec((1, tn), lambda j: (0, j))],
        out_specs=pl.BlockSpec((bsz, tn), lambda j: (0, j)),
        out_shape=jax.ShapeDtypeStruct((bsz, n), F32),
        name="adaln_mod",
    )(c, w_ada, b_ada.reshape(1, n))


def _inproj_body(x_ref, sc_ref, sh_ref, g_ref, w_ref, o_ref, u_ref):
    h = _rms(x_ref[...], g_ref[...]) * (1.0 + sc_ref[0]) + sh_ref[0]
    p = jnp.dot(h.astype(BF16), w_ref[...], preferred_element_type=F32)
    o_ref[...] = p[:, 0:SSD_PROJ_WIDTH]
    u_ref[...] = p[:, SSD_PROJ_WIDTH:]


def _in_proj(x2d, scale1, shift1, g1, w_cat, seq):
    t, d = x2d.shape
    n = SSD_PROJ_WIDTH
    tm = 256
    per_b = seq // tm
    return pl.pallas_call(
        _inproj_body,
        grid=(t // tm,),
        in_specs=[pl.BlockSpec((tm, d), lambda i: (i, 0)),
                  pl.BlockSpec((1, 1, d), lambda i: (i // per_b, 0, 0)),
                  pl.BlockSpec((1, 1, d), lambda i: (i // per_b, 0, 0)),
                  pl.BlockSpec((1, d), lambda i: (0, 0)),
                  pl.BlockSpec(w_cat.shape, lambda i: (0, 0))],
        out_specs=[pl.BlockSpec((tm, n), lambda i: (i, 0)), pl.BlockSpec((tm, S5_WIDTH), lambda i: (i, 0))],
        out_shape=[jax.ShapeDtypeStruct((t, n), F32), jax.ShapeDtypeStruct((t, S5_WIDTH), F32)],
        compiler_params=pltpu.CompilerParams(dimension_semantics=("parallel",),
                                             vmem_limit_bytes=VMEM_LIMIT),
        name="in_proj",
    )(x2d, scale1, shift1, g1, w_cat)


HALO = 8
SSD_PAIRS = SSD_HEADS // 2


def _ssd_body(z_ref, xs_ref, bc_ref, dt_ref, cw_ref, cb_ref, dtb_ref, a_ref, d_ref, ng_ref,
              o_ref, win_ref, act_ref, state_ref, y_ref):
    c = pl.program_id(1)
    L = SSD_CHUNK

    @pl.when(c == 0)
    def _():
        win_ref[0:HALO, :] = jnp.zeros((HALO, 2048), F32)
        state_ref[...] = jnp.zeros(state_ref.shape, F32)

    @pl.when(c > 0)
    def _():
        win_ref[0:HALO, :] = win_ref[L:L + HALO, :]

    win_ref[HALO:HALO + L, 0:1024] = xs_ref[...]
    win_ref[HALO:HALO + L, 1024:2048] = bc_ref[...]

    for sl in range(8):
        cols = slice(sl * 256, (sl + 1) * 256)
        acc = jnp.broadcast_to(cb_ref[:, cols], (L, 256))
        for k in range(SSD_CONV):
            r0 = HALO - (SSD_CONV - 1) + k
            acc = acc + win_ref[r0:r0 + L, cols] * cw_ref[k:k + 1, cols]
        act_ref[:, cols] = _silu(acc)

    dtv = dt_ref[...] + dtb_ref[...]
    dt = jnp.maximum(dtv, 0.0) + jnp.log1p(jnp.exp(-jnp.abs(dtv)))
    da = dt * a_ref[...]
    row = lax.broadcasted_iota(I32, (L, L), 0)
    col = lax.broadcasted_iota(I32, (L, L), 1)
    causal = row >= col
    tril = jnp.where(causal, 1.0, 0.0).astype(F32)
    cs = jnp.dot(tril, da, precision=HIGHEST, preferred_element_type=F32)
    cs_t = cs.T
    dec_t = jnp.exp(cs_t[:, L - 1:L] - cs_t)
    ecs = jnp.exp(cs)
    lane = lax.broadcasted_iota(I32, (L, LANES), 1)
    first = lane < SSD_HEAD_DIM

    for g in range(SSD_GROUPS):
        bg = act_ref[:, 1024 + g * 128:1024 + (g + 1) * 128]
        cg = act_ref[:, 1536 + g * 128:1536 + (g + 1) * 128]
        bg_t = bg.T
        cg_b = cg.astype(BF16)
        cb = jnp.dot(cg_b, bg_t.astype(BF16), preferred_element_type=F32)
        for r in range(2):
            p = g * 2 + r
            h0, h1 = 2 * p, 2 * p + 1
            x2 = act_ref[:, p * 128:(p + 1) * 128]
            dt2 = jnp.where(first, dt[:, h0:h0 + 1], dt[:, h1:h1 + 1])
            xdt = (x2 * dt2).astype(BF16)
            ydiag, snew = [], []
            for h in (h0, h1):
                diff = cs[:, h:h + 1] - cs_t[h:h + 1, :]
                lmat = jnp.where(causal, jnp.exp(jnp.where(causal, diff, 0.0)), 0.0)
                ydiag.append(jnp.dot((cb * lmat).astype(BF16), xdt, preferred_element_type=F32))
                bw = (bg_t * dec_t[h:h + 1, :]).astype(BF16)
                snew.append(jnp.dot(bw, xdt, preferred_element_type=F32))
            prev = state_ref[p]
            yoff = jnp.dot(cg_b, prev.astype(BF16), preferred_element_type=F32)
            yoff = yoff * jnp.where(first, ecs[:, h0:h0 + 1], ecs[:, h1:h1 + 1])
            cdec = jnp.where(first[0:1, :], ecs[L - 1:L, h0:h0 + 1], ecs[L - 1:L, h1:h1 + 1])
            state_ref[p] = prev * cdec + jnp.where(first, snew[0], snew[1])
            y2 = jnp.where(first, ydiag[0], ydiag[1]) + yoff + d_ref[:, p * 128:(p + 1) * 128] * x2
            y_ref[:, p * 128:(p + 1) * 128] = y2

    gw = SSD_WIDTH // SSD_GROUPS
    for g in range(SSD_GROUPS):
        cols = slice(g * gw, (g + 1) * gw)
        yg = y_ref[:, cols] * _silu(z_ref[:, cols])
        o_ref[:, cols] = _rms(yg, ng_ref[:, cols])


def _ssd_mixer(proj, conv_w, conv_b, dt_bias_p, a_p, d_exp, norm_g, bsz, seq):
    t = proj.shape[0]
    L = SSD_CHUNK
    nc = seq // L
    row = lambda b, c: b * nc + c
    full = lambda shape: pl.BlockSpec(shape, lambda b, c: (0, 0))
    return pl.pallas_call(
        _ssd_body,
        grid=(bsz, nc),
        in_specs=[pl.BlockSpec((L, 1024), lambda b, c: (row(b, c), COL_Z)),
                  pl.BlockSpec((L, 1024), lambda b, c: (row(b, c), COL_XS)),
                  pl.BlockSpec((L, 1024), lambda b, c: (row(b, c), COL_BC)),
                  pl.BlockSpec((L, DT_PAD), lambda b, c: (row(b, c), COL_DT)),
                  full((SSD_CONV, 2048)), full((1, 2048)), full((1, DT_PAD)), full((1, DT_PAD)),
                  full((1, 1024)), full((1, 1024))],
        out_specs=pl.BlockSpec((L, 1024), lambda b, c: (row(b, c), 0)),
        out_shape=jax.ShapeDtypeStruct((t, SSD_WIDTH), F32),
        scratch_shapes=[pltpu.VMEM((L + HALO, 2048), F32),
                        pltpu.VMEM((L, 2048), F32),
                        pltpu.VMEM((SSD_PAIRS, SSD_STATE, LANES), F32),
                        pltpu.VMEM((L, 1024), F32)],
        compiler_params=pltpu.CompilerParams(dimension_semantics=("parallel", "arbitrary"),
                                             vmem_limit_bytes=VMEM_LIMIT),
        name="ssd_mixer",
    )(proj, proj, proj, proj, conv_w, conv_b, dt_bias_p, a_p, d_exp, norm_g)


def _s5_cols_to_hg(w, axis):
    shape = w.shape
    split = shape[:axis] + (S5_GROUPS, S5_GROUP_CH) + shape[axis + 1:]
    return jnp.swapaxes(w.reshape(split), axis, axis + 1).reshape(shape)


def _s5_prepare(a_re, a_im, log_dt, b_re, b_im, c_re, c_im, d_s5, glu_w, glu_b):
    G, P, H, C = S5_GROUPS, S5_STATE, S5_GROUP_CH, S5_CHUNK
    lr, li = a_re.astype(F32), a_im.astype(F32)
    dt = jnp.exp(log_dt.astype(F32))[:, None]
    mag = jnp.exp(lr * dt)
    lb_re, lb_im = mag * jnp.cos(li * dt), mag * jnp.sin(li * dt)
    den = lr * lr + li * li
    coef_re = ((lb_re - 1.0) * lr + lb_im * li) / den
    coef_im = (lb_im * lr - (lb_re - 1.0) * li) / den
    bb_re = coef_re[..., None] * b_re - coef_im[..., None] * b_im
    bb_im = coef_re[..., None] * b_im + coef_im[..., None] * b_re
    tau = jnp.arange(C + 1, dtype=F32)[:, None, None]
    pm = jnp.exp(lr * dt * tau)
    pw_re, pw_im = pm * jnp.cos(li * dt * tau), pm * jnp.sin(li * dt * tau)
    m_re = pw_re[..., None] * bb_re - pw_im[..., None] * bb_im
    m_im = pw_re[..., None] * bb_im + pw_im[..., None] * bb_re
    e_re = c_re * pw_re[:, :, None, :] - c_im * pw_im[:, :, None, :]
    e_im = c_re * pw_im[:, :, None, :] + c_im * pw_re[:, :, None, :]
    kern = (jnp.einsum('ghp,tgpk->tghk', c_re, m_re[:C], precision=HIGHEST)
            - jnp.einsum('ghp,tgpk->tghk', c_im, m_im[:C], precision=HIGHEST))
    j_in = jnp.arange(C)[:, None]
    j_out = jnp.arange(C)[None, :]
    lag = j_out - j_in
    toep = jnp.where((lag >= 0)[:, :, None, None, None], kern[jnp.clip(lag, 0, C - 1)], 0.0)
    toep = toep.transpose(2, 0, 4, 1, 3).reshape(G, C * H, C * H)
    rev = jnp.arange(C - 1, -1, -1)
    wst_re = m_re[rev].transpose(1, 0, 3, 2).reshape(G, C * H, P)
    wst_im = m_im[rev].transpose(1, 0, 3, 2).reshape(G, C * H, P)
    wout_re = e_re[1:].transpose(1, 3, 0, 2).reshape(G, P, C * H)
    wout_im = (-e_im[1:]).transpose(1, 3, 0, 2).reshape(G, P, C * H)
    padp = lambda v, axis: jnp.pad(v, [(0, LANES - P) if a == axis else (0, 0) for a in range(v.ndim)])
    lam_re, lam_im = padp(pw_re[C][:, None, :], 2), padp(pw_im[C][:, None, :], 2)
    dvec = jnp.tile(d_s5.astype(F32), (1, C))[:, None, :]
    eye = jnp.eye(C, dtype=F32)
    wglu = (eye[None, :, None, :, None] * glu_w.astype(F32)[:, None, :, None, :]).reshape(G, C * H, C * H)
    bglu = jnp.tile(glu_b.astype(F32), (1, C))[:, None, :]
    return (toep.astype(BF16), padp(wst_re, 2).astype(BF16), padp(wst_im, 2).astype(BF16), lam_re, lam_im,
            padp(wout_re, 1).astype(BF16), padp(wout_im, 1).astype(BF16), dvec, wglu.astype(BF16), bglu)


def _s5_body(nb, x_ref, toep_ref, wsr_ref, wsi_ref, lr_ref, li_ref, wor_ref, woi_ref, d_ref,
             wg_ref, bg_ref, o_ref, sr_ref, si_ref, pr_ref, pi_ref):
    x = x_ref[0]
    xb = x.astype(BF16)
    sr_ref[...] = jnp.dot(xb, wsr_ref[0], preferred_element_type=F32)
    si_ref[...] = jnp.dot(xb, wsi_ref[0], preferred_element_type=F32)
    lam_r = jnp.broadcast_to(lr_ref[0], (nb, LANES))
    lam_i = jnp.broadcast_to(li_ref[0], (nb, LANES))
    n_chunks = x.shape[0] // nb

    def step(c, carry):
        xr, xi = carry
        rows = pl.ds(c, nb, stride=n_chunks)
        pr_ref[rows, :] = xr
        pi_ref[rows, :] = xi
        nr = lam_r * xr - lam_i * xi + sr_ref[rows, :]
        ni = lam_r * xi + lam_i * xr + si_ref[rows, :]
        return nr, ni

    zero = jnp.zeros((nb, LANES), F32)
    lax.fori_loop(0, n_chunks, step, (zero, zero))
    y = (jnp.dot(xb, toep_ref[0], preferred_element_type=F32)
         + jnp.dot(pr_ref[...].astype(BF16), wor_ref[0], preferred_element_type=F32)
         + jnp.dot(pi_ref[...].astype(BF16), woi_ref[0], preferred_element_type=F32)
         + d_ref[0] * x)
    v = _gelu_exact(y)
    gate = jax.nn.sigmoid(jnp.dot(v.astype(BF16), wg_ref[0], preferred_element_type=F32) + bg_ref[0])
    o_ref[0] = v * gate


def _s5_mixer(xg, prep, nb):
    G, R, W = xg.shape
    toep, wsr, wsi, lam_re, lam_im, wor, woi, dvec, wglu, bglu = prep
    P = LANES
    per_g = lambda shape: pl.BlockSpec((1,) + shape, lambda g: (g, 0, 0))
    return pl.pallas_call(
        functools.partial(_s5_body, nb),
        grid=(G,),
        in_specs=[per_g((R, W)), per_g((W, W)), per_g((W, P)), per_g((W, P)), per_g((1, P)), per_g((1, P)),
                  per_g((P, W)), per_g((P, W)), per_g((1, W)), per_g((W, W)), per_g((1, W))],
        out_specs=per_g((R, W)),
        out_shape=jax.ShapeDtypeStruct((G, R, W), F32),
        scratch_shapes=[pltpu.VMEM((R, P), F32)] * 4,
        compiler_params=pltpu.CompilerParams(dimension_semantics=("parallel",),
                                             vmem_limit_bytes=VMEM_LIMIT),
        name="s5_mixer",
    )(xg, toep, wsr, wsi, lam_re, lam_im, wor, woi, dvec, wglu, bglu)


def _outproj_body(ys_ref, s5_ref, x_ref, g1_ref, sc2_ref, sh2_ref, ns5_ref, n2_ref, wo_ref, wq_ref,
                  x1_ref, h2_ref, q_ref):
    a = ys_ref[...].astype(BF16)
    b = _rms(s5_ref[...], ns5_ref[...]).astype(BF16)
    mix = (jnp.dot(a, wo_ref[0:SSD_WIDTH, :], preferred_element_type=F32)
           + jnp.dot(b, wo_ref[SSD_WIDTH:, :], preferred_element_type=F32))
    x1 = x_ref[...] + g1_ref[0] * mix
    x1_ref[...] = x1
    h2 = _rms(x1, n2_ref[...]) * (1.0 + sc2_ref[0]) + sh2_ref[0]
    h2_ref[...] = h2
    q = jnp.dot(h2.astype(BF16), wq_ref[...], preferred_element_type=F32).astype(BF16)
    for hc in range(2 * PEER_HEADS):
        q_ref[hc] = q[:, hc * PEER_HALF:(hc + 1) * PEER_HALF]


def _out_proj(yssd, s5o, x2d, gate1, scale2, shift2, ns5, n2, w_out, w_query, seq):
    t, d = x2d.shape
    nhc = w_query.shape[1] // PEER_HALF
    tm = 256
    per_b = seq // tm
    rowblk = lambda w: pl.BlockSpec((tm, w), lambda i: (i, 0))
    modblk = pl.BlockSpec((1, 1, d), lambda i: (i // per_b, 0, 0))
    full = lambda shape: pl.BlockSpec(shape, lambda i: (0, 0))
    return pl.pallas_call(
        _outproj_body,
        grid=(t // tm,),
        in_specs=[rowblk(d), rowblk(d), rowblk(d), modblk, modblk, modblk, full((1, d)), full((1, d)),
                  full(w_out.shape), full(w_query.shape)],
        out_specs=[rowblk(d), rowblk(d), pl.BlockSpec((nhc, tm, PEER_HALF), lambda i: (0, i, 0))],
        out_shape=[jax.ShapeDtypeStruct((t, d), F32), jax.ShapeDtypeStruct((t, d), F32),
                   jax.ShapeDtypeStruct((nhc, t, PEER_HALF), BF16)],
        compiler_params=pltpu.CompilerParams(dimension_semantics=("parallel",),
                                             vmem_limit_bytes=VMEM_LIMIT),
        name="out_proj_query",
    )(yssd, s5o, x2d, gate1, scale2, shift2, ns5, n2, w_out, w_query)


def _topk_rows(s, n_out, payload=None):
    n = s.shape[0]
    iota = lax.broadcasted_iota(I32, s.shape, 0)
    vals, picks = [], []
    for _ in range(n_out):
        m = jnp.max(s, axis=0, keepdims=True)
        am = jnp.min(jnp.where(s == m, iota, n), axis=0, keepdims=True)
        hit = iota == am
        vals.append(m)
        picks.append(am if payload is None else jnp.max(jnp.where(hit, payload, -1), axis=0, keepdims=True))
        s = jnp.where(hit, -jnp.inf, s)
        yield
    return jnp.concatenate(vals, axis=0), jnp.concatenate(picks, axis=0)


def _route_head_steps(q1, k1, q2, k2):
    K = PEER_TOPK
    nt = (((1,), (1,)), ((), ()))
    s1, i1 = yield from _topk_rows(lax.dot_general(k1, q1.astype(BF16), nt, preferred_element_type=F32), K)
    s2, i2 = yield from _topk_rows(lax.dot_general(k2, q2.astype(BF16), nt, preferred_element_type=F32), K)
    cs = [s1[0:1, :] + s2] + [s1[a:a + 1, :] + s2[0:8, :] for a in range(1, 8)] + [s1[8:K, :] + s2[0:1, :]]
    ci = ([i1[0:1, :] * PEER_NKEYS + i2] + [i1[a:a + 1, :] * PEER_NKEYS + i2[0:8, :] for a in range(1, 8)]
          + [i1[8:K, :] * PEER_NKEYS + i2[0:1, :]])
    top_s, ids = yield from _topk_rows(jnp.concatenate(cs, axis=0), K, payload=jnp.concatenate(ci, axis=0))
    e = jnp.exp(top_s - top_s[0:1, :])
    return e / jnp.sum(e, axis=0, keepdims=True), ids


def _run(gen, steps=None):
    try:
        while steps is None or steps > 0:
            next(gen)
            steps = None if steps is None else steps - 1
    except StopIteration as done:
        return done.value
    return None


SUBLANES = 8
PEER_TOK_TILE = 512
PEER_RING = 16
PEER_AHEAD = 8
PEER_JGROUPS = PEER_SLOTS // SUBLANES
PEER_ROUTE_STEPS = 3
SLAB_ROWS = 2 * SUBLANES


def _sublane_sums(p, sub):
    m4, m2, m1 = sub < 4, (sub & 2) == 0, (sub & 1) == 0
    z = jnp.where(m4, p[0:4], p[4:8]) + pltpu.roll(jnp.where(m4, p[4:8], p[0:4]), 4, 1)
    y = jnp.where(m2, z[0:2] + pltpu.roll(z[0:2], 6, 1), z[2:4] + pltpu.roll(z[2:4], 2, 1))
    return jnp.where(m1, y[0] + pltpu.roll(y[0], 7, 0), y[1] + pltpu.roll(y[1], 1, 0))


def _peer_body(q_ref, keys_ref, h_ref, x1_ref, g2_ref, nf_ref, uv_ref, o_ref,
               buf_ref, po_ref, wb_ref, gates_ref, eidv_ref, eidt_ref, eid_smem, sem, sem_eid):
    i = pl.program_id(0)
    n_tiles = pl.num_programs(0) - 1
    wslot = i % 2
    rslot = 1 - wslot
    n_blocks = PEER_TOK_TILE // PEER_RING
    assert PEER_TOK_TILE // LANES * PEER_HEADS == n_blocks
    sub = lax.broadcasted_iota(I32, (SUBLANES, LANES), 0)
    lane = lax.broadcasted_iota(I32, (SUBLANES, LANES), 1)
    lane_s = lax.broadcasted_iota(I32, (PEER_SLOTS, LANES), 1)

    def route_unit(b):
        rb, h = b // PEER_HEADS, b % PEER_HEADS
        toks = pl.ds(pl.multiple_of(rb * LANES, LANES), LANES)
        gates, ids = yield from _route_head_steps(q_ref[2 * h, toks, :], keys_ref[2 * h],
                                                  q_ref[2 * h + 1, toks, :], keys_ref[2 * h + 1])
        rows = pl.ds(pl.multiple_of(h * PEER_TOPK, PEER_TOPK), PEER_TOPK)
        gates_ref[wslot, rb, rows, :] = gates
        eidv_ref[rows, :] = ids

    def publish_ids(b):
        rb, h = b // PEER_HEADS, b % PEER_HEADS

        @pl.when(h == PEER_HEADS - 1)
        def _():
            toks = pl.ds(pl.multiple_of(rb * LANES, LANES), LANES)
            eidt_ref[...] = eidv_ref[...].astype(F32).T.astype(I32)
            cp = pltpu.make_async_copy(eidt_ref, eid_smem.at[wslot, toks, :], sem_eid)
            cp.start()
            cp.wait()

    def issue_rows(par, t, slot, k0, k1):
        for k in range(k0, k1):
            e = eid_smem[par, t, k]
            pltpu.make_async_copy(uv_ref.at[e], buf_ref.at[slot, k], sem.at[slot]).start(priority=k % 2)

    def wait_rows(slot):
        pltpu.make_async_copy(uv_ref.at[pl.ds(0, PEER_SLOTS)], buf_ref.at[slot], sem.at[slot]).wait()

    def token(t, slot, h_rows):
        t_next = (t + PEER_AHEAD) % PEER_TOK_TILE
        par_next = jnp.where(t + PEER_AHEAD >= PEER_TOK_TILE, wslot, rslot)
        nslot = (slot + PEER_AHEAD) % PEER_RING
        per = PEER_SLOTS // 2 // PEER_JGROUPS
        wait_rows(slot)
        r = slot % SUBLANES
        h_t = jnp.concatenate([h_rows[r:r + 1, LANES * s:LANES * (s + 1)] for s in range(SUBLANES)], axis=0)
        dense = jnp.zeros((SUBLANES, LANES), F32)
        for j in range(PEER_JGROUPS):
            prods = buf_ref[slot, SUBLANES * j:SUBLANES * (j + 1), 0:SUBLANES, :] * h_t[None]
            col = jnp.sum(_sublane_sums(prods, sub), axis=1, keepdims=True)
            dense = jnp.where(lane == j, col, dense)
            issue_rows(par_next, t_next, nslot, j * per, (j + 1) * per)
        act = _gelu_exact(dense)
        gblk = gates_ref[rslot, t // LANES]
        gcol = jnp.sum(jnp.where(lane_s == t % LANES, gblk, 0.0), axis=1, keepdims=True)
        for j in range(PEER_JGROUPS):
            rows = slice(SUBLANES * j, SUBLANES * (j + 1))
            wb_ref[rows, :] = jnp.broadcast_to(act[:, j:j + 1] * gcol[rows, :], (SUBLANES, LANES))

        accs = [jnp.zeros((SUBLANES, LANES), F32) for _ in range(4)]
        for k in range(PEER_SLOTS):
            wk = jnp.broadcast_to(wb_ref[k:k + 1, :], (SUBLANES, LANES))
            accs[k % 4] = accs[k % 4] + buf_ref[slot, k, SUBLANES:SLAB_ROWS, :] * wk
            if k % SUBLANES == SUBLANES - 1:
                j = k // SUBLANES
                issue_rows(par_next, t_next, nslot, PEER_SLOTS // 2 + j * per, PEER_SLOTS // 2 + (j + 1) * per)
        out = (accs[0] + accs[1]) + (accs[2] + accs[3])
        return jnp.concatenate([out[s:s + 1, :] for s in range(SUBLANES)], axis=1)

    @pl.when(i == 0)
    def _():
        def first(b, _):
            _run(route_unit(b))
            publish_ids(b)
            return 0

        lax.fori_loop(0, n_blocks, first, 0)

    @pl.when(i == 1)
    def _():
        def prologue(t, _):
            issue_rows(rslot, t, t, 0, PEER_SLOTS)
            return 0

        lax.fori_loop(0, PEER_AHEAD, prologue, 0)

    @pl.when(i > 0)
    def _():
        def block(b, _):
            routing = route_unit(b)
            base = pl.multiple_of(b * PEER_RING, PEER_RING)
            rows = []
            for slot in range(PEER_RING):
                if slot % SUBLANES == 0:
                    h_rows = h_ref[pl.ds(pl.multiple_of(base + slot, SUBLANES), SUBLANES), :]
                rows.append(token(base + slot, slot, h_rows))
                _run(routing, steps=PEER_ROUTE_STEPS)
            _run(routing)
            po_ref[pl.ds(base, PEER_RING), :] = jnp.concatenate(rows, axis=0)
            publish_ids(b)
            return 0

        lax.fori_loop(0, n_blocks, block, 0)
        x2 = x1_ref[...] + g2_ref[0] * po_ref[...]
        o_ref[...] = _rms(x2, nf_ref[...])

    @pl.when(i == n_tiles)
    def _():
        for slot in range(PEER_AHEAD):
            wait_rows(slot)


def _peer_ffn(q, keys, h2, x1, gate2, nf, uv3, seq):
    t, d = h2.shape
    nhc, _, dh = q.shape
    tt = PEER_TOK_TILE
    n_tiles = t // tt
    per_b = seq // tt
    prev = lambda i: jnp.maximum(i - 1, 0)
    tokblk = pl.BlockSpec((tt, d), lambda i: (prev(i), 0))
    return pl.pallas_call(
        _peer_body,
        grid=(n_tiles + 1,),
        in_specs=[pl.BlockSpec((nhc, tt, dh), lambda i: (0, jnp.minimum(i, n_tiles - 1), 0)),
                  pl.BlockSpec(keys.shape, lambda i: (0, 0, 0)),
                  tokblk, tokblk,
                  pl.BlockSpec((1, 1, d), lambda i: (prev(i) // per_b, 0, 0)),
                  pl.BlockSpec((1, d), lambda i: (0, 0)),
                  pl.BlockSpec(memory_space=pl.ANY)],
        out_specs=tokblk,
        out_shape=jax.ShapeDtypeStruct((t, d), F32),
        scratch_shapes=[pltpu.VMEM((PEER_RING, PEER_SLOTS, SLAB_ROWS, LANES), F32),
                        pltpu.VMEM((tt, d), F32),
                        pltpu.VMEM((PEER_SLOTS, LANES), F32),
                        pltpu.VMEM((2, tt // LANES, PEER_SLOTS, LANES), F32),
                        pltpu.VMEM((PEER_SLOTS, LANES), I32),
                        pltpu.VMEM((LANES, PEER_SLOTS), I32),
                        pltpu.SMEM((2, tt, PEER_SLOTS), I32),
                        pltpu.SemaphoreType.DMA((PEER_RING,)),
                        pltpu.SemaphoreType.DMA(())],
        compiler_params=pltpu.CompilerParams(dimension_semantics=("arbitrary",),
                                             vmem_limit_bytes=VMEM_LIMIT),
        name="peer_ffn",
    )(q, keys, h2, x1, gate2, nf, uv3)


def kernel(x, c, w_ada, b_ada, norm1_g, w_in, conv_w, conv_b, dt_bias, a_log, d_ssd, norm_ssd_g, s5_a_re, s5_a_im, s5_log_dt, s5_b_re, s5_b_im, s5_c_re, s5_c_im, s5_d, glu_w, glu_b, norm_s5_g, w_out, norm2_g, w_query, sub_keys, expert_u, expert_v, norm_f_g):
    bsz, seq, d = x.shape
    t = bsz * seq
    depth = w_ada.shape[0]
    xt = x.reshape(t, d)
    for l in range(depth):
        mod = _adaln_mod(c, w_ada[l], b_ada[l])
        shift1, scale1, gate1, shift2, scale2, gate2 = [m.reshape(bsz, 1, d) for m in jnp.split(mod, 6, axis=-1)]

        wz, wxbc, wdt, wu = jnp.split(w_in[l], [SSD_WIDTH, SSD_WIDTH + 2048, SSD_WIDTH + 2048 + SSD_HEADS], axis=1)
        wu = _s5_cols_to_hg(wu, axis=1)
        w_cat = jnp.concatenate([wz, wxbc, jnp.pad(wdt, ((0, 0), (0, DT_PAD - SSD_HEADS))), wu], axis=1).astype(BF16)
        proj, u = _in_proj(xt, scale1, shift1, norm1_g[l].reshape(1, d), w_cat, seq)

        pad_h = lambda v: jnp.pad(v.astype(F32), (0, DT_PAD - SSD_HEADS)).reshape(1, DT_PAD)
        yssd = _ssd_mixer(proj, conv_w[l], conv_b[l].reshape(1, -1), pad_h(dt_bias[l]),
                          pad_h(-jnp.exp(a_log[l].astype(F32))),
                          jnp.repeat(d_ssd[l].astype(F32), SSD_HEAD_DIM).reshape(1, SSD_WIDTH),
                          norm_ssd_g[l].reshape(1, SSD_WIDTH), bsz, seq)

        blk5 = S5_CHUNK * S5_GROUP_CH
        xg = u.reshape(t * S5_GROUP_CH, S5_GROUPS).T.reshape(S5_GROUPS, t // S5_CHUNK, blk5)
        prep = _s5_prepare(s5_a_re[l], s5_a_im[l], s5_log_dt[l], s5_b_re[l], s5_b_im[l], s5_c_re[l],
                           s5_c_im[l], s5_d[l], glu_w[l], glu_b[l])
        og = _s5_mixer(xg, prep, bsz)
        s5o = og.reshape(S5_GROUPS, t * S5_GROUP_CH).T.reshape(t, S5_WIDTH)

        w_o = jnp.concatenate([w_out[l][:SSD_WIDTH], _s5_cols_to_hg(w_out[l][SSD_WIDTH:], axis=0)], axis=0)
        x1, h2, q = _out_proj(yssd, s5o, xt, gate1, scale2, shift2, _s5_cols_to_hg(norm_s5_g[l], axis=0).reshape(1, -1),
                              norm2_g[l].reshape(1, d), w_o.astype(BF16), w_query[l].astype(BF16), seq)

        keys = sub_keys[l].reshape(2 * PEER_HEADS, PEER_NKEYS, PEER_HALF).astype(BF16)
        n_exp = expert_u.shape[1]
        uv3 = jnp.concatenate([expert_u[l].reshape(n_exp, SUBLANES, LANES),
                               expert_v[l].reshape(n_exp, SUBLANES, LANES)], axis=1)
        last = l == depth - 1
        assert last, "the final RMSNorm is fused into the last layer's PEER kernel"
        xt = _peer_ffn(q, keys, h2, x1, gate2, norm_f_g.reshape(1, d), uv3, seq)
    return xt.reshape(bsz, seq, d)
```

```python
import functools

import jax
import jax.numpy as jnp
from jax import lax
from jax.experimental import pallas as pl
from jax.experimental.pallas import tpu as pltpu

F32 = jnp.float32
BF16 = jnp.bfloat16
I32 = jnp.int32

D_MODEL = 1024
SSD_WIDTH = 1024
SSD_HEAD_DIM = 64
SSD_HEADS = 16
SSD_GROUPS = 4
SSD_STATE = 128
SSD_CONV = 4
SSD_CHUNK = 128
S5_WIDTH = 1024
S5_GROUP_CH = 16
S5_GROUPS = 64
S5_STATE = 64
S5_CHUNK = 16
PEER_HEADS = 8
PEER_NKEYS = 128
PEER_TOPK = 16
PEER_HALF = 128
PEER_SLOTS = PEER_HEADS * PEER_TOPK
EPS = 1e-6
LANES = 128
DT_PAD = LANES
SSD_PROJ_WIDTH = SSD_WIDTH + 2 * 1024 + DT_PAD
COL_Z, COL_XS, COL_BC = 0, 1, 2
COL_DT = (3 * 1024) // DT_PAD
HIGHEST = lax.Precision.HIGHEST
VMEM_LIMIT = 56 * 1024 * 1024


def _silu(v):
    return v * jax.nn.sigmoid(v)


def _gelu_exact(v):
    return 0.5 * v * (1.0 + lax.erf(v * (2.0 ** -0.5)))


def _rms(v, g):
    return v * lax.rsqrt(jnp.mean(v * v, axis=-1, keepdims=True) + EPS) * g


def _mod_body(c_ref, w_ref, b_ref, o_ref):
    o_ref[...] = jnp.dot(_silu(c_ref[...]), w_ref[...], preferred_element_type=F32) + b_ref[...]


def _adaln_mod(c, w_ada, b_ada):
    bsz, d = c.shape
    n = w_ada.shape[1]
    tn = 1024
    return pl.pallas_call(
        _mod_body,
        grid=(n // tn,),
        in_specs=[pl.BlockSpec((bsz, d), lambda j: (0, 0)),
                  pl.BlockSpec((d, tn), lambda j: (0, j)),
                  pl.BlockSpec((1, tn), lambda j: (0, j))],
        out_specs=pl.BlockSpec((bsz, tn), lambda j: (0, j)),
        out_shape=jax.ShapeDtypeStruct((bsz, n), F32),
        name="adaln_mod",
    )(c, w_ada, b_ada.reshape(1, n))


def _inproj_body(x_ref, sc_ref, sh_ref, g_ref, w_ref, o_ref, u_ref):
    h = _rms(x_ref[...], g_ref[...]) * (1.0 + sc_ref[0]) + sh_ref[0]
    p = jnp.dot(h.astype(BF16), w_ref[...], preferred_element_type=F32)
    o_ref[...] = p[:, 0:SSD_PROJ_WIDTH]
    u_ref[...] = p[:, SSD_PROJ_WIDTH:]


def _in_proj(x2d, scale1, shift1, g1, w_cat, seq):
    t, d = x2d.shape
    n = SSD_PROJ_WIDTH
    tm = 256
    per_b = seq // tm
    return pl.pallas_call(
        _inproj_body,
        grid=(t // tm,),
        in_specs=[pl.BlockSpec((tm, d), lambda i: (i, 0)),
                  pl.BlockSpec((1, 1, d), lambda i: (i // per_b, 0, 0)),
                  pl.BlockSpec((1, 1, d), lambda i: (i // per_b, 0, 0)),
                  pl.BlockSpec((1, d), lambda i: (0, 0)),
                  pl.BlockSpec(w_cat.shape, lambda i: (0, 0))],
        out_specs=[pl.BlockSpec((tm, n), lambda i: (i, 0)), pl.BlockSpec((tm, S5_WIDTH), lambda i: (i, 0))],
        out_shape=[jax.ShapeDtypeStruct((t, n), F32), jax.ShapeDtypeStruct((t, S5_WIDTH), F32)],
        compiler_params=pltpu.CompilerParams(dimension_semantics=("parallel",),
                                             vmem_limit_bytes=VMEM_LIMIT),
        name="in_proj",
    )(x2d, scale1, shift1, g1, w_cat)


HALO = 8
SSD_PAIRS = SSD_HEADS // 2


def _ssd_body(z_ref, xs_ref, bc_ref, dt_ref, cw_ref, cb_ref, dtb_ref, a_ref, d_ref, ng_ref,
              o_ref, win_ref, act_ref, state_ref, y_ref):
    c = pl.program_id(1)
    L = SSD_CHUNK

    @pl.when(c == 0)
    def _():
        win_ref[0:HALO, :] = jnp.zeros((HALO, 2048), F32)
        state_ref[...] = jnp.zeros(state_ref.shape, F32)

    @pl.when(c > 0)
    def _():
        win_ref[0:HALO, :] = win_ref[L:L + HALO, :]

    win_ref[HALO:HALO + L, 0:1024] = xs_ref[...]
    win_ref[HALO:HALO + L, 1024:2048] = bc_ref[...]

    for sl in range(8):
        cols = slice(sl * 256, (sl + 1) * 256)
        acc = jnp.broadcast_to(cb_ref[:, cols], (L, 256))
        for k in range(SSD_CONV):
            r0 = HALO - (SSD_CONV - 1) + k
            acc = acc + win_ref[r0:r0 + L, cols] * cw_ref[k:k + 1, cols]
        act_ref[:, cols] = _silu(acc)

    dtv = dt_ref[...] + dtb_ref[...]
    dt = jnp.maximum(dtv, 0.0) + jnp.log1p(jnp.exp(-jnp.abs(dtv)))
    da = dt * a_ref[...]
    row = lax.broadcasted_iota(I32, (L, L), 0)
    col = lax.broadcasted_iota(I32, (L, L), 1)
    causal = row >= col
    tril = jnp.where(causal, 1.0, 0.0).astype(F32)
    cs = jnp.dot(tril, da, precision=HIGHEST, preferred_element_type=F32)
    cs_t = cs.T
    dec_t = jnp.exp(cs_t[:, L - 1:L] - cs_t)
    ecs = jnp.exp(cs)
    lane = lax.broadcasted_iota(I32, (L, LANES), 1)
    first = lane < SSD_HEAD_DIM

    for g in range(SSD_GROUPS):
        bg = act_ref[:, 1024 + g * 128:1024 + (g + 1) * 128]
        cg = act_ref[:, 1536 + g * 128:1536 + (g + 1) * 128]
        bg_t = bg.T
        cg_b = cg.astype(BF16)
        cb = jnp.dot(cg_b, bg_t.astype(BF16), preferred_element_type=F32)
        for r in range(2):
            p = g * 2 + r
            h0, h1 = 2 * p, 2 * p + 1
            x2 = act_ref[:, p * 128:(p + 1) * 128]
            dt2 = jnp.where(first, dt[:, h0:h0 + 1], dt[:, h1:h1 + 1])
            xdt = (x2 * dt2).astype(BF16)
            ydiag, snew = [], []
            for h in (h0, h1):
                diff = cs[:, h:h + 1] - cs_t[h:h + 1, :]
                lmat = jnp.where(causal, jnp.exp(jnp.where(causal, diff, 0.0)), 0.0)
                ydiag.append(jnp.dot((cb * lmat).astype(BF16), xdt, preferred_element_type=F32))
                bw = (bg_t * dec_t[h:h + 1, :]).astype(BF16)
                snew.append(jnp.dot(bw, xdt, preferred_element_type=F32))
            prev = state_ref[p]
            yoff = jnp.dot(cg_b, prev.astype(BF16), preferred_element_type=F32)
            yoff = yoff * jnp.where(first, ecs[:, h0:h0 + 1], ecs[:, h1:h1 + 1])
            cdec = jnp.where(first[0:1, :], ecs[L - 1:L, h0:h0 + 1], ecs[L - 1:L, h1:h1 + 1])
            state_ref[p] = prev * cdec + jnp.where(first, snew[0], snew[1])
            y2 = jnp.where(first, ydiag[0], ydiag[1]) + yoff + d_ref[:, p * 128:(p + 1) * 128] * x2
            y_ref[:, p * 128:(p + 1) * 128] = y2

    gw = SSD_WIDTH // SSD_GROUPS
    for g in range(SSD_GROUPS):
        cols = slice(g * gw, (g + 1) * gw)
        yg = y_ref[:, cols] * _silu(z_ref[:, cols])
        o_ref[:, cols] = _rms(yg, ng_ref[:, cols])


def _ssd_mixer(proj, conv_w, conv_b, dt_bias_p, a_p, d_exp, norm_g, bsz, seq):
    t = proj.shape[0]
    L = SSD_CHUNK
    nc = seq // L
    row = lambda b, c: b * nc + c
    full = lambda shape: pl.BlockSpec(shape, lambda b, c: (0, 0))
    return pl.pallas_call(
        _ssd_body,
        grid=(bsz, nc),
        in_specs=[pl.BlockSpec((L, 1024), lambda b, c: (row(b, c), COL_Z)),
                  pl.BlockSpec((L, 1024), lambda b, c: (row(b, c), COL_XS)),
                  pl.BlockSpec((L, 1024), lambda b, c: (row(b, c), COL_BC)),
                  pl.BlockSpec((L, DT_PAD), lambda b, c: (row(b, c), COL_DT)),
                  full((SSD_CONV, 2048)), full((1, 2048)), full((1, DT_PAD)), full((1, DT_PAD)),
                  full((1, 1024)), full((1, 1024))],
        out_specs=pl.BlockSpec((L, 1024), lambda b, c: (row(b, c), 0)),
        out_shape=jax.ShapeDtypeStruct((t, SSD_WIDTH), F32),
        scratch_shapes=[pltpu.VMEM((L + HALO, 2048), F32),
                        pltpu.VMEM((L, 2048), F32),
                        pltpu.VMEM((SSD_PAIRS, SSD_STATE, LANES), F32),
                        pltpu.VMEM((L, 1024), F32)],
        compiler_params=pltpu.CompilerParams(dimension_semantics=("parallel", "arbitrary"),
                                             vmem_limit_bytes=VMEM_LIMIT),
        name="ssd_mixer",
    )(proj, proj, proj, proj, conv_w, conv_b, dt_bias_p, a_p, d_exp, norm_g)


def _s5_cols_to_hg(w, axis):
    shape = w.shape
    split = shape[:axis] + (S5_GROUPS, S5_GROUP_CH) + shape[axis + 1:]
    return jnp.swapaxes(w.reshape(split), axis, axis + 1).reshape(shape)


def _s5_prepare(a_re, a_im, log_dt, b_re, b_im, c_re, c_im, d_s5, glu_w, glu_b):
    G, P, H, C = S5_GROUPS, S5_STATE, S5_GROUP_CH, S5_CHUNK
    lr, li = a_re.astype(F32), a_im.astype(F32)
    dt = jnp.exp(log_dt.astype(F32))[:, None]
    mag = jnp.exp(lr * dt)
    lb_re, lb_im = mag * jnp.cos(li * dt), mag * jnp.sin(li * dt)
    den = lr * lr + li * li
    coef_re = ((lb_re - 1.0) * lr + lb_im * li) / den
    coef_im = (lb_im * lr - (lb_re - 1.0) * li) / den
    bb_re = coef_re[..., None] * b_re - coef_im[..., None] * b_im
    bb_im = coef_re[..., None] * b_im + coef_im[..., None] * b_re
    tau = jnp.arange(C + 1, dtype=F32)[:, None, None]
    pm = jnp.exp(lr * dt * tau)
    pw_re, pw_im = pm * jnp.cos(li * dt * tau), pm * jnp.sin(li * dt * tau)
    m_re = pw_re[..., None] * bb_re - pw_im[..., None] * bb_im
    m_im = pw_re[..., None] * bb_im + pw_im[..., None] * bb_re
    e_re = c_re * pw_re[:, :, None, :] - c_im * pw_im[:, :, None, :]
    e_im = c_re * pw_im[:, :, None, :] + c_im * pw_re[:, :, None, :]
    kern = (jnp.einsum('ghp,tgpk->tghk', c_re, m_re[:C], precision=HIGHEST)
            - jnp.einsum('ghp,tgpk->tghk', c_im, m_im[:C], precision=HIGHEST))
    j_in = jnp.arange(C)[:, None]
    j_out = jnp.arange(C)[None, :]
    lag = j_out - j_in
    toep = jnp.where((lag >= 0)[:, :, None, None, None], kern[jnp.clip(lag, 0, C - 1)], 0.0)
    toep = toep.transpose(2, 0, 4, 1, 3).reshape(G, C * H, C * H)
    rev = jnp.arange(C - 1, -1, -1)
    wst_re = m_re[rev].transpose(1, 0, 3, 2).reshape(G, C * H, P)
    wst_im = m_im[rev].transpose(1, 0, 3, 2).reshape(G, C * H, P)
    wout_re = e_re[1:].transpose(1, 3, 0, 2).reshape(G, P, C * H)
    wout_im = (-e_im[1:]).transpose(1, 3, 0, 2).reshape(G, P, C * H)
    padp = lambda v, axis: jnp.pad(v, [(0, LANES - P) if a == axis else (0, 0) for a in range(v.ndim)])
    lam_re, lam_im = padp(pw_re[C][:, None, :], 2), padp(pw_im[C][:, None, :], 2)
    dvec = jnp.tile(d_s5.astype(F32), (1, C))[:, None, :]
    eye = jnp.eye(C, dtype=F32)
    wglu = (eye[None, :, None, :, None] * glu_w.astype(F32)[:, None, :, None, :]).reshape(G, C * H, C * H)
    bglu = jnp.tile(glu_b.astype(F32), (1, C))[:, None, :]
    return (toep.astype(BF16), padp(wst_re, 2).astype(BF16), padp(wst_im, 2).astype(BF16), lam_re, lam_im,
            padp(wout_re, 1).astype(BF16), padp(wout_im, 1).astype(BF16), dvec, wglu.astype(BF16), bglu)


def _s5_body(nb, x_ref, toep_ref, wsr_ref, wsi_ref, lr_ref, li_ref, wor_ref, woi_ref, d_ref,
             wg_ref, bg_ref, o_ref, sr_ref, si_ref, pr_ref, pi_ref):
    x = x_ref[0]
    xb = x.astype(BF16)
    sr_ref[...] = jnp.dot(xb, wsr_ref[0], preferred_element_type=F32)
    si_ref[...] = jnp.dot(xb, wsi_ref[0], preferred_element_type=F32)
    lam_r = jnp.broadcast_to(lr_ref[0], (nb, LANES))
    lam_i = jnp.broadcast_to(li_ref[0], (nb, LANES))
    n_chunks = x.shape[0] // nb

    def step(c, carry):
        xr, xi = carry
        rows = pl.ds(c, nb, stride=n_chunks)
        pr_ref[rows, :] = xr
        pi_ref[rows, :] = xi
        nr = lam_r * xr - lam_i * xi + sr_ref[rows, :]
        ni = lam_r * xi + lam_i * xr + si_ref[rows, :]
        return nr, ni

    zero = jnp.zeros((nb, LANES), F32)
    lax.fori_loop(0, n_chunks, step, (zero, zero))
    y = (jnp.dot(xb, toep_ref[0], preferred_element_type=F32)
         + jnp.dot(pr_ref[...].astype(BF16), wor_ref[0], preferred_element_type=F32)
         + jnp.dot(pi_ref[...].astype(BF16), woi_ref[0], preferred_element_type=F32)
         + d_ref[0] * x)
    v = _gelu_exact(y)
    gate = jax.nn.sigmoid(jnp.dot(v.astype(BF16), wg_ref[0], preferred_element_type=F32) + bg_ref[0])
    o_ref[0] = v * gate


def _s5_mixer(xg, prep, nb):
    G, R, W = xg.shape
    toep, wsr, wsi, lam_re, lam_im, wor, woi, dvec, wglu, bglu = prep
    P = LANES
    per_g = lambda shape: pl.BlockSpec((1,) + shape, lambda g: (g, 0, 0))
    return pl.pallas_call(
        functools.partial(_s5_body, nb),
        grid=(G,),
        in_specs=[per_g((R, W)), per_g((W, W)), per_g((W, P)), per_g((W, P)), per_g((1, P)), per_g((1, P)),
                  per_g((P, W)), per_g((P, W)), per_g((1, W)), per_g((W, W)), per_g((1, W))],
        out_specs=per_g((R, W)),
        out_shape=jax.ShapeDtypeStruct((G, R, W), F32),
        scratch_shapes=[pltpu.VMEM((R, P), F32)] * 4,
        compiler_params=pltpu.CompilerParams(dimension_semantics=("parallel",),
                                             vmem_limit_bytes=VMEM_LIMIT),
        name="s5_mixer",
    )(xg, toep, wsr, wsi, lam_re, lam_im, wor, woi, dvec, wglu, bglu)


def _outproj_body(ys_ref, s5_ref, x_ref, g1_ref, sc2_ref, sh2_ref, ns5_ref, n2_ref, wo_ref, wq_ref,
                  x1_ref, h2_ref, q_ref):
    a = ys_ref[...].astype(BF16)
    b = _rms(s5_ref[...], ns5_ref[...]).astype(BF16)
    mix = (jnp.dot(a, wo_ref[0:SSD_WIDTH, :], preferred_element_type=F32)
           + jnp.dot(b, wo_ref[SSD_WIDTH:, :], preferred_element_type=F32))
    x1 = x_ref[...] + g1_ref[0] * mix
    x1_ref[...] = x1
    h2 = _rms(x1, n2_ref[...]) * (1.0 + sc2_ref[0]) + sh2_ref[0]
    h2_ref[...] = h2
    q = jnp.dot(h2.astype(BF16), wq_ref[...], preferred_element_type=F32).astype(BF16)
    for hc in range(2 * PEER_HEADS):
        q_ref[hc] = q[:, hc * PEER_HALF:(hc + 1) * PEER_HALF]


def _out_proj(yssd, s5o, x2d, gate1, scale2, shift2, ns5, n2, w_out, w_query, seq):
    t, d = x2d.shape
    nhc = w_query.shape[1] // PEER_HALF
    tm = 256
    per_b = seq // tm
    rowblk = lambda w: pl.BlockSpec((tm, w), lambda i: (i, 0))
    modblk = pl.BlockSpec((1, 1, d), lambda i: (i // per_b, 0, 0))
    full = lambda shape: pl.BlockSpec(shape, lambda i: (0, 0))
    return pl.pallas_call(
        _outproj_body,
        grid=(t // tm,),
        in_specs=[rowblk(d), rowblk(d), rowblk(d), modblk, modblk, modblk, full((1, d)), full((1, d)),
                  full(w_out.shape), full(w_query.shape)],
        out_specs=[rowblk(d), rowblk(d), pl.BlockSpec((nhc, tm, PEER_HALF), lambda i: (0, i, 0))],
        out_shape=[jax.ShapeDtypeStruct((t, d), F32), jax.ShapeDtypeStruct((t, d), F32),
                   jax.ShapeDtypeStruct((nhc, t, PEER_HALF), BF16)],
        compiler_params=pltpu.CompilerParams(dimension_semantics=("parallel",),
                                             vmem_limit_bytes=VMEM_LIMIT),
        name="out_proj_query",
    )(yssd, s5o, x2d, gate1, scale2, shift2, ns5, n2, w_out, w_query)


def _topk_rows(s, n_out, payload=None):
    n = s.shape[0]
    iota = lax.broadcasted_iota(I32, s.shape, 0)
    vals, picks = [], []
    for _ in range(n_out):
        m = jnp.max(s, axis=0, keepdims=True)
        am = jnp.min(jnp.where(s == m, iota, n), axis=0, keepdims=True)
        hit = iota == am
        vals.append(m)
        picks.append(am if payload is None else jnp.max(jnp.where(hit, payload, -1), axis=0, keepdims=True))
        s = jnp.where(hit, -jnp.inf, s)
        yield
    return jnp.concatenate(vals, axis=0), jnp.concatenate(picks, axis=0)


def _route_head_steps(q1, k1, q2, k2):
    K = PEER_TOPK
    nt = (((1,), (1,)), ((), ()))
    s1, i1 = yield from _topk_rows(lax.dot_general(k1, q1.astype(BF16), nt, preferred_element_type=F32), K)
    s2, i2 = yield from _topk_rows(lax.dot_general(k2, q2.astype(BF16), nt, preferred_element_type=F32), K)
    cs = [s1[0:1, :] + s2] + [s1[a:a + 1, :] + s2[0:8, :] for a in range(1, 8)] + [s1[8:K, :] + s2[0:1, :]]
    ci = ([i1[0:1, :] * PEER_NKEYS + i2] + [i1[a:a + 1, :] * PEER_NKEYS + i2[0:8, :] for a in range(1, 8)]
          + [i1[8:K, :] * PEER_NKEYS + i2[0:1, :]])
    top_s, ids = yield from _topk_rows(jnp.concatenate(cs, axis=0), K, payload=jnp.concatenate(ci, axis=0))
    e = jnp.exp(top_s - top_s[0:1, :])
    return e / jnp.sum(e, axis=0, keepdims=True), ids


def _run(gen, steps=None):
    try:
        while steps is None or steps > 0:
            next(gen)
            steps = None if steps is None else steps - 1
    except StopIteration as done:
        return done.value
    return None


SUBLANES = 8
PEER_TOK_TILE = 512
PEER_RING = 16
PEER_AHEAD = 8
PEER_JGROUPS = PEER_SLOTS // SUBLANES
PEER_ROUTE_STEPS = 3
SLAB_ROWS = 2 * SUBLANES


def _pack_body(u_ref, v_ref, o_ref):
    n = u_ref.shape[0]
    for s in range(SUBLANES):
        o_ref[pl.ds(s, n, stride=SLAB_ROWS), :] = u_ref[:, LANES * s:LANES * (s + 1)]
        o_ref[pl.ds(SUBLANES + s, n, stride=SLAB_ROWS), :] = v_ref[:, LANES * s:LANES * (s + 1)]


def _pack_experts(expert_u, expert_v):
    n_exp, d = expert_u.shape
    eb = 256
    return pl.pallas_call(
        _pack_body,
        grid=(n_exp // eb,),
        in_specs=[pl.BlockSpec((eb, d), lambda i: (i, 0)), pl.BlockSpec((eb, d), lambda i: (i, 0))],
        out_specs=pl.BlockSpec((eb * SLAB_ROWS, LANES), lambda i: (i, 0)),
        out_shape=jax.ShapeDtypeStruct((n_exp * SLAB_ROWS, LANES), F32),
        compiler_params=pltpu.CompilerParams(dimension_semantics=("parallel",)),
        name="pack_experts",
    )(expert_u, expert_v)


def _sublane_sums(p, sub):
    m4, m2, m1 = sub < 4, (sub & 2) == 0, (sub & 1) == 0
    z = jnp.where(m4, p[0:4], p[4:8]) + pltpu.roll(jnp.where(m4, p[4:8], p[0:4]), 4, 1)
    y = jnp.where(m2, z[0:2] + pltpu.roll(z[0:2], 6, 1), z[2:4] + pltpu.roll(z[2:4], 2, 1))
    return jnp.where(m1, y[0] + pltpu.roll(y[0], 7, 0), y[1] + pltpu.roll(y[1], 1, 0))


def _peer_body(q_ref, keys_ref, h_ref, x1_ref, g2_ref, nf_ref, uv_ref, o_ref,
               buf_ref, po_ref, wb_ref, gates_ref, eidv_ref, eidt_ref, eid_smem, sem, sem_eid):
    i = pl.program_id(0)
    n_tiles = pl.num_programs(0) - 1
    wslot = i % 2
    rslot = 1 - wslot
    n_blocks = PEER_TOK_TILE // PEER_RING
    assert PEER_TOK_TILE // LANES * PEER_HEADS == n_blocks
    sub = lax.broadcasted_iota(I32, (SUBLANES, LANES), 0)
    lane = lax.broadcasted_iota(I32, (SUBLANES, LANES), 1)
    lane_s = lax.broadcasted_iota(I32, (PEER_SLOTS, LANES), 1)

    def route_unit(b):
        rb, h = b // PEER_HEADS, b % PEER_HEADS
        toks = pl.ds(pl.multiple_of(rb * LANES, LANES), LANES)
        gates, ids = yield from _route_head_steps(q_ref[2 * h, toks, :], keys_ref[2 * h],
                                                  q_ref[2 * h + 1, toks, :], keys_ref[2 * h + 1])
        rows = pl.ds(pl.multiple_of(h * PEER_TOPK, PEER_TOPK), PEER_TOPK)
        gates_ref[wslot, rb, rows, :] = gates
        eidv_ref[rows, :] = ids

    def publish_ids(b):
        rb, h = b // PEER_HEADS, b % PEER_HEADS

        @pl.when(h == PEER_HEADS - 1)
        def _():
            toks = pl.ds(pl.multiple_of(rb * LANES, LANES), LANES)
            eidt_ref[...] = eidv_ref[...].astype(F32).T.astype(I32)
            cp = pltpu.make_async_copy(eidt_ref, eid_smem.at[wslot, toks, :], sem_eid)
            cp.start()
            cp.wait()

    def issue_rows(par, t, slot, k0, k1):
        for k in range(k0, k1):
            e = eid_smem[par, t, k]
            pltpu.make_async_copy(uv_ref.at[e], buf_ref.at[slot, k], sem.at[slot]).start(priority=k % 2)

    def wait_rows(slot):
        pltpu.make_async_copy(uv_ref.at[pl.ds(0, PEER_SLOTS)], buf_ref.at[slot], sem.at[slot]).wait()

    def token(t, slot, h_rows):
        t_next = (t + PEER_AHEAD) % PEER_TOK_TILE
        par_next = jnp.where(t + PEER_AHEAD >= PEER_TOK_TILE, wslot, rslot)
        nslot = (slot + PEER_AHEAD) % PEER_RING
        per = PEER_SLOTS // 2 // PEER_JGROUPS
        wait_rows(slot)
        r = slot % SUBLANES
        h_t = jnp.concatenate([h_rows[r:r + 1, LANES * s:LANES * (s + 1)] for s in range(SUBLANES)], axis=0)
        dense = jnp.zeros((SUBLANES, LANES), F32)
        for j in range(PEER_JGROUPS):
            prods = buf_ref[slot, SUBLANES * j:SUBLANES * (j + 1), 0:SUBLANES, :] * h_t[None]
            col = jnp.sum(_sublane_sums(prods, sub), axis=1, keepdims=True)
            dense = jnp.where(lane == j, col, dense)
            issue_rows(par_next, t_next, nslot, j * per, (j + 1) * per)
        act = _gelu_exact(dense)
        gblk = gates_ref[rslot, t // LANES]
        gcol = jnp.sum(jnp.where(lane_s == t % LANES, gblk, 0.0), axis=1, keepdims=True)
        for j in range(PEER_JGROUPS):
            rows = slice(SUBLANES * j, SUBLANES * (j + 1))
            wb_ref[rows, :] = jnp.broadcast_to(act[:, j:j + 1] * gcol[rows, :], (SUBLANES, LANES))

        accs = [jnp.zeros((SUBLANES, LANES), F32) for _ in range(4)]
        for k in range(PEER_SLOTS):
            wk = jnp.broadcast_to(wb_ref[k:k + 1, :], (SUBLANES, LANES))
            accs[k % 4] = accs[k % 4] + buf_ref[slot, k, SUBLANES:SLAB_ROWS, :] * wk
            if k % SUBLANES == SUBLANES - 1:
                j = k // SUBLANES
                issue_rows(par_next, t_next, nslot, PEER_SLOTS // 2 + j * per, PEER_SLOTS // 2 + (j + 1) * per)
        out = (accs[0] + accs[1]) + (accs[2] + accs[3])
        return jnp.concatenate([out[s:s + 1, :] for s in range(SUBLANES)], axis=1)

    @pl.when(i == 0)
    def _():
        def first(b, _):
            _run(route_unit(b))
            publish_ids(b)
            return 0

        lax.fori_loop(0, n_blocks, first, 0)

    @pl.when(i == 1)
    def _():
        def prologue(t, _):
            issue_rows(rslot, t, t, 0, PEER_SLOTS)
            return 0

        lax.fori_loop(0, PEER_AHEAD, prologue, 0)

    @pl.when(i > 0)
    def _():
        def block(b, _):
            routing = route_unit(b)
            base = pl.multiple_of(b * PEER_RING, PEER_RING)
            rows = []
            for slot in range(PEER_RING):
                if slot % SUBLANES == 0:
                    h_rows = h_ref[pl.ds(pl.multiple_of(base + slot, SUBLANES), SUBLANES), :]
                rows.append(token(base + slot, slot, h_rows))
                _run(routing, steps=PEER_ROUTE_STEPS)
            _run(routing)
            po_ref[pl.ds(base, PEER_RING), :] = jnp.concatenate(rows, axis=0)
            publish_ids(b)
            return 0

        lax.fori_loop(0, n_blocks, block, 0)
        x2 = x1_ref[...] + g2_ref[0] * po_ref[...]
        o_ref[...] = _rms(x2, nf_ref[...])

    @pl.when(i == n_tiles)
    def _():
        for slot in range(PEER_AHEAD):
            wait_rows(slot)


def _peer_ffn(q, keys, h2, x1, gate2, nf, uv3, seq):
    t, d = h2.shape
    nhc, _, dh = q.shape
    tt = PEER_TOK_TILE
    n_tiles = t // tt
    per_b = seq // tt
    prev = lambda i: jnp.maximum(i - 1, 0)
    tokblk = pl.BlockSpec((tt, d), lambda i: (prev(i), 0))
    return pl.pallas_call(
        _peer_body,
        grid=(n_tiles + 1,),
        in_specs=[pl.BlockSpec((nhc, tt, dh), lambda i: (0, jnp.minimum(i, n_tiles - 1), 0)),
                  pl.BlockSpec(keys.shape, lambda i: (0, 0, 0)),
                  tokblk, tokblk,
                  pl.BlockSpec((1, 1, d), lambda i: (prev(i) // per_b, 0, 0)),
                  pl.BlockSpec((1, d), lambda i: (0, 0)),
                  pl.BlockSpec(memory_space=pl.ANY)],
        out_specs=tokblk,
        out_shape=jax.ShapeDtypeStruct((t, d), F32),
        scratch_shapes=[pltpu.VMEM((PEER_RING, PEER_SLOTS, SLAB_ROWS, LANES), F32),
                        pltpu.VMEM((tt, d), F32),
                        pltpu.VMEM((PEER_SLOTS, LANES), F32),
                        pltpu.VMEM((2, tt // LANES, PEER_SLOTS, LANES), F32),
                        pltpu.VMEM((PEER_SLOTS, LANES), I32),
                        pltpu.VMEM((LANES, PEER_SLOTS), I32),
                        pltpu.SMEM((2, tt, PEER_SLOTS), I32),
                        pltpu.SemaphoreType.DMA((PEER_RING,)),
                        pltpu.SemaphoreType.DMA(())],
        compiler_params=pltpu.CompilerParams(dimension_semantics=("arbitrary",),
                                             vmem_limit_bytes=VMEM_LIMIT),
        name="peer_ffn",
    )(q, keys, h2, x1, gate2, nf, uv3)


def kernel(x, c, w_ada, b_ada, norm1_g, w_in, conv_w, conv_b, dt_bias, a_log, d_ssd, norm_ssd_g, s5_a_re, s5_a_im, s5_log_dt, s5_b_re, s5_b_im, s5_c_re, s5_c_im, s5_d, glu_w, glu_b, norm_s5_g, w_out, norm2_g, w_query, sub_keys, expert_u, expert_v, norm_f_g):
    bsz, seq, d = x.shape
    t = bsz * seq
    depth = w_ada.shape[0]
    xt = x.reshape(t, d)
    for l in range(depth):
        mod = _adaln_mod(c, w_ada[l], b_ada[l])
        shift1, scale1, gate1, shift2, scale2, gate2 = [m.reshape(bsz, 1, d) for m in jnp.split(mod, 6, axis=-1)]

        wz, wxbc, wdt, wu = jnp.split(w_in[l], [SSD_WIDTH, SSD_WIDTH + 2048, SSD_WIDTH + 2048 + SSD_HEADS], axis=1)
        wu = _s5_cols_to_hg(wu, axis=1)
        w_cat = jnp.concatenate([wz, wxbc, jnp.pad(wdt, ((0, 0), (0, DT_PAD - SSD_HEADS))), wu], axis=1).astype(BF16)
        proj, u = _in_proj(xt, scale1, shift1, norm1_g[l].reshape(1, d), w_cat, seq)

        pad_h = lambda v: jnp.pad(v.astype(F32), (0, DT_PAD - SSD_HEADS)).reshape(1, DT_PAD)
        yssd = _ssd_mixer(proj, conv_w[l], conv_b[l].reshape(1, -1), pad_h(dt_bias[l]),
                          pad_h(-jnp.exp(a_log[l].astype(F32))),
                          jnp.repeat(d_ssd[l].astype(F32), SSD_HEAD_DIM).reshape(1, SSD_WIDTH),
                          norm_ssd_g[l].reshape(1, SSD_WIDTH), bsz, seq)

        blk5 = S5_CHUNK * S5_GROUP_CH
        xg = u.reshape(t * S5_GROUP_CH, S5_GROUPS).T.reshape(S5_GROUPS, t // S5_CHUNK, blk5)
        prep = _s5_prepare(s5_a_re[l], s5_a_im[l], s5_log_dt[l], s5_b_re[l], s5_b_im[l], s5_c_re[l],
                           s5_c_im[l], s5_d[l], glu_w[l], glu_b[l])
        og = _s5_mixer(xg, prep, bsz)
        s5o = og.reshape(S5_GROUPS, t * S5_GROUP_CH).T.reshape(t, S5_WIDTH)

        w_o = jnp.concatenate([w_out[l][:SSD_WIDTH], _s5_cols_to_hg(w_out[l][SSD_WIDTH:], axis=0)], axis=0)
        x1, h2, q = _out_proj(yssd, s5o, xt, gate1, scale2, shift2, _s5_cols_to_hg(norm_s5_g[l], axis=0).reshape(1, -1),
                              norm2_g[l].reshape(1, d), w_o.astype(BF16), w_query[l].astype(BF16), seq)

        keys = sub_keys[l].reshape(2 * PEER_HEADS, PEER_NKEYS, PEER_HALF).astype(BF16)
        n_exp = expert_u.shape[1]
        uv3 = _pack_experts(expert_u[l], expert_v[l]).reshape(n_exp, SLAB_ROWS, LANES)
        last = l == depth - 1
        assert last, "the final RMSNorm is fused into the last layer's PEER kernel"
        xt = _peer_ffn(q, keys, h2, x1, gate2, norm_f_g.reshape(1, d), uv3, seq)
    return xt.reshape(bsz, seq, d)
```

```python
import functools

import jax
import jax.numpy as jnp
from jax import lax
from jax.experimental import pallas as pl
from jax.experimental.pallas import tpu as pltpu

F32 = jnp.float32
BF16 = jnp.bfloat16
I32 = jnp.int32

D_MODEL = 1024
SSD_WIDTH = 1024
SSD_HEAD_DIM = 64
SSD_HEADS = 16
SSD_GROUPS = 4
SSD_STATE = 128
SSD_CONV = 4
SSD_CHUNK = 128
S5_WIDTH = 1024
S5_GROUP_CH = 16
S5_GROUPS = 64
S5_STATE = 64
S5_CHUNK = 16
PEER_HEADS = 8
PEER_NKEYS = 128
PEER_TOPK = 16
PEER_HALF = 128
PEER_SLOTS = PEER_HEADS * PEER_TOPK
EPS = 1e-6
LANES = 128
DT_PAD = LANES
SSD_PROJ_WIDTH = SSD_WIDTH + 2 * 1024 + DT_PAD
COL_Z, COL_XS, COL_BC = 0, 1, 2
COL_DT = (3 * 1024) // DT_PAD
HIGHEST = lax.Precision.HIGHEST
VMEM_LIMIT = 56 * 1024 * 1024


def _silu(v):
    return v * jax.nn.sigmoid(v)


def _gelu_exact(v):
    return 0.5 * v * (1.0 + lax.erf(v * (2.0 ** -0.5)))


def _rms(v, g):
    return v * lax.rsqrt(jnp.mean(v * v, axis=-1, keepdims=True) + EPS) * g


def _mod_body(c_ref, w_ref, b_ref, o_ref):
    o_ref[...] = jnp.dot(_silu(c_ref[...]), w_ref[...], preferred_element_type=F32) + b_ref[...]


def _adaln_mod(c, w_ada, b_ada):
    bsz, d = c.shape
    n = w_ada.shape[1]
    tn = 1024
    return pl.pallas_call(
        _mod_body,
        grid=(n // tn,),
        in_specs=[pl.BlockSpec((bsz, d), lambda j: (0, 0)),
                  pl.BlockSpec((d, tn), lambda j: (0, j)),
                  pl.BlockSpec((1, tn), lambda j: (0, j))],
        out_specs=pl.BlockSpec((bsz, tn), lambda j: (0, j)),
        out_shape=jax.ShapeDtypeStruct((bsz, n), F32),
        name="adaln_mod",
    )(c, w_ada, b_ada.reshape(1, n))


def _inproj_body(x_ref, sc_ref, sh_ref, g_ref, w_ref, o_ref, u_ref):
    h = _rms(x_ref[...], g_ref[...]) * (1.0 + sc_ref[0]) + sh_ref[0]
    p = jnp.dot(h.astype(BF16), w_ref[...], preferred_element_type=F32)
    o_ref[...] = p[:, 0:SSD_PROJ_WIDTH]
    u_ref[...] = p[:, SSD_PROJ_WIDTH:]


def _in_proj(x2d, scale1, shift1, g1, w_cat, seq):
    t, d = x2d.shape
    n = SSD_PROJ_WIDTH
    tm = 256
    per_b = seq // tm
    return pl.pallas_call(
        _inproj_body,
        grid=(t // tm,),
        in_specs=[pl.BlockSpec((tm, d), lambda i: (i, 0)),
                  pl.BlockSpec((1, 1, d), lambda i: (i // per_b, 0, 0)),
                  pl.BlockSpec((1, 1, d), lambda i: (i // per_b, 0, 0)),
                  pl.BlockSpec((1, d), lambda i: (0, 0)),
                  pl.BlockSpec(w_cat.shape, lambda i: (0, 0))],
        out_specs=[pl.BlockSpec((tm, n), lambda i: (i, 0)), pl.BlockSpec((tm, S5_WIDTH), lambda i: (i, 0))],
        out_shape=[jax.ShapeDtypeStruct((t, n), F32), jax.ShapeDtypeStruct((t, S5_WIDTH), F32)],
        compiler_params=pltpu.CompilerParams(dimension_semantics=("parallel",),
                                             vmem_limit_bytes=VMEM_LIMIT),
        name="in_proj",
    )(x2d, scale1, shift1, g1, w_cat)


HALO = 8
SSD_PAIRS = SSD_HEADS // 2


def _ssd_body(z_ref, xs_ref, bc_ref, dt_ref, cw_ref, cb_ref, dtb_ref, a_ref, d_ref, ng_ref,
              o_ref, win_ref, act_ref, state_ref, y_ref):
    c = pl.program_id(1)
    L = SSD_CHUNK

    @pl.when(c == 0)
    def _():
        win_ref[0:HALO, :] = jnp.zeros((HALO, 2048), F32)
        state_ref[...] = jnp.zeros(state_ref.shape, F32)

    @pl.when(c > 0)
    def _():
        win_ref[0:HALO, :] = win_ref[L:L + HALO, :]

    win_ref[HALO:HALO + L, 0:1024] = xs_ref[...]
    win_ref[HALO:HALO + L, 1024:2048] = bc_ref[...]

    for sl in range(8):
        cols = slice(sl * 256, (sl + 1) * 256)
        acc = jnp.broadcast_to(cb_ref[:, cols], (L, 256))
        for k in range(SSD_CONV):
            r0 = HALO - (SSD_CONV - 1) + k
            acc = acc + win_ref[r0:r0 + L, cols] * cw_ref[k:k + 1, cols]
        act_ref[:, cols] = _silu(acc)

    dtv = dt_ref[...] + dtb_ref[...]
    dt = jnp.maximum(dtv, 0.0) + jnp.log1p(jnp.exp(-jnp.abs(dtv)))
    da = dt * a_ref[...]
    row = lax.broadcasted_iota(I32, (L, L), 0)
    col = lax.broadcasted_iota(I32, (L, L), 1)
    causal = row >= col
    tril = jnp.where(causal, 1.0, 0.0).astype(F32)
    cs = jnp.dot(tril, da, precision=HIGHEST, preferred_element_type=F32)
    cs_t = cs.T
    dec_t = jnp.exp(cs_t[:, L - 1:L] - cs_t)
    ecs = jnp.exp(cs)
    lane = lax.broadcasted_iota(I32, (L, LANES), 1)
    first = lane < SSD_HEAD_DIM

    for g in range(SSD_GROUPS):
        bg = act_ref[:, 1024 + g * 128:1024 + (g + 1) * 128]
        cg = act_ref[:, 1536 + g * 128:1536 + (g + 1) * 128]
        bg_t = bg.T
        cg_b = cg.astype(BF16)
        cb = jnp.dot(cg_b, bg_t.astype(BF16), preferred_element_type=F32)
        for r in range(2):
            p = g * 2 + r
            h0, h1 = 2 * p, 2 * p + 1
            x2 = act_ref[:, p * 128:(p + 1) * 128]
            dt2 = jnp.where(first, dt[:, h0:h0 + 1], dt[:, h1:h1 + 1])
            xdt = (x2 * dt2).astype(BF16)
            ydiag, snew = [], []
            for h in (h0, h1):
                diff = cs[:, h:h + 1] - cs_t[h:h + 1, :]
                lmat = jnp.where(causal, jnp.exp(jnp.where(causal, diff, 0.0)), 0.0)
                ydiag.append(jnp.dot((cb * lmat).astype(BF16), xdt, preferred_element_type=F32))
                bw = (bg_t * dec_t[h:h + 1, :]).astype(BF16)
                snew.append(jnp.dot(bw, xdt, preferred_element_type=F32))
            prev = state_ref[p]
            yoff = jnp.dot(cg_b, prev.astype(BF16), preferred_element_type=F32)
            yoff = yoff * jnp.where(first, ecs[:, h0:h0 + 1], ecs[:, h1:h1 + 1])
            cdec = jnp.where(first[0:1, :], ecs[L - 1:L, h0:h0 + 1], ecs[L - 1:L, h1:h1 + 1])
            state_ref[p] = prev * cdec + jnp.where(first, snew[0], snew[1])
            y2 = jnp.where(first, ydiag[0], ydiag[1]) + yoff + d_ref[:, p * 128:(p + 1) * 128] * x2
            y_ref[:, p * 128:(p + 1) * 128] = y2

    gw = SSD_WIDTH // SSD_GROUPS
    for g in range(SSD_GROUPS):
        cols = slice(g * gw, (g + 1) * gw)
        yg = y_ref[:, cols] * _silu(z_ref[:, cols])
        o_ref[:, cols] = _rms(yg, ng_ref[:, cols])


def _ssd_mixer(proj, conv_w, conv_b, dt_bias_p, a_p, d_exp, norm_g, bsz, seq):
    t = proj.shape[0]
    L = SSD_CHUNK
    nc = seq // L
    row = lambda b, c: b * nc + c
    full = lambda shape: pl.BlockSpec(shape, lambda b, c: (0, 0))
    return pl.pallas_call(
        _ssd_body,
        grid=(bsz, nc),
        in_specs=[pl.BlockSpec((L, 1024), lambda b, c: (row(b, c), COL_Z)),
                  pl.BlockSpec((L, 1024), lambda b, c: (row(b, c), COL_XS)),
                  pl.BlockSpec((L, 1024), lambda b, c: (row(b, c), COL_BC)),
                  pl.BlockSpec((L, DT_PAD), lambda b, c: (row(b, c), COL_DT)),
                  full((SSD_CONV, 2048)), full((1, 2048)), full((1, DT_PAD)), full((1, DT_PAD)),
                  full((1, 1024)), full((1, 1024))],
        out_specs=pl.BlockSpec((L, 1024), lambda b, c: (row(b, c), 0)),
        out_shape=jax.ShapeDtypeStruct((t, SSD_WIDTH), F32),
        scratch_shapes=[pltpu.VMEM((L + HALO, 2048), F32),
                        pltpu.VMEM((L, 2048), F32),
                        pltpu.VMEM((SSD_PAIRS, SSD_STATE, LANES), F32),
                        pltpu.VMEM((L, 1024), F32)],
        compiler_params=pltpu.CompilerParams(dimension_semantics=("parallel", "arbitrary"),
                                             vmem_limit_bytes=VMEM_LIMIT),
        name="ssd_mixer",
    )(proj, proj, proj, proj, conv_w, conv_b, dt_bias_p, a_p, d_exp, norm_g)


def _s5_cols_to_hg(w, axis):
    shape = w.shape
    split = shape[:axis] + (S5_GROUPS, S5_GROUP_CH) + shape[axis + 1:]
    return jnp.swapaxes(w.reshape(split), axis, axis + 1).reshape(shape)


def _s5_prepare(a_re, a_im, log_dt, b_re, b_im, c_re, c_im, d_s5, glu_w, glu_b):
    G, P, H, C = S5_GROUPS, S5_STATE, S5_GROUP_CH, S5_CHUNK
    lr, li = a_re.astype(F32), a_im.astype(F32)
    dt = jnp.exp(log_dt.astype(F32))[:, None]
    mag = jnp.exp(lr * dt)
    lb_re, lb_im = mag * jnp.cos(li * dt), mag * jnp.sin(li * dt)
    den = lr * lr + li * li
    coef_re = ((lb_re - 1.0) * lr + lb_im * li) / den
    coef_im = (lb_im * lr - (lb_re - 1.0) * li) / den
    bb_re = coef_re[..., None] * b_re - coef_im[..., None] * b_im
    bb_im = coef_re[..., None] * b_im + coef_im[..., None] * b_re
    tau = jnp.arange(C + 1, dtype=F32)[:, None, None]
    pm = jnp.exp(lr * dt * tau)
    pw_re, pw_im = pm * jnp.cos(li * dt * tau), pm * jnp.sin(li * dt * tau)
    m_re = pw_re[..., None] * bb_re - pw_im[..., None] * bb_im
    m_im = pw_re[..., None] * bb_im + pw_im[..., None] * bb_re
    e_re = c_re * pw_re[:, :, None, :] - c_im * pw_im[:, :, None, :]
    e_im = c_re * pw_im[:, :, None, :] + c_im * pw_re[:, :, None, :]
    kern = (jnp.einsum('ghp,tgpk->tghk', c_re, m_re[:C], precision=HIGHEST)
            - jnp.einsum('ghp,tgpk->tghk', c_im, m_im[:C], precision=HIGHEST))
    j_in = jnp.arange(C)[:, None]
    j_out = jnp.arange(C)[None, :]
    lag = j_out - j_in
    toep = jnp.where((lag >= 0)[:, :, None, None, None], kern[jnp.clip(lag, 0, C - 1)], 0.0)
    toep = toep.transpose(2, 0, 4, 1, 3).reshape(G, C * H, C * H)
    rev = jnp.arange(C - 1, -1, -1)
    wst_re = m_re[rev].transpose(1, 0, 3, 2).reshape(G, C * H, P)
    wst_im = m_im[rev].transpose(1, 0, 3, 2).reshape(G, C * H, P)
    wout_re = e_re[1:].transpose(1, 3, 0, 2).reshape(G, P, C * H)
    wout_im = (-e_im[1:]).transpose(1, 3, 0, 2).reshape(G, P, C * H)
    padp = lambda v, axis: jnp.pad(v, [(0, LANES - P) if a == axis else (0, 0) for a in range(v.ndim)])
    lam_re, lam_im = padp(pw_re[C][:, None, :], 2), padp(pw_im[C][:, None, :], 2)
    dvec = jnp.tile(d_s5.astype(F32), (1, C))[:, None, :]
    eye = jnp.eye(C, dtype=F32)
    wglu = (eye[None, :, None, :, None] * glu_w.astype(F32)[:, None, :, None, :]).reshape(G, C * H, C * H)
    bglu = jnp.tile(glu_b.astype(F32), (1, C))[:, None, :]
    return (toep.astype(BF16), padp(wst_re, 2).astype(BF16), padp(wst_im, 2).astype(BF16), lam_re, lam_im,
            padp(wout_re, 1).astype(BF16), padp(wout_im, 1).astype(BF16), dvec, wglu.astype(BF16), bglu)


def _s5_body(nb, x_ref, toep_ref, wsr_ref, wsi_ref, lr_ref, li_ref, wor_ref, woi_ref, d_ref,
             wg_ref, bg_ref, o_ref, sr_ref, si_ref, pr_ref, pi_ref):
    x = x_ref[0]
    xb = x.astype(BF16)
    sr_ref[...] = jnp.dot(xb, wsr_ref[0], preferred_element_type=F32)
    si_ref[...] = jnp.dot(xb, wsi_ref[0], preferred_element_type=F32)
    lam_r = jnp.broadcast_to(lr_ref[0], (nb, LANES))
    lam_i = jnp.broadcast_to(li_ref[0], (nb, LANES))
    n_chunks = x.shape[0] // nb

    def step(c, carry):
        xr, xi = carry
        rows = pl.ds(c, nb, stride=n_chunks)
        pr_ref[rows, :] = xr
        pi_ref[rows, :] = xi
        nr = lam_r * xr - lam_i * xi + sr_ref[rows, :]
        ni = lam_r * xi + lam_i * xr + si_ref[rows, :]
        return nr, ni

    zero = jnp.zeros((nb, LANES), F32)
    lax.fori_loop(0, n_chunks, step, (zero, zero))
    y = (jnp.dot(xb, toep_ref[0], preferred_element_type=F32)
         + jnp.dot(pr_ref[...].astype(BF16), wor_ref[0], preferred_element_type=F32)
         + jnp.dot(pi_ref[...].astype(BF16), woi_ref[0], preferred_element_type=F32)
         + d_ref[0] * x)
    v = _gelu_exact(y)
    gate = jax.nn.sigmoid(jnp.dot(v.astype(BF16), wg_ref[0], preferred_element_type=F32) + bg_ref[0])
    o_ref[0] = v * gate


def _s5_mixer(xg, prep, nb):
    G, R, W = xg.shape
    toep, wsr, wsi, lam_re, lam_im, wor, woi, dvec, wglu, bglu = prep
    P = LANES
    per_g = lambda shape: pl.BlockSpec((1,) + shape, lambda g: (g, 0, 0))
    return pl.pallas_call(
        functools.partial(_s5_body, nb),
        grid=(G,),
        in_specs=[per_g((R, W)), per_g((W, W)), per_g((W, P)), per_g((W, P)), per_g((1, P)), per_g((1, P)),
                  per_g((P, W)), per_g((P, W)), per_g((1, W)), per_g((W, W)), per_g((1, W))],
        out_specs=per_g((R, W)),
        out_shape=jax.ShapeDtypeStruct((G, R, W), F32),
        scratch_shapes=[pltpu.VMEM((R, P), F32)] * 4,
        compiler_params=pltpu.CompilerParams(dimension_semantics=("parallel",),
                                             vmem_limit_bytes=VMEM_LIMIT),
        name="s5_mixer",
    )(xg, toep, wsr, wsi, lam_re, lam_im, wor, woi, dvec, wglu, bglu)


def _outproj_body(ys_ref, s5_ref, x_ref, g1_ref, sc2_ref, sh2_ref, ns5_ref, n2_ref, wo_ref, wq_ref,
                  x1_ref, h2_ref, q_ref):
    a = ys_ref[...].astype(BF16)
    b = _rms(s5_ref[...], ns5_ref[...]).astype(BF16)
    mix = (jnp.dot(a, wo_ref[0:SSD_WIDTH, :], preferred_element_type=F32)
           + jnp.dot(b, wo_ref[SSD_WIDTH:, :], preferred_element_type=F32))
    x1 = x_ref[...] + g1_ref[0] * mix
    x1_ref[...] = x1
    h2 = _rms(x1, n2_ref[...]) * (1.0 + sc2_ref[0]) + sh2_ref[0]
    h2_ref[...] = h2
    q = jnp.dot(h2.astype(BF16), wq_ref[...], preferred_element_type=F32).astype(BF16)
    for hc in range(2 * PEER_HEADS):
        q_ref[hc] = q[:, hc * PEER_HALF:(hc + 1) * PEER_HALF]


def _out_proj(yssd, s5o, x2d, gate1, scale2, shift2, ns5, n2, w_out, w_query, seq):
    t, d = x2d.shape
    nhc = w_query.shape[1] // PEER_HALF
    tm = 256
    per_b = seq // tm
    rowblk = lambda w: pl.BlockSpec((tm, w), lambda i: (i, 0))
    modblk = pl.BlockSpec((1, 1, d), lambda i: (i // per_b, 0, 0))
    full = lambda shape: pl.BlockSpec(shape, lambda i: (0, 0))
    return pl.pallas_call(
        _outproj_body,
        grid=(t // tm,),
        in_specs=[rowblk(d), rowblk(d), rowblk(d), modblk, modblk, modblk, full((1, d)), full((1, d)),
                  full(w_out.shape), full(w_query.shape)],
        out_specs=[rowblk(d), rowblk(d), pl.BlockSpec((nhc, tm, PEER_HALF), lambda i: (0, i, 0))],
        out_shape=[jax.ShapeDtypeStruct((t, d), F32), jax.ShapeDtypeStruct((t, d), F32),
                   jax.ShapeDtypeStruct((nhc, t, PEER_HALF), BF16)],
        compiler_params=pltpu.CompilerParams(dimension_semantics=("parallel",),
                                             vmem_limit_bytes=VMEM_LIMIT),
        name="out_proj_query",
    )(yssd, s5o, x2d, gate1, scale2, shift2, ns5, n2, w_out, w_query)


def _topk_rows(s, n_out, payload=None):
    n = s.shape[0]
    iota = lax.broadcasted_iota(I32, s.shape, 0)
    vals, picks = [], []
    for _ in range(n_out):
        m = jnp.max(s, axis=0, keepdims=True)
        am = jnp.min(jnp.where(s == m, iota, n), axis=0, keepdims=True)
        hit = iota == am
        vals.append(m)
        picks.append(am if payload is None else jnp.max(jnp.where(hit, payload, -1), axis=0, keepdims=True))
        s = jnp.where(hit, -jnp.inf, s)
        yield
    return jnp.concatenate(vals, axis=0), jnp.concatenate(picks, axis=0)


def _route_head_steps(q1, k1, q2, k2):
    K = PEER_TOPK
    nt = (((1,), (1,)), ((), ()))
    s1, i1 = yield from _topk_rows(lax.dot_general(k1, q1.astype(BF16), nt, preferred_element_type=F32), K)
    s2, i2 = yield from _topk_rows(lax.dot_general(k2, q2.astype(BF16), nt, preferred_element_type=F32), K)
    cs = [s1[0:1, :] + s2] + [s1[a:a + 1, :] + s2[0:8, :] for a in range(1, 8)] + [s1[8:K, :] + s2[0:1, :]]
    ci = ([i1[0:1, :] * PEER_NKEYS + i2] + [i1[a:a + 1, :] * PEER_NKEYS + i2[0:8, :] for a in range(1, 8)]
          + [i1[8:K, :] * PEER_NKEYS + i2[0:1, :]])
    top_s, ids = yield from _topk_rows(jnp.concatenate(cs, axis=0), K, payload=jnp.concatenate(ci, axis=0))
    e = jnp.exp(top_s - top_s[0:1, :])
    return e / jnp.sum(e, axis=0, keepdims=True), ids


def _run(gen, steps=None):
    try:
        while steps is None or steps > 0:
            next(gen)
            steps = None if steps is None else steps - 1
    except StopIteration as done:
        return done.value
    return None


SUBLANES = 8
PEER_TOK_TILE = 512
PEER_RING = 16
PEER_AHEAD = 8
PEER_JGROUPS = PEER_SLOTS // SUBLANES
PEER_ROUTE_STEPS = 3
SLAB_ROWS = 2 * SUBLANES


def _sublane_sums(p, sub):
    m4, m2, m1 = sub < 4, (sub & 2) == 0, (sub & 1) == 0
    z = jnp.where(m4, p[0:4], p[4:8]) + pltpu.roll(jnp.where(m4, p[4:8], p[0:4]), 4, 1)
    y = jnp.where(m2, z[0:2] + pltpu.roll(z[0:2], 6, 1), z[2:4] + pltpu.roll(z[2:4], 2, 1))
    return jnp.where(m1, y[0] + pltpu.roll(y[0], 7, 0), y[1] + pltpu.roll(y[1], 1, 0))


def _peer_body(q_ref, keys_ref, h_ref, x1_ref, g2_ref, nf_ref, uv_ref, o_ref,
               buf_ref, po_ref, wb_ref, gates_ref, eidv_ref, eidt_ref, eid_smem, sem, sem_eid):
    i = pl.program_id(0)
    n_tiles = pl.num_programs(0) - 1
    wslot = i % 2
    rslot = 1 - wslot
    n_blocks = PEER_TOK_TILE // PEER_RING
    assert PEER_TOK_TILE // LANES * PEER_HEADS == n_blocks
    sub = lax.broadcasted_iota(I32, (SUBLANES, LANES), 0)
    lane = lax.broadcasted_iota(I32, (SUBLANES, LANES), 1)
    lane_s = lax.broadcasted_iota(I32, (PEER_SLOTS, LANES), 1)

    def route_unit(b):
        rb, h = b // PEER_HEADS, b % PEER_HEADS
        toks = pl.ds(pl.multiple_of(rb * LANES, LANES), LANES)
        gates, ids = yield from _route_head_steps(q_ref[2 * h, toks, :], keys_ref[2 * h],
                                                  q_ref[2 * h + 1, toks, :], keys_ref[2 * h + 1])
        rows = pl.ds(pl.multiple_of(h * PEER_TOPK, PEER_TOPK), PEER_TOPK)
        gates_ref[wslot, rb, rows, :] = gates
        eidv_ref[rows, :] = ids

    def publish_ids(b):
        rb, h = b // PEER_HEADS, b % PEER_HEADS

        @pl.when(h == PEER_HEADS - 1)
        def _():
            toks = pl.ds(pl.multiple_of(rb * LANES, LANES), LANES)
            eidt_ref[...] = eidv_ref[...].astype(F32).T.astype(I32)
            cp = pltpu.make_async_copy(eidt_ref, eid_smem.at[wslot, toks, :], sem_eid)
            cp.start()
            cp.wait()

    def issue_rows(par, t, slot, k0, k1):
        for k in range(k0, k1):
            e = eid_smem[par, t, k]
            pltpu.make_async_copy(uv_ref.at[e], buf_ref.at[slot, k], sem.at[slot]).start(priority=k % 2)

    def wait_rows(slot):
        pltpu.make_async_copy(uv_ref.at[pl.ds(0, PEER_SLOTS)], buf_ref.at[slot], sem.at[slot]).wait()

    def token(t, slot, h_rows, routing):
        t_next = (t + PEER_AHEAD) % PEER_TOK_TILE
        par_next = jnp.where(t + PEER_AHEAD >= PEER_TOK_TILE, wslot, rslot)
        nslot = (slot + PEER_AHEAD) % PEER_RING
        per = PEER_SLOTS // 2 // PEER_JGROUPS
        wait_rows(slot)
        r = slot % SUBLANES
        h_t = jnp.concatenate([h_rows[r:r + 1, LANES * s:LANES * (s + 1)] for s in range(SUBLANES)], axis=0)
        dense = jnp.zeros((SUBLANES, LANES), F32)
        for j in range(PEER_JGROUPS):
            prods = buf_ref[slot, SUBLANES * j:SUBLANES * (j + 1), 0:SUBLANES, :] * h_t[None]
            col = jnp.sum(_sublane_sums(prods, sub), axis=1, keepdims=True)
            dense = jnp.where(lane == j, col, dense)
            issue_rows(par_next, t_next, nslot, j * per, (j + 1) * per)
            if j == PEER_JGROUPS // 2 - 1:
                _run(routing, steps=1)
        _run(routing, steps=1)
        act = _gelu_exact(dense)
        gblk = gates_ref[rslot, t // LANES]
        gcol = jnp.sum(jnp.where(lane_s == t % LANES, gblk, 0.0), axis=1, keepdims=True)
        for j in range(PEER_JGROUPS):
            rows = slice(SUBLANES * j, SUBLANES * (j + 1))
            wb_ref[rows, :] = jnp.broadcast_to(act[:, j:j + 1] * gcol[rows, :], (SUBLANES, LANES))

        accs = [jnp.zeros((SUBLANES, LANES), F32) for _ in range(4)]
        for k in range(PEER_SLOTS):
            wk = jnp.broadcast_to(wb_ref[k:k + 1, :], (SUBLANES, LANES))
            accs[k % 4] = accs[k % 4] + buf_ref[slot, k, SUBLANES:SLAB_ROWS, :] * wk
            if k % SUBLANES == SUBLANES - 1:
                j = k // SUBLANES
                issue_rows(par_next, t_next, nslot, PEER_SLOTS // 2 + j * per, PEER_SLOTS // 2 + (j + 1) * per)
            if k == PEER_SLOTS // 2 - 1:
                _run(routing, steps=1)
        out = (accs[0] + accs[1]) + (accs[2] + accs[3])
        return jnp.concatenate([out[s:s + 1, :] for s in range(SUBLANES)], axis=1)

    @pl.when(i == 0)
    def _():
        def first(b, _):
            _run(route_unit(b))
            publish_ids(b)
            return 0

        lax.fori_loop(0, n_blocks, first, 0)

    @pl.when(i == 1)
    def _():
        def prologue(t, _):
            issue_rows(rslot, t, t, 0, PEER_SLOTS)
            return 0

        lax.fori_loop(0, PEER_AHEAD, prologue, 0)

    @pl.when(i > 0)
    def _():
        def block(b, _):
            routing = route_unit(b)
            base = pl.multiple_of(b * PEER_RING, PEER_RING)
            rows = []
            for slot in range(PEER_RING):
                if slot % SUBLANES == 0:
                    h_rows = h_ref[pl.ds(pl.multiple_of(base + slot, SUBLANES), SUBLANES), :]
                rows.append(token(base + slot, slot, h_rows, routing))
                _run(routing, steps=PEER_ROUTE_STEPS - 3)
            _run(routing)
            po_ref[pl.ds(base, PEER_RING), :] = jnp.concatenate(rows, axis=0)
            publish_ids(b)
            return 0

        lax.fori_loop(0, n_blocks, block, 0)
        x2 = x1_ref[...] + g2_ref[0] * po_ref[...]
        o_ref[...] = _rms(x2, nf_ref[...])

    @pl.when(i == n_tiles)
    def _():
        for slot in range(PEER_AHEAD):
            wait_rows(slot)


def _peer_ffn(q, keys, h2, x1, gate2, nf, uv3, seq):
    t, d = h2.shape
    nhc, _, dh = q.shape
    tt = PEER_TOK_TILE
    n_tiles = t // tt
    per_b = seq // tt
    prev = lambda i: jnp.maximum(i - 1, 0)
    tokblk = pl.BlockSpec((tt, d), lambda i: (prev(i), 0))
    return pl.pallas_call(
        _peer_body,
        grid=(n_tiles + 1,),
        in_specs=[pl.BlockSpec((nhc, tt, dh), lambda i: (0, jnp.minimum(i, n_tiles - 1), 0)),
                  pl.BlockSpec(keys.shape, lambda i: (0, 0, 0)),
                  tokblk, tokblk,
                  pl.BlockSpec((1, 1, d), lambda i: (prev(i) // per_b, 0, 0)),
                  pl.BlockSpec((1, d), lambda i: (0, 0)),
                  pl.BlockSpec(memory_space=pl.ANY)],
        out_specs=tokblk,
        out_shape=jax.ShapeDtypeStruct((t, d), F32),
        scratch_shapes=[pltpu.VMEM((PEER_RING, PEER_SLOTS, SLAB_ROWS, LANES), F32),
                        pltpu.VMEM((tt, d), F32),
                        pltpu.VMEM((PEER_SLOTS, LANES), F32),
                        pltpu.VMEM((2, tt // LANES, PEER_SLOTS, LANES), F32),
                        pltpu.VMEM((PEER_SLOTS, LANES), I32),
                        pltpu.VMEM((LANES, PEER_SLOTS), I32),
                        pltpu.SMEM((2, tt, PEER_SLOTS), I32),
                        pltpu.SemaphoreType.DMA((PEER_RING,)),
                        pltpu.SemaphoreType.DMA(())],
        compiler_params=pltpu.CompilerParams(dimension_semantics=("arbitrary",),
                                             vmem_limit_bytes=VMEM_LIMIT),
        name="peer_ffn",
    )(q, keys, h2, x1, gate2, nf, uv3)


def kernel(x, c, w_ada, b_ada, norm1_g, w_in, conv_w, conv_b, dt_bias, a_log, d_ssd, norm_ssd_g, s5_a_re, s5_a_im, s5_log_dt, s5_b_re, s5_b_im, s5_c_re, s5_c_im, s5_d, glu_w, glu_b, norm_s5_g, w_out, norm2_g, w_query, sub_keys, expert_u, expert_v, norm_f_g):
    bsz, seq, d = x.shape
    t = bsz * seq
    depth = w_ada.shape[0]
    xt = x.reshape(t, d)
    for l in range(depth):
        mod = _adaln_mod(c, w_ada[l], b_ada[l])
        shift1, scale1, gate1, shift2, scale2, gate2 = [m.reshape(bsz, 1, d) for m in jnp.split(mod, 6, axis=-1)]

        wz, wxbc, wdt, wu = jnp.split(w_in[l], [SSD_WIDTH, SSD_WIDTH + 2048, SSD_WIDTH + 2048 + SSD_HEADS], axis=1)
        wu = _s5_cols_to_hg(wu, axis=1)
        w_cat = jnp.concatenate([wz, wxbc, jnp.pad(wdt, ((0, 0), (0, DT_PAD - SSD_HEADS))), wu], axis=1).astype(BF16)
        proj, u = _in_proj(xt, scale1, shift1, norm1_g[l].reshape(1, d), w_cat, seq)

        pad_h = lambda v: jnp.pad(v.astype(F32), (0, DT_PAD - SSD_HEADS)).reshape(1, DT_PAD)
        yssd = _ssd_mixer(proj, conv_w[l], conv_b[l].reshape(1, -1), pad_h(dt_bias[l]),
                          pad_h(-jnp.exp(a_log[l].astype(F32))),
                          jnp.repeat(d_ssd[l].astype(F32), SSD_HEAD_DIM).reshape(1, SSD_WIDTH),
                          norm_ssd_g[l].reshape(1, SSD_WIDTH), bsz, seq)

        blk5 = S5_CHUNK * S5_GROUP_CH
        xg = u.reshape(t * S5_GROUP_CH, S5_GROUPS).T.reshape(S5_GROUPS, t // S5_CHUNK, blk5)
        prep = _s5_prepare(s5_a_re[l], s5_a_im[l], s5_log_dt[l], s5_b_re[l], s5_b_im[l], s5_c_re[l],
                           s5_c_im[l], s5_d[l], glu_w[l], glu_b[l])
        og = _s5_mixer(xg, prep, bsz)
        s5o = og.reshape(S5_GROUPS, t * S5_GROUP_CH).T.reshape(t, S5_WIDTH)

        w_o = jnp.concatenate([w_out[l][:SSD_WIDTH], _s5_cols_to_hg(w_out[l][SSD_WIDTH:], axis=0)], axis=0)
        x1, h2, q = _out_proj(yssd, s5o, xt, gate1, scale2, shift2, _s5_cols_to_hg(norm_s5_g[l], axis=0).reshape(1, -1),
                              norm2_g[l].reshape(1, d), w_o.astype(BF16), w_query[l].astype(BF16), seq)

        keys = sub_keys[l].reshape(2 * PEER_HEADS, PEER_NKEYS, PEER_HALF).astype(BF16)
        n_exp = expert_u.shape[1]
        uv3 = jnp.concatenate([expert_u[l].reshape(n_exp, SUBLANES, LANES),
                               expert_v[l].reshape(n_exp, SUBLANES, LANES)], axis=1)
        last = l == depth - 1
        assert last, "the final RMSNorm is fused into the last layer's PEER kernel"
        xt = _peer_ffn(q, keys, h2, x1, gate2, norm_f_g.reshape(1, d), uv3, seq)
    return xt.reshape(bsz, seq, d)
```

```python
import functools

import jax
import jax.numpy as jnp
from jax import lax
from jax.experimental import pallas as pl
from jax.experimental.pallas import tpu as pltpu

F32 = jnp.float32
BF16 = jnp.bfloat16
I32 = jnp.int32

D_MODEL = 1024
SSD_WIDTH = 1024
SSD_HEAD_DIM = 64
SSD_HEADS = 16
SSD_GROUPS = 4
SSD_STATE = 128
SSD_CONV = 4
SSD_CHUNK = 128
S5_WIDTH = 1024
S5_GROUP_CH = 16
S5_GROUPS = 64
S5_STATE = 64
S5_CHUNK = 16
PEER_HEADS = 8
PEER_NKEYS = 128
PEER_TOPK = 16
PEER_HALF = 128
PEER_SLOTS = PEER_HEADS * PEER_TOPK
EPS = 1e-6
LANES = 128
DT_PAD = LANES
SSD_PROJ_WIDTH = SSD_WIDTH + 2 * 1024 + DT_PAD
COL_Z, COL_XS, COL_BC = 0, 1, 2
COL_DT = (3 * 1024) // DT_PAD
HIGHEST = lax.Precision.HIGHEST
VMEM_LIMIT = 56 * 1024 * 1024


def _silu(v):
    return v * jax.nn.sigmoid(v)


def _gelu_exact(v):
    return 0.5 * v * (1.0 + lax.erf(v * (2.0 ** -0.5)))


def _rms(v, g):
    return v * lax.rsqrt(jnp.mean(v * v, axis=-1, keepdims=True) + EPS) * g


def _mod_body(c_ref, w_ref, b_ref, o_ref):
    o_ref[...] = jnp.dot(_silu(c_ref[...]), w_ref[...], preferred_element_type=F32) + b_ref[...]


def _adaln_mod(c, w_ada, b_ada):
    bsz, d = c.shape
    n = w_ada.shape[1]
    tn = 1024
    return pl.pallas_call(
        _mod_body,
        grid=(n // tn,),
        in_specs=[pl.BlockSpec((bsz, d), lambda j: (0, 0)),
                  pl.BlockSpec((d, tn), lambda j: (0, j)),
                  pl.BlockSpec((1, tn), lambda j: (0, j))],
        out_specs=pl.BlockSpec((bsz, tn), lambda j: (0, j)),
        out_shape=jax.ShapeDtypeStruct((bsz, n), F32),
        name="adaln_mod",
    )(c, w_ada, b_ada.reshape(1, n))


def _inproj_body(x_ref, sc_ref, sh_ref, g_ref, w_ref, o_ref, u_ref):
    h = _rms(x_ref[...], g_ref[...]) * (1.0 + sc_ref[0]) + sh_ref[0]
    p = jnp.dot(h.astype(BF16), w_ref[...], preferred_element_type=F32)
    o_ref[...] = p[:, 0:SSD_PROJ_WIDTH]
    u_ref[...] = p[:, SSD_PROJ_WIDTH:]


def _in_proj(x2d, scale1, shift1, g1, w_cat, seq):
    t, d = x2d.shape
    n = SSD_PROJ_WIDTH
    tm = 256
    per_b = seq // tm
    return pl.pallas_call(
        _inproj_body,
        grid=(t // tm,),
        in_specs=[pl.BlockSpec((tm, d), lambda i: (i, 0)),
                  pl.BlockSpec((1, 1, d), lambda i: (i // per_b, 0, 0)),
                  pl.BlockSpec((1, 1, d), lambda i: (i // per_b, 0, 0)),
                  pl.BlockSpec((1, d), lambda i: (0, 0)),
                  pl.BlockSpec(w_cat.shape, lambda i: (0, 0))],
        out_specs=[pl.BlockSpec((tm, n), lambda i: (i, 0)), pl.BlockSpec((tm, S5_WIDTH), lambda i: (i, 0))],
        out_shape=[jax.ShapeDtypeStruct((t, n), F32), jax.ShapeDtypeStruct((t, S5_WIDTH), F32)],
        compiler_params=pltpu.CompilerParams(dimension_semantics=("parallel",),
                                             vmem_limit_bytes=VMEM_LIMIT),
        name="in_proj",
    )(x2d, scale1, shift1, g1, w_cat)


HALO = 8
SSD_PAIRS = SSD_HEADS // 2


def _ssd_body(z_ref, xs_ref, bc_ref, dt_ref, cw_ref, cb_ref, dtb_ref, a_ref, d_ref, ng_ref,
              o_ref, win_ref, act_ref, state_ref, y_ref):
    c = pl.program_id(1)
    L = SSD_CHUNK

    @pl.when(c == 0)
    def _():
        win_ref[0:HALO, :] = jnp.zeros((HALO, 2048), F32)
        state_ref[...] = jnp.zeros(state_ref.shape, F32)

    @pl.when(c > 0)
    def _():
        win_ref[0:HALO, :] = win_ref[L:L + HALO, :]

    win_ref[HALO:HALO + L, 0:1024] = xs_ref[...]
    win_ref[HALO:HALO + L, 1024:2048] = bc_ref[...]

    for sl in range(8):
        cols = slice(sl * 256, (sl + 1) * 256)
        acc = jnp.broadcast_to(cb_ref[:, cols], (L, 256))
        for k in range(SSD_CONV):
            r0 = HALO - (SSD_CONV - 1) + k
            acc = acc + win_ref[r0:r0 + L, cols] * cw_ref[k:k + 1, cols]
        act_ref[:, cols] = _silu(acc)

    dtv = dt_ref[...] + dtb_ref[...]
    dt = jnp.maximum(dtv, 0.0) + jnp.log1p(jnp.exp(-jnp.abs(dtv)))
    da = dt * a_ref[...]
    row = lax.broadcasted_iota(I32, (L, L), 0)
    col = lax.broadcasted_iota(I32, (L, L), 1)
    causal = row >= col
    tril = jnp.where(causal, 1.0, 0.0).astype(F32)
    cs = jnp.dot(tril, da, precision=HIGHEST, preferred_element_type=F32)
    cs_t = cs.T
    dec_t = jnp.exp(cs_t[:, L - 1:L] - cs_t)
    ecs = jnp.exp(cs)
    lane = lax.broadcasted_iota(I32, (L, LANES), 1)
    first = lane < SSD_HEAD_DIM

    for g in range(SSD_GROUPS):
        bg = act_ref[:, 1024 + g * 128:1024 + (g + 1) * 128]
        cg = act_ref[:, 1536 + g * 128:1536 + (g + 1) * 128]
        bg_t = bg.T
        cg_b = cg.astype(BF16)
        cb = jnp.dot(cg_b, bg_t.astype(BF16), preferred_element_type=F32)
        for r in range(2):
            p = g * 2 + r
            h0, h1 = 2 * p, 2 * p + 1
            x2 = act_ref[:, p * 128:(p + 1) * 128]
            dt2 = jnp.where(first, dt[:, h0:h0 + 1], dt[:, h1:h1 + 1])
            xdt = (x2 * dt2).astype(BF16)
            ydiag, snew = [], []
            for h in (h0, h1):
                diff = cs[:, h:h + 1] - cs_t[h:h + 1, :]
                lmat = jnp.where(causal, jnp.exp(jnp.where(causal, diff, 0.0)), 0.0)
                ydiag.append(jnp.dot((cb * lmat).astype(BF16), xdt, preferred_element_type=F32))
                bw = (bg_t * dec_t[h:h + 1, :]).astype(BF16)
                snew.append(jnp.dot(bw, xdt, preferred_element_type=F32))
            prev = state_ref[p]
            yoff = jnp.dot(cg_b, prev.astype(BF16), preferred_element_type=F32)
            yoff = yoff * jnp.where(first, ecs[:, h0:h0 + 1], ecs[:, h1:h1 + 1])
            cdec = jnp.where(first[0:1, :], ecs[L - 1:L, h0:h0 + 1], ecs[L - 1:L, h1:h1 + 1])
            state_ref[p] = prev * cdec + jnp.where(first, snew[0], snew[1])
            y2 = jnp.where(first, ydiag[0], ydiag[1]) + yoff + d_ref[:, p * 128:(p + 1) * 128] * x2
            y_ref[:, p * 128:(p + 1) * 128] = y2

    gw = SSD_WIDTH // SSD_GROUPS
    for g in range(SSD_GROUPS):
        cols = slice(g * gw, (g + 1) * gw)
        yg = y_ref[:, cols] * _silu(z_ref[:, cols])
        o_ref[:, cols] = _rms(yg, ng_ref[:, cols])


def _ssd_mixer(proj, conv_w, conv_b, dt_bias_p, a_p, d_exp, norm_g, bsz, seq):
    t = proj.shape[0]
    L = SSD_CHUNK
    nc = seq // L
    row = lambda b, c: b * nc + c
    full = lambda shape: pl.BlockSpec(shape, lambda b, c: (0, 0))
    return pl.pallas_call(
        _ssd_body,
        grid=(bsz, nc),
        in_specs=[pl.BlockSpec((L, 1024), lambda b, c: (row(b, c), COL_Z)),
                  pl.BlockSpec((L, 1024), lambda b, c: (row(b, c), COL_XS)),
                  pl.BlockSpec((L, 1024), lambda b, c: (row(b, c), COL_BC)),
                  pl.BlockSpec((L, DT_PAD), lambda b, c: (row(b, c), COL_DT)),
                  full((SSD_CONV, 2048)), full((1, 2048)), full((1, DT_PAD)), full((1, DT_PAD)),
                  full((1, 1024)), full((1, 1024))],
        out_specs=pl.BlockSpec((L, 1024), lambda b, c: (row(b, c), 0)),
        out_shape=jax.ShapeDtypeStruct((t, SSD_WIDTH), F32),
        scratch_shapes=[pltpu.VMEM((L + HALO, 2048), F32),
                        pltpu.VMEM((L, 2048), F32),
                        pltpu.VMEM((SSD_PAIRS, SSD_STATE, LANES), F32),
                        pltpu.VMEM((L, 1024), F32)],
        compiler_params=pltpu.CompilerParams(dimension_semantics=("parallel", "arbitrary"),
                                             vmem_limit_bytes=VMEM_LIMIT),
        name="ssd_mixer",
    )(proj, proj, proj, proj, conv_w, conv_b, dt_bias_p, a_p, d_exp, norm_g)


def _s5_cols_to_hg(w, axis):
    shape = w.shape
    split = shape[:axis] + (S5_GROUPS, S5_GROUP_CH) + shape[axis + 1:]
    return jnp.swapaxes(w.reshape(split), axis, axis + 1).reshape(shape)


def _s5_prepare(a_re, a_im, log_dt, b_re, b_im, c_re, c_im, d_s5, glu_w, glu_b):
    G, P, H, C = S5_GROUPS, S5_STATE, S5_GROUP_CH, S5_CHUNK
    lr, li = a_re.astype(F32), a_im.astype(F32)
    dt = jnp.exp(log_dt.astype(F32))[:, None]
    mag = jnp.exp(lr * dt)
    lb_re, lb_im = mag * jnp.cos(li * dt), mag * jnp.sin(li * dt)
    den = lr * lr + li * li
    coef_re = ((lb_re - 1.0) * lr + lb_im * li) / den
    coef_im = (lb_im * lr - (lb_re - 1.0) * li) / den
    bb_re = coef_re[..., None] * b_re - coef_im[..., None] * b_im
    bb_im = coef_re[..., None] * b_im + coef_im[..., None] * b_re
    tau = jnp.arange(C + 1, dtype=F32)[:, None, None]
    pm = jnp.exp(lr * dt * tau)
    pw_re, pw_im = pm * jnp.cos(li * dt * tau), pm * jnp.sin(li * dt * tau)
    m_re = pw_re[..., None] * bb_re - pw_im[..., None] * bb_im
    m_im = pw_re[..., None] * bb_im + pw_im[..., None] * bb_re
    e_re = c_re * pw_re[:, :, None, :] - c_im * pw_im[:, :, None, :]
    e_im = c_re * pw_im[:, :, None, :] + c_im * pw_re[:, :, None, :]
    kern = (jnp.einsum('ghp,tgpk->tghk', c_re, m_re[:C], precision=HIGHEST)
            - jnp.einsum('ghp,tgpk->tghk', c_im, m_im[:C], precision=HIGHEST))
    j_in = jnp.arange(C)[:, None]
    j_out = jnp.arange(C)[None, :]
    lag = j_out - j_in
    toep = jnp.where((lag >= 0)[:, :, None, None, None], kern[jnp.clip(lag, 0, C - 1)], 0.0)
    toep = toep.transpose(2, 0, 4, 1, 3).reshape(G, C * H, C * H)
    rev = jnp.arange(C - 1, -1, -1)
    wst_re = m_re[rev].transpose(1, 0, 3, 2).reshape(G, C * H, P)
    wst_im = m_im[rev].transpose(1, 0, 3, 2).reshape(G, C * H, P)
    wout_re = e_re[1:].transpose(1, 3, 0, 2).reshape(G, P, C * H)
    wout_im = (-e_im[1:]).transpose(1, 3, 0, 2).reshape(G, P, C * H)
    padp = lambda v, axis: jnp.pad(v, [(0, LANES - P) if a == axis else (0, 0) for a in range(v.ndim)])
    lam_re, lam_im = padp(pw_re[C][:, None, :], 2), padp(pw_im[C][:, None, :], 2)
    dvec = jnp.tile(d_s5.astype(F32), (1, C))[:, None, :]
    eye = jnp.eye(C, dtype=F32)
    wglu = (eye[None, :, None, :, None] * glu_w.astype(F32)[:, None, :, None, :]).reshape(G, C * H, C * H)
    bglu = jnp.tile(glu_b.astype(F32), (1, C))[:, None, :]
    return (toep.astype(BF16), padp(wst_re, 2).astype(BF16), padp(wst_im, 2).astype(BF16), lam_re, lam_im,
            padp(wout_re, 1).astype(BF16), padp(wout_im, 1).astype(BF16), dvec, wglu.astype(BF16), bglu)


def _s5_body(nb, x_ref, toep_ref, wsr_ref, wsi_ref, lr_ref, li_ref, wor_ref, woi_ref, d_ref,
             wg_ref, bg_ref, o_ref, sr_ref, si_ref, pr_ref, pi_ref):
    x = x_ref[0]
    xb = x.astype(BF16)
    sr_ref[...] = jnp.dot(xb, wsr_ref[0], preferred_element_type=F32)
    si_ref[...] = jnp.dot(xb, wsi_ref[0], preferred_element_type=F32)
    lam_r = jnp.broadcast_to(lr_ref[0], (nb, LANES))
    lam_i = jnp.broadcast_to(li_ref[0], (nb, LANES))
    n_chunks = x.shape[0] // nb

    def step(c, carry):
        xr, xi = carry
        rows = pl.ds(c, nb, stride=n_chunks)
        pr_ref[rows, :] = xr
        pi_ref[rows, :] = xi
        nr = lam_r * xr - lam_i * xi + sr_ref[rows, :]
        ni = lam_r * xi + lam_i * xr + si_ref[rows, :]
        return nr, ni

    zero = jnp.zeros((nb, LANES), F32)
    lax.fori_loop(0, n_chunks, step, (zero, zero))
    y = (jnp.dot(xb, toep_ref[0], preferred_element_type=F32)
         + jnp.dot(pr_ref[...].astype(BF16), wor_ref[0], preferred_element_type=F32)
         + jnp.dot(pi_ref[...].astype(BF16), woi_ref[0], preferred_element_type=F32)
         + d_ref[0] * x)
    v = _gelu_exact(y)
    gate = jax.nn.sigmoid(jnp.dot(v.astype(BF16), wg_ref[0], preferred_element_type=F32) + bg_ref[0])
    o_ref[0] = v * gate


def _s5_mixer(xg, prep, nb):
    G, R, W = xg.shape
    toep, wsr, wsi, lam_re, lam_im, wor, woi, dvec, wglu, bglu = prep
    P = LANES
    per_g = lambda shape: pl.BlockSpec((1,) + shape, lambda g: (g, 0, 0))
    return pl.pallas_call(
        functools.partial(_s5_body, nb),
        grid=(G,),
        in_specs=[per_g((R, W)), per_g((W, W)), per_g((W, P)), per_g((W, P)), per_g((1, P)), per_g((1, P)),
                  per_g((P, W)), per_g((P, W)), per_g((1, W)), per_g((W, W)), per_g((1, W))],
        out_specs=per_g((R, W)),
        out_shape=jax.ShapeDtypeStruct((G, R, W), F32),
        scratch_shapes=[pltpu.VMEM((R, P), F32)] * 4,
        compiler_params=pltpu.CompilerParams(dimension_semantics=("parallel",),
                                             vmem_limit_bytes=VMEM_LIMIT),
        name="s5_mixer",
    )(xg, toep, wsr, wsi, lam_re, lam_im, wor, woi, dvec, wglu, bglu)


def _outproj_body(ys_ref, s5_ref, x_ref, g1_ref, sc2_ref, sh2_ref, ns5_ref, n2_ref, wo_ref, wq_ref,
                  x1_ref, h2_ref, q_ref):
    a = ys_ref[...].astype(BF16)
    b = _rms(s5_ref[...], ns5_ref[...]).astype(BF16)
    mix = (jnp.dot(a, wo_ref[0:SSD_WIDTH, :], preferred_element_type=F32)
           + jnp.dot(b, wo_ref[SSD_WIDTH:, :], preferred_element_type=F32))
    x1 = x_ref[...] + g1_ref[0] * mix
    x1_ref[...] = x1
    h2 = _rms(x1, n2_ref[...]) * (1.0 + sc2_ref[0]) + sh2_ref[0]
    h2_ref[...] = h2
    q = jnp.dot(h2.astype(BF16), wq_ref[...], preferred_element_type=F32).astype(BF16)
    for hc in range(2 * PEER_HEADS):
        q_ref[hc] = q[:, hc * PEER_HALF:(hc + 1) * PEER_HALF]


def _out_proj(yssd, s5o, x2d, gate1, scale2, shift2, ns5, n2, w_out, w_query, seq):
    t, d = x2d.shape
    nhc = w_query.shape[1] // PEER_HALF
    tm = 256
    per_b = seq // tm
    rowblk = lambda w: pl.BlockSpec((tm, w), lambda i: (i, 0))
    modblk = pl.BlockSpec((1, 1, d), lambda i: (i // per_b, 0, 0))
    full = lambda shape: pl.BlockSpec(shape, lambda i: (0, 0))
    return pl.pallas_call(
        _outproj_body,
        grid=(t // tm,),
        in_specs=[rowblk(d), rowblk(d), rowblk(d), modblk, modblk, modblk, full((1, d)), full((1, d)),
                  full(w_out.shape), full(w_query.shape)],
        out_specs=[rowblk(d), rowblk(d), pl.BlockSpec((nhc, tm, PEER_HALF), lambda i: (0, i, 0))],
        out_shape=[jax.ShapeDtypeStruct((t, d), F32), jax.ShapeDtypeStruct((t, d), F32),
                   jax.ShapeDtypeStruct((nhc, t, PEER_HALF), BF16)],
        compiler_params=pltpu.CompilerParams(dimension_semantics=("parallel",),
                                             vmem_limit_bytes=VMEM_LIMIT),
        name="out_proj_query",
    )(yssd, s5o, x2d, gate1, scale2, shift2, ns5, n2, w_out, w_query)


def _topk_rows(s, n_out, payload=None):
    n = s.shape[0]
    iota = lax.broadcasted_iota(I32, s.shape, 0)
    vals, picks = [], []
    for _ in range(n_out):
        m = jnp.max(s, axis=0, keepdims=True)
        am = jnp.min(jnp.where(s == m, iota, n), axis=0, keepdims=True)
        hit = iota == am
        vals.append(m)
        picks.append(am if payload is None else jnp.max(jnp.where(hit, payload, -1), axis=0, keepdims=True))
        s = jnp.where(hit, -jnp.inf, s)
        yield
    return jnp.concatenate(vals, axis=0), jnp.concatenate(picks, axis=0)


def _route_head_steps(q1, k1, q2, k2):
    K = PEER_TOPK
    nt = (((1,), (1,)), ((), ()))
    s1, i1 = yield from _topk_rows(lax.dot_general(k1, q1.astype(BF16), nt, preferred_element_type=F32), K)
    s2, i2 = yield from _topk_rows(lax.dot_general(k2, q2.astype(BF16), nt, preferred_element_type=F32), K)
    cs = [s1[0:1, :] + s2] + [s1[a:a + 1, :] + s2[0:8, :] for a in range(1, 8)] + [s1[8:K, :] + s2[0:1, :]]
    ci = ([i1[0:1, :] * PEER_NKEYS + i2] + [i1[a:a + 1, :] * PEER_NKEYS + i2[0:8, :] for a in range(1, 8)]
          + [i1[8:K, :] * PEER_NKEYS + i2[0:1, :]])
    top_s, ids = yield from _topk_rows(jnp.concatenate(cs, axis=0), K, payload=jnp.concatenate(ci, axis=0))
    e = jnp.exp(top_s - top_s[0:1, :])
    return e / jnp.sum(e, axis=0, keepdims=True), ids


def _run(gen, steps=None):
    try:
        while steps is None or steps > 0:
            next(gen)
            steps = None if steps is None else steps - 1
    except StopIteration as done:
        return done.value
    return None


SUBLANES = 8
PEER_TOK_TILE = 512
PEER_RING = 16
PEER_AHEAD = 8
PEER_JGROUPS = PEER_SLOTS // SUBLANES
PEER_ROUTE_STEPS = 3
SLAB_ROWS = 2 * SUBLANES


def _sublane_sums(p, sub):
    m4, m2, m1 = sub < 4, (sub & 2) == 0, (sub & 1) == 0
    z = jnp.where(m4, p[0:4], p[4:8]) + pltpu.roll(jnp.where(m4, p[4:8], p[0:4]), 4, 1)
    y = jnp.where(m2, z[0:2] + pltpu.roll(z[0:2], 6, 1), z[2:4] + pltpu.roll(z[2:4], 2, 1))
    return jnp.where(m1, y[0] + pltpu.roll(y[0], 7, 0), y[1] + pltpu.roll(y[1], 1, 0))


def _peer_body(q_ref, keys_ref, h_ref, x1_ref, g2_ref, nf_ref, uv_ref, o_ref,
               buf_ref, po_ref, wb_ref, gates_ref, eidv_ref, eidt_ref, eid_smem, sem, sem_eid):
    i = pl.program_id(0)
    n_tiles = pl.num_programs(0) - 1
    wslot = i % 2
    rslot = 1 - wslot
    n_blocks = PEER_TOK_TILE // PEER_RING
    assert PEER_TOK_TILE // LANES * PEER_HEADS == n_blocks
    sub = lax.broadcasted_iota(I32, (SUBLANES, LANES), 0)
    lane = lax.broadcasted_iota(I32, (SUBLANES, LANES), 1)
    lane_s = lax.broadcasted_iota(I32, (PEER_SLOTS, LANES), 1)

    def route_unit(b):
        rb, h = b // PEER_HEADS, b % PEER_HEADS
        toks = pl.ds(pl.multiple_of(rb * LANES, LANES), LANES)
        gates, ids = yield from _route_head_steps(q_ref[2 * h, toks, :], keys_ref[2 * h],
                                                  q_ref[2 * h + 1, toks, :], keys_ref[2 * h + 1])
        rows = pl.ds(pl.multiple_of(h * PEER_TOPK, PEER_TOPK), PEER_TOPK)
        gates_ref[wslot, rb, rows, :] = gates
        eidv_ref[rows, :] = ids

    def publish_ids(b):
        rb, h = b // PEER_HEADS, b % PEER_HEADS

        @pl.when(h == PEER_HEADS - 1)
        def _():
            toks = pl.ds(pl.multiple_of(rb * LANES, LANES), LANES)
            eidt_ref[...] = eidv_ref[...].astype(F32).T.astype(I32)
            cp = pltpu.make_async_copy(eidt_ref, eid_smem.at[wslot, toks, :], sem_eid)
            cp.start()
            cp.wait()

    def issue_rows(par, t, slot, k0, k1):
        for k in range(k0, k1):
            e = eid_smem[par, t, k]
            pltpu.make_async_copy(uv_ref.at[e], buf_ref.at[slot, k], sem.at[slot]).start(priority=k % 2)

    def wait_rows(slot):
        pltpu.make_async_copy(uv_ref.at[pl.ds(0, PEER_SLOTS)], buf_ref.at[slot], sem.at[slot]).wait()

    def token(t, slot, h_rows, routing):
        t_next = (t + PEER_AHEAD) % PEER_TOK_TILE
        par_next = jnp.where(t + PEER_AHEAD >= PEER_TOK_TILE, wslot, rslot)
        nslot = (slot + PEER_AHEAD) % PEER_RING
        per = PEER_SLOTS // 2 // PEER_JGROUPS
        wait_rows(slot)
        r = slot % SUBLANES
        h_t = jnp.concatenate([h_rows[r:r + 1, LANES * s:LANES * (s + 1)] for s in range(SUBLANES)], axis=0)
        dense = jnp.zeros((SUBLANES, LANES), F32)
        for j in range(PEER_JGROUPS):
            prods = buf_ref[slot, SUBLANES * j:SUBLANES * (j + 1), 0:SUBLANES, :] * h_t[None]
            col = jnp.sum(_sublane_sums(prods, sub), axis=1, keepdims=True)
            dense = jnp.where(lane == j, col, dense)
            issue_rows(par_next, t_next, nslot, j * per, (j + 1) * per)
            if j == PEER_JGROUPS // 2 - 1:
                _run(routing, steps=1)
        _run(routing, steps=1)
        act = _gelu_exact(dense)
        gblk = gates_ref[rslot, t // LANES]
        gcol = jnp.sum(jnp.where(lane_s == t % LANES, gblk, 0.0), axis=1, keepdims=True)
        for j in range(PEER_JGROUPS):
            rows = slice(SUBLANES * j, SUBLANES * (j + 1))
            wb_ref[rows, :] = jnp.broadcast_to(act[:, j:j + 1] * gcol[rows, :], (SUBLANES, LANES))

        accs = [jnp.zeros((SUBLANES, LANES), F32) for _ in range(4)]
        for k in range(PEER_SLOTS):
            wk = jnp.broadcast_to(wb_ref[k:k + 1, :], (SUBLANES, LANES))
            accs[k % 4] = accs[k % 4] + buf_ref[slot, k, SUBLANES:SLAB_ROWS, :] * wk
            if k % SUBLANES == SUBLANES - 1:
                j = k // SUBLANES
                issue_rows(par_next, t_next, nslot, PEER_SLOTS // 2 + j * per, PEER_SLOTS // 2 + (j + 1) * per)
            if k == PEER_SLOTS // 2 - 1:
                _run(routing, steps=1)
        out = (accs[0] + accs[1]) + (accs[2] + accs[3])
        return jnp.concatenate([out[s:s + 1, :] for s in range(SUBLANES)], axis=1)

    @pl.when(i == 0)
    def _():
        def first(b, _):
            _run(route_unit(b))
            publish_ids(b)
            return 0

        lax.fori_loop(0, n_blocks, first, 0)

    @pl.when(i == 1)
    def _():
        def prologue(t, _):
            issue_rows(rslot, t, t, 0, PEER_SLOTS)
            return 0

        lax.fori_loop(0, PEER_AHEAD, prologue, 0)

    @pl.when(i > 0)
    def _():
        def block(b, _):
            routing = route_unit(b)
            base = pl.multiple_of(b * PEER_RING, PEER_RING)
            rows = []
            for slot in range(PEER_RING):
                if slot % SUBLANES == 0:
                    h_rows = h_ref[pl.ds(pl.multiple_of(base + slot, SUBLANES), SUBLANES), :]
                rows.append(token(base + slot, slot, h_rows, routing))
                _run(routing, steps=PEER_ROUTE_STEPS - 3)
            _run(routing)
            po_ref[pl.ds(base, PEER_RING), :] = jnp.concatenate(rows, axis=0)
            publish_ids(b)
            return 0

        lax.fori_loop(0, n_blocks, block, 0)
        x2 = x1_ref[...] + g2_ref[0] * po_ref[...]
        o_ref[...] = _rms(x2, nf_ref[...])

    @pl.when(i == n_tiles)
    def _():
        for slot in range(PEER_AHEAD):
            wait_rows(slot)


def _peer_ffn(q, keys, h2, x1, gate2, nf, uv3, seq):
    t, d = h2.shape
    nhc, _, dh = q.shape
    tt = PEER_TOK_TILE
    n_tiles = t // tt
    per_b = seq // tt
    prev = lambda i: jnp.maximum(i - 1, 0)
    tokblk = pl.BlockSpec((tt, d), lambda i: (prev(i), 0))
    return pl.pallas_call(
        _peer_body,
        grid=(n_tiles + 1,),
        in_specs=[pl.BlockSpec((nhc, tt, dh), lambda i: (0, jnp.minimum(i, n_tiles - 1), 0)),
                  pl.BlockSpec(keys.shape, lambda i: (0, 0, 0)),
                  tokblk, tokblk,
                  pl.BlockSpec((1, 1, d), lambda i: (prev(i) // per_b, 0, 0)),
                  pl.BlockSpec((1, d), lambda i: (0, 0)),
                  pl.BlockSpec(memory_space=pl.ANY)],
        out_specs=tokblk,
        out_shape=jax.ShapeDtypeStruct((t, d), F32),
        scratch_shapes=[pltpu.VMEM((PEER_RING, PEER_SLOTS, SLAB_ROWS, LANES), F32),
                        pltpu.VMEM((tt, d), F32),
                        pltpu.VMEM((PEER_SLOTS, LANES), F32),
                        pltpu.VMEM((2, tt // LANES, PEER_SLOTS, LANES), F32),
                        pltpu.VMEM((PEER_SLOTS, LANES), I32),
                        pltpu.VMEM((LANES, PEER_SLOTS), I32),
                        pltpu.SMEM((2, tt, PEER_SLOTS), I32),
                        pltpu.SemaphoreType.DMA((PEER_RING,)),
                        pltpu.SemaphoreType.DMA(())],
        compiler_params=pltpu.CompilerParams(dimension_semantics=("arbitrary",),
                                             vmem_limit_bytes=VMEM_LIMIT),
        name="peer_ffn",
    )(q, keys, h2, x1, gate2, nf, uv3)


def kernel(x, c, w_ada, b_ada, norm1_g, w_in, conv_w, conv_b, dt_bias, a_log, d_ssd, norm_ssd_g, s5_a_re, s5_a_im, s5_log_dt, s5_b_re, s5_b_im, s5_c_re, s5_c_im, s5_d, glu_w, glu_b, norm_s5_g, w_out, norm2_g, w_query, sub_keys, expert_u, expert_v, norm_f_g):
    bsz, seq, d = x.shape
    t = bsz * seq
    depth = w_ada.shape[0]
    xt = x.reshape(t, d)
    for l in range(depth):
        mod = _adaln_mod(c, w_ada[l], b_ada[l])
        shift1, scale1, gate1, shift2, scale2, gate2 = [m.reshape(bsz, 1, d) for m in jnp.split(mod, 6, axis=-1)]

        wz, wxbc, wdt, wu = jnp.split(w_in[l], [SSD_WIDTH, SSD_WIDTH + 2048, SSD_WIDTH + 2048 + SSD_HEADS], axis=1)
        wu = _s5_cols_to_hg(wu, axis=1)
        w_cat = jnp.concatenate([wz, wxbc, jnp.pad(wdt, ((0, 0), (0, DT_PAD - SSD_HEADS))), wu], axis=1).astype(BF16)
        proj, u = _in_proj(xt, scale1, shift1, norm1_g[l].reshape(1, d), w_cat, seq)

        pad_h = lambda v: jnp.pad(v.astype(F32), (0, DT_PAD - SSD_HEADS)).reshape(1, DT_PAD)
        yssd = _ssd_mixer(proj, conv_w[l], conv_b[l].reshape(1, -1), pad_h(dt_bias[l]),
                          pad_h(-jnp.exp(a_log[l].astype(F32))),
                          jnp.repeat(d_ssd[l].astype(F32), SSD_HEAD_DIM).reshape(1, SSD_WIDTH),
                          norm_ssd_g[l].reshape(1, SSD_WIDTH), bsz, seq)

        blk5 = S5_CHUNK * S5_GROUP_CH
        xg = u.reshape(t // S5_CHUNK, blk5, S5_GROUPS).transpose(2, 0, 1)
        prep = _s5_prepare(s5_a_re[l], s5_a_im[l], s5_log_dt[l], s5_b_re[l], s5_b_im[l], s5_c_re[l],
                           s5_c_im[l], s5_d[l], glu_w[l], glu_b[l])
        og = _s5_mixer(xg, prep, bsz)
        s5o = og.transpose(1, 2, 0).reshape(t, S5_WIDTH)

        w_o = jnp.concatenate([w_out[l][:SSD_WIDTH], _s5_cols_to_hg(w_out[l][SSD_WIDTH:], axis=0)], axis=0)
        x1, h2, q = _out_proj(yssd, s5o, xt, gate1, scale2, shift2, _s5_cols_to_hg(norm_s5_g[l], axis=0).reshape(1, -1),
                              norm2_g[l].reshape(1, d), w_o.astype(BF16), w_query[l].astype(BF16), seq)

        keys = sub_keys[l].reshape(2 * PEER_HEADS, PEER_NKEYS, PEER_HALF).astype(BF16)
        n_exp = expert_u.shape[1]
        uv3 = jnp.concatenate([expert_u[l].reshape(n_exp, SUBLANES, LANES),
                               expert_v[l].reshape(n_exp, SUBLANES, LANES)], axis=1)
        last = l == depth - 1
        assert last, "the final RMSNorm is fused into the last layer's PEER kernel"
        xt = _peer_ffn(q, keys, h2, x1, gate2, norm_f_g.reshape(1, d), uv3, seq)
    return xt.reshape(bsz, seq, d)
```

```python
import functools

import jax
import jax.numpy as jnp
from jax import lax
from jax.experimental import pallas as pl
from jax.experimental.pallas import tpu as pltpu

F32 = jnp.float32
BF16 = jnp.bfloat16
I32 = jnp.int32

D_MODEL = 1024
SSD_WIDTH = 1024
SSD_HEAD_DIM = 64
SSD_HEADS = 16
SSD_GROUPS = 4
SSD_STATE = 128
SSD_CONV = 4
SSD_CHUNK = 128
S5_WIDTH = 1024
S5_GROUP_CH = 16
S5_GROUPS = 64
S5_STATE = 64
S5_CHUNK = 16
PEER_HEADS = 8
PEER_NKEYS = 128
PEER_TOPK = 16
PEER_HALF = 128
PEER_SLOTS = PEER_HEADS * PEER_TOPK
EPS = 1e-6
LANES = 128
DT_PAD = LANES
SSD_PROJ_WIDTH = SSD_WIDTH + 2 * 1024 + DT_PAD
COL_Z, COL_XS, COL_BC = 0, 1, 2
COL_DT = (3 * 1024) // DT_PAD
HIGHEST = lax.Precision.HIGHEST
VMEM_LIMIT = 56 * 1024 * 1024


def _silu(v):
    return v * jax.nn.sigmoid(v)


def _gelu_exact(v):
    return 0.5 * v * (1.0 + lax.erf(v * (2.0 ** -0.5)))


def _rms(v, g):
    return v * lax.rsqrt(jnp.mean(v * v, axis=-1, keepdims=True) + EPS) * g


def _mod_body(c_ref, w_ref, b_ref, o_ref):
    o_ref[...] = jnp.dot(_silu(c_ref[...]), w_ref[...], preferred_element_type=F32) + b_ref[...]


def _adaln_mod(c, w_ada, b_ada):
    bsz, d = c.shape
    n = w_ada.shape[1]
    tn = 1024
    return pl.pallas_call(
        _mod_body,
        grid=(n // tn,),
        in_specs=[pl.BlockSpec((bsz, d), lambda j: (0, 0)),
                  pl.BlockSpec((d, tn), lambda j: (0, j)),
                  pl.BlockSpec((1, tn), lambda j: (0, j))],
        out_specs=pl.BlockSpec((bsz, tn), lambda j: (0, j)),
        out_shape=jax.ShapeDtypeStruct((bsz, n), F32),
        name="adaln_mod",
    )(c, w_ada, b_ada.reshape(1, n))


def _inproj_body(x_ref, sc_ref, sh_ref, g_ref, w_ref, o_ref, u_ref):
    h = _rms(x_ref[...], g_ref[...]) * (1.0 + sc_ref[0]) + sh_ref[0]
    p = jnp.dot(h.astype(BF16), w_ref[...], preferred_element_type=F32)
    o_ref[...] = p[:, 0:SSD_PROJ_WIDTH]
    u_ref[...] = p[:, SSD_PROJ_WIDTH:]


def _in_proj(x2d, scale1, shift1, g1, w_cat, seq):
    t, d = x2d.shape
    n = SSD_PROJ_WIDTH
    tm = 256
    per_b = seq // tm
    return pl.pallas_call(
        _inproj_body,
        grid=(t // tm,),
        in_specs=[pl.BlockSpec((tm, d), lambda i: (i, 0)),
                  pl.BlockSpec((1, 1, d), lambda i: (i // per_b, 0, 0)),
                  pl.BlockSpec((1, 1, d), lambda i: (i // per_b, 0, 0)),
                  pl.BlockSpec((1, d), lambda i: (0, 0)),
                  pl.BlockSpec(w_cat.shape, lambda i: (0, 0))],
        out_specs=[pl.BlockSpec((tm, n), lambda i: (i, 0)), pl.BlockSpec((tm, S5_WIDTH), lambda i: (i, 0))],
        out_shape=[jax.ShapeDtypeStruct((t, n), F32), jax.ShapeDtypeStruct((t, S5_WIDTH), F32)],
        compiler_params=pltpu.CompilerParams(dimension_semantics=("parallel",),
                                             vmem_limit_bytes=VMEM_LIMIT),
        name="in_proj",
    )(x2d, scale1, shift1, g1, w_cat)


HALO = 8
SSD_PAIRS = SSD_HEADS // 2


def _ssd_body(z_ref, xs_ref, bc_ref, dt_ref, cw_ref, cb_ref, dtb_ref, a_ref, d_ref, ng_ref,
              o_ref, win_ref, act_ref, state_ref, y_ref):
    c = pl.program_id(1)
    L = SSD_CHUNK

    @pl.when(c == 0)
    def _():
        win_ref[0:HALO, :] = jnp.zeros((HALO, 2048), F32)
        state_ref[...] = jnp.zeros(state_ref.shape, F32)

    @pl.when(c > 0)
    def _():
        win_ref[0:HALO, :] = win_ref[L:L + HALO, :]

    win_ref[HALO:HALO + L, 0:1024] = xs_ref[...]
    win_ref[HALO:HALO + L, 1024:2048] = bc_ref[...]

    for sl in range(8):
        cols = slice(sl * 256, (sl + 1) * 256)
        acc = jnp.broadcast_to(cb_ref[:, cols], (L, 256))
        for k in range(SSD_CONV):
            r0 = HALO - (SSD_CONV - 1) + k
            acc = acc + win_ref[r0:r0 + L, cols] * cw_ref[k:k + 1, cols]
        act_ref[:, cols] = _silu(acc)

    dtv = dt_ref[...] + dtb_ref[...]
    dt = jnp.maximum(dtv, 0.0) + jnp.log1p(jnp.exp(-jnp.abs(dtv)))
    da = dt * a_ref[...]
    row = lax.broadcasted_iota(I32, (L, L), 0)
    col = lax.broadcasted_iota(I32, (L, L), 1)
    causal = row >= col
    tril = jnp.where(causal, 1.0, 0.0).astype(F32)
    cs = jnp.dot(tril, da, precision=HIGHEST, preferred_element_type=F32)
    cs_t = cs.T
    dec_t = jnp.exp(cs_t[:, L - 1:L] - cs_t)
    ecs = jnp.exp(cs)
    lane = lax.broadcasted_iota(I32, (L, LANES), 1)
    first = lane < SSD_HEAD_DIM

    for g in range(SSD_GROUPS):
        bg = act_ref[:, 1024 + g * 128:1024 + (g + 1) * 128]
        cg = act_ref[:, 1536 + g * 128:1536 + (g + 1) * 128]
        bg_t = bg.T
        cg_b = cg.astype(BF16)
        cb = jnp.dot(cg_b, bg_t.astype(BF16), preferred_element_type=F32)
        for r in range(2):
            p = g * 2 + r
            h0, h1 = 2 * p, 2 * p + 1
            x2 = act_ref[:, p * 128:(p + 1) * 128]
            dt2 = jnp.where(first, dt[:, h0:h0 + 1], dt[:, h1:h1 + 1])
            xdt = (x2 * dt2).astype(BF16)
            ydiag, snew = [], []
            for h in (h0, h1):
                diff = cs[:, h:h + 1] - cs_t[h:h + 1, :]
                lmat = jnp.where(causal, jnp.exp(jnp.where(causal, diff, 0.0)), 0.0)
                ydiag.append(jnp.dot((cb * lmat).astype(BF16), xdt, preferred_element_type=F32))
                bw = (bg_t * dec_t[h:h + 1, :]).astype(BF16)
                snew.append(jnp.dot(bw, xdt, preferred_element_type=F32))
            prev = state_ref[p]
            yoff = jnp.dot(cg_b, prev.astype(BF16), preferred_element_type=F32)
            yoff = yoff * jnp.where(first, ecs[:, h0:h0 + 1], ecs[:, h1:h1 + 1])
            cdec = jnp.where(first[0:1, :], ecs[L - 1:L, h0:h0 + 1], ecs[L - 1:L, h1:h1 + 1])
            state_ref[p] = prev * cdec + jnp.where(first, snew[0], snew[1])
            y2 = jnp.where(first, ydiag[0], ydiag[1]) + yoff + d_ref[:, p * 128:(p + 1) * 128] * x2
            y_ref[:, p * 128:(p + 1) * 128] = y2

    gw = SSD_WIDTH // SSD_GROUPS
    for g in range(SSD_GROUPS):
        cols = slice(g * gw, (g + 1) * gw)
        yg = y_ref[:, cols] * _silu(z_ref[:, cols])
        o_ref[:, cols] = _rms(yg, ng_ref[:, cols])


def _ssd_mixer(proj, conv_w, conv_b, dt_bias_p, a_p, d_exp, norm_g, bsz, seq):
    t = proj.shape[0]
    L = SSD_CHUNK
    nc = seq // L
    row = lambda b, c: b * nc + c
    full = lambda shape: pl.BlockSpec(shape, lambda b, c: (0, 0))
    return pl.pallas_call(
        _ssd_body,
        grid=(bsz, nc),
        in_specs=[pl.BlockSpec((L, 1024), lambda b, c: (row(b, c), COL_Z)),
                  pl.BlockSpec((L, 1024), lambda b, c: (row(b, c), COL_XS)),
                  pl.BlockSpec((L, 1024), lambda b, c: (row(b, c), COL_BC)),
                  pl.BlockSpec((L, DT_PAD), lambda b, c: (row(b, c), COL_DT)),
                  full((SSD_CONV, 2048)), full((1, 2048)), full((1, DT_PAD)), full((1, DT_PAD)),
                  full((1, 1024)), full((1, 1024))],
        out_specs=pl.BlockSpec((L, 1024), lambda b, c: (row(b, c), 0)),
        out_shape=jax.ShapeDtypeStruct((t, SSD_WIDTH), F32),
        scratch_shapes=[pltpu.VMEM((L + HALO, 2048), F32),
                        pltpu.VMEM((L, 2048), F32),
                        pltpu.VMEM((SSD_PAIRS, SSD_STATE, LANES), F32),
                        pltpu.VMEM((L, 1024), F32)],
        compiler_params=pltpu.CompilerParams(dimension_semantics=("parallel", "arbitrary"),
                                             vmem_limit_bytes=VMEM_LIMIT),
        name="ssd_mixer",
    )(proj, proj, proj, proj, conv_w, conv_b, dt_bias_p, a_p, d_exp, norm_g)


def _s5_cols_to_hg(w, axis):
    shape = w.shape
    split = shape[:axis] + (S5_GROUPS, S5_GROUP_CH) + shape[axis + 1:]
    return jnp.swapaxes(w.reshape(split), axis, axis + 1).reshape(shape)


def _s5_prepare(a_re, a_im, log_dt, b_re, b_im, c_re, c_im, d_s5, glu_w, glu_b):
    G, P, H, C = S5_GROUPS, S5_STATE, S5_GROUP_CH, S5_CHUNK
    lr, li = a_re.astype(F32), a_im.astype(F32)
    dt = jnp.exp(log_dt.astype(F32))[:, None]
    mag = jnp.exp(lr * dt)
    lb_re, lb_im = mag * jnp.cos(li * dt), mag * jnp.sin(li * dt)
    den = lr * lr + li * li
    coef_re = ((lb_re - 1.0) * lr + lb_im * li) / den
    coef_im = (lb_im * lr - (lb_re - 1.0) * li) / den
    bb_re = coef_re[..., None] * b_re - coef_im[..., None] * b_im
    bb_im = coef_re[..., None] * b_im + coef_im[..., None] * b_re
    tau = jnp.arange(C + 1, dtype=F32)[:, None, None]
    pm = jnp.exp(lr * dt * tau)
    pw_re, pw_im = pm * jnp.cos(li * dt * tau), pm * jnp.sin(li * dt * tau)
    m_re = pw_re[..., None] * bb_re - pw_im[..., None] * bb_im
    m_im = pw_re[..., None] * bb_im + pw_im[..., None] * bb_re
    e_re = c_re * pw_re[:, :, None, :] - c_im * pw_im[:, :, None, :]
    e_im = c_re * pw_im[:, :, None, :] + c_im * pw_re[:, :, None, :]
    kern = (jnp.einsum('ghp,tgpk->tghk', c_re, m_re[:C], precision=HIGHEST)
            - jnp.einsum('ghp,tgpk->tghk', c_im, m_im[:C], precision=HIGHEST))
    j_in = jnp.arange(C)[:, None]
    j_out = jnp.arange(C)[None, :]
    lag = j_out - j_in
    toep = jnp.where((lag >= 0)[:, :, None, None, None], kern[jnp.clip(lag, 0, C - 1)], 0.0)
    toep = toep.transpose(2, 0, 4, 1, 3).reshape(G, C * H, C * H)
    rev = jnp.arange(C - 1, -1, -1)
    wst_re = m_re[rev].transpose(1, 0, 3, 2).reshape(G, C * H, P)
    wst_im = m_im[rev].transpose(1, 0, 3, 2).reshape(G, C * H, P)
    wout_re = e_re[1:].transpose(1, 3, 0, 2).reshape(G, P, C * H)
    wout_im = (-e_im[1:]).transpose(1, 3, 0, 2).reshape(G, P, C * H)
    padp = lambda v, axis: jnp.pad(v, [(0, LANES - P) if a == axis else (0, 0) for a in range(v.ndim)])
    lam_re, lam_im = padp(pw_re[C][:, None, :], 2), padp(pw_im[C][:, None, :], 2)
    dvec = jnp.tile(d_s5.astype(F32), (1, C))[:, None, :]
    eye = jnp.eye(C, dtype=F32)
    wglu = (eye[None, :, None, :, None] * glu_w.astype(F32)[:, None, :, None, :]).reshape(G, C * H, C * H)
    bglu = jnp.tile(glu_b.astype(F32), (1, C))[:, None, :]
    return (toep.astype(BF16), padp(wst_re, 2).astype(BF16), padp(wst_im, 2).astype(BF16), lam_re, lam_im,
            padp(wout_re, 1).astype(BF16), padp(wout_im, 1).astype(BF16), dvec, wglu.astype(BF16), bglu)


def _s5_body(nb, x_ref, toep_ref, wsr_ref, wsi_ref, lr_ref, li_ref, wor_ref, woi_ref, d_ref,
             wg_ref, bg_ref, o_ref, sr_ref, si_ref, pr_ref, pi_ref):
    x = x_ref[0]
    xb = x.astype(BF16)
    sr_ref[...] = jnp.dot(xb, wsr_ref[0], preferred_element_type=F32)
    si_ref[...] = jnp.dot(xb, wsi_ref[0], preferred_element_type=F32)
    lam_r = jnp.broadcast_to(lr_ref[0], (nb, LANES))
    lam_i = jnp.broadcast_to(li_ref[0], (nb, LANES))
    n_chunks = x.shape[0] // nb

    def step(c, carry):
        xr, xi = carry
        rows = pl.ds(c, nb, stride=n_chunks)
        pr_ref[rows, :] = xr
        pi_ref[rows, :] = xi
        nr = lam_r * xr - lam_i * xi + sr_ref[rows, :]
        ni = lam_r * xi + lam_i * xr + si_ref[rows, :]
        return nr, ni

    zero = jnp.zeros((nb, LANES), F32)
    lax.fori_loop(0, n_chunks, step, (zero, zero))
    y = (jnp.dot(xb, toep_ref[0], preferred_element_type=F32)
         + jnp.dot(pr_ref[...].astype(BF16), wor_ref[0], preferred_element_type=F32)
         + jnp.dot(pi_ref[...].astype(BF16), woi_ref[0], preferred_element_type=F32)
         + d_ref[0] * x)
    v = _gelu_exact(y)
    gate = jax.nn.sigmoid(jnp.dot(v.astype(BF16), wg_ref[0], preferred_element_type=F32) + bg_ref[0])
    o_ref[0] = v * gate


def _s5_mixer(xg, prep, nb):
    G, R, W = xg.shape
    toep, wsr, wsi, lam_re, lam_im, wor, woi, dvec, wglu, bglu = prep
    P = LANES
    per_g = lambda shape: pl.BlockSpec((1,) + shape, lambda g: (g, 0, 0))
    return pl.pallas_call(
        functools.partial(_s5_body, nb),
        grid=(G,),
        in_specs=[per_g((R, W)), per_g((W, W)), per_g((W, P)), per_g((W, P)), per_g((1, P)), per_g((1, P)),
                  per_g((P, W)), per_g((P, W)), per_g((1, W)), per_g((W, W)), per_g((1, W))],
        out_specs=per_g((R, W)),
        out_shape=jax.ShapeDtypeStruct((G, R, W), F32),
        scratch_shapes=[pltpu.VMEM((R, P), F32)] * 4,
        compiler_params=pltpu.CompilerParams(dimension_semantics=("parallel",),
                                             vmem_limit_bytes=VMEM_LIMIT),
        name="s5_mixer",
    )(xg, toep, wsr, wsi, lam_re, lam_im, wor, woi, dvec, wglu, bglu)


def _outproj_body(ys_ref, s5_ref, x_ref, g1_ref, sc2_ref, sh2_ref, ns5_ref, n2_ref, wo_ref, wq_ref,
                  x1_ref, h2_ref, q_ref):
    a = ys_ref[...].astype(BF16)
    b = _rms(s5_ref[...], ns5_ref[...]).astype(BF16)
    mix = (jnp.dot(a, wo_ref[0:SSD_WIDTH, :], preferred_element_type=F32)
           + jnp.dot(b, wo_ref[SSD_WIDTH:, :], preferred_element_type=F32))
    x1 = x_ref[...] + g1_ref[0] * mix
    x1_ref[...] = x1
    h2 = _rms(x1, n2_ref[...]) * (1.0 + sc2_ref[0]) + sh2_ref[0]
    h2_ref[...] = h2
    q = jnp.dot(h2.astype(BF16), wq_ref[...], preferred_element_type=F32).astype(BF16)
    for hc in range(2 * PEER_HEADS):
        q_ref[hc] = q[:, hc * PEER_HALF:(hc + 1) * PEER_HALF]


def _out_proj(yssd, s5o, x2d, gate1, scale2, shift2, ns5, n2, w_out, w_query, seq):
    t, d = x2d.shape
    nhc = w_query.shape[1] // PEER_HALF
    tm = 256
    per_b = seq // tm
    rowblk = lambda w: pl.BlockSpec((tm, w), lambda i: (i, 0))
    modblk = pl.BlockSpec((1, 1, d), lambda i: (i // per_b, 0, 0))
    full = lambda shape: pl.BlockSpec(shape, lambda i: (0, 0))
    return pl.pallas_call(
        _outproj_body,
        grid=(t // tm,),
        in_specs=[rowblk(d), rowblk(d), rowblk(d), modblk, modblk, modblk, full((1, d)), full((1, d)),
                  full(w_out.shape), full(w_query.shape)],
        out_specs=[rowblk(d), rowblk(d), pl.BlockSpec((nhc, tm, PEER_HALF), lambda i: (0, i, 0))],
        out_shape=[jax.ShapeDtypeStruct((t, d), F32), jax.ShapeDtypeStruct((t, d), F32),
                   jax.ShapeDtypeStruct((nhc, t, PEER_HALF), BF16)],
        compiler_params=pltpu.CompilerParams(dimension_semantics=("parallel",),
                                             vmem_limit_bytes=VMEM_LIMIT),
        name="out_proj_query",
    )(yssd, s5o, x2d, gate1, scale2, shift2, ns5, n2, w_out, w_query)


def _topk_rows(s, n_out, payload=None):
    n = s.shape[0]
    iota = lax.broadcasted_iota(I32, s.shape, 0)
    vals, picks = [], []
    for _ in range(n_out):
        m = jnp.max(s, axis=0, keepdims=True)
        am = jnp.min(jnp.where(s == m, iota, n), axis=0, keepdims=True)
        hit = iota == am
        vals.append(m)
        picks.append(am if payload is None else jnp.max(jnp.where(hit, payload, -1), axis=0, keepdims=True))
        s = jnp.where(hit, -jnp.inf, s)
        yield
    return jnp.concatenate(vals, axis=0), jnp.concatenate(picks, axis=0)


def _route_head_steps(q1, k1, q2, k2):
    K = PEER_TOPK
    nt = (((1,), (1,)), ((), ()))
    s1, i1 = yield from _topk_rows(lax.dot_general(k1, q1.astype(BF16), nt, preferred_element_type=F32), K)
    s2, i2 = yield from _topk_rows(lax.dot_general(k2, q2.astype(BF16), nt, preferred_element_type=F32), K)
    cs = [s1[0:1, :] + s2] + [s1[a:a + 1, :] + s2[0:8, :] for a in range(1, 8)] + [s1[8:K, :] + s2[0:1, :]]
    ci = ([i1[0:1, :] * PEER_NKEYS + i2] + [i1[a:a + 1, :] * PEER_NKEYS + i2[0:8, :] for a in range(1, 8)]
          + [i1[8:K, :] * PEER_NKEYS + i2[0:1, :]])
    top_s, ids = yield from _topk_rows(jnp.concatenate(cs, axis=0), K, payload=jnp.concatenate(ci, axis=0))
    e = jnp.exp(top_s - top_s[0:1, :])
    return e / jnp.sum(e, axis=0, keepdims=True), ids


def _run(gen, steps=None):
    try:
        while steps is None or steps > 0:
            next(gen)
            steps = None if steps is None else steps - 1
    except StopIteration as done:
        return done.value
    return None


SUBLANES = 8
PEER_TOK_TILE = 512
PEER_RING = 16
PEER_AHEAD = 8
PEER_JGROUPS = PEER_SLOTS // SUBLANES
SLAB_ROWS = 2 * SUBLANES


def _sublane_sums(p, sub):
    m4, m2, m1 = sub < 4, (sub & 2) == 0, (sub & 1) == 0
    z = jnp.where(m4, p[0:4], p[4:8]) + pltpu.roll(jnp.where(m4, p[4:8], p[0:4]), 4, 1)
    y = jnp.where(m2, z[0:2] + pltpu.roll(z[0:2], 6, 1), z[2:4] + pltpu.roll(z[2:4], 2, 1))
    return jnp.where(m1, y[0] + pltpu.roll(y[0], 7, 0), y[1] + pltpu.roll(y[1], 1, 0))


def _peer_body(q_ref, keys_ref, h_ref, x1_ref, g2_ref, nf_ref, uv_ref, o_ref,
               buf_ref, po_ref, wb_ref, gates_ref, eidv_ref, eidt_ref, eid_smem, sem, sem_eid):
    i = pl.program_id(0)
    n_tiles = pl.num_programs(0) - 1
    wslot = i % 2
    rslot = 1 - wslot
    n_blocks = PEER_TOK_TILE // PEER_RING
    assert PEER_TOK_TILE // LANES * PEER_HEADS == n_blocks
    sub = lax.broadcasted_iota(I32, (SUBLANES, LANES), 0)
    lane = lax.broadcasted_iota(I32, (SUBLANES, LANES), 1)
    lane_s = lax.broadcasted_iota(I32, (PEER_SLOTS, LANES), 1)

    def route_unit(b):
        rb, h = b // PEER_HEADS, b % PEER_HEADS
        toks = pl.ds(pl.multiple_of(rb * LANES, LANES), LANES)
        gates, ids = yield from _route_head_steps(q_ref[2 * h, toks, :], keys_ref[2 * h],
                                                  q_ref[2 * h + 1, toks, :], keys_ref[2 * h + 1])
        rows = pl.ds(pl.multiple_of(h * PEER_TOPK, PEER_TOPK), PEER_TOPK)
        gates_ref[wslot, rb, rows, :] = gates
        eidv_ref[rows, :] = ids

    def publish_ids(b):
        rb, h = b // PEER_HEADS, b % PEER_HEADS

        @pl.when(h == PEER_HEADS - 1)
        def _():
            toks = pl.ds(pl.multiple_of(rb * LANES, LANES), LANES)
            eidt_ref[...] = eidv_ref[...].astype(F32).T.astype(I32)
            cp = pltpu.make_async_copy(eidt_ref, eid_smem.at[wslot, toks, :], sem_eid)
            cp.start()
            cp.wait()

    def issue_rows(par, t, slot, k0, k1):
        for k in range(k0, k1):
            e = eid_smem[par, t, k]
            pltpu.make_async_copy(uv_ref.at[e], buf_ref.at[slot, k], sem.at[slot]).start(priority=k % 2)

    def wait_rows(slot):
        pltpu.make_async_copy(uv_ref.at[pl.ds(0, PEER_SLOTS)], buf_ref.at[slot], sem.at[slot]).wait()

    def token(t, slot, h_rows, routing):
        t_next = (t + PEER_AHEAD) % PEER_TOK_TILE
        par_next = jnp.where(t + PEER_AHEAD >= PEER_TOK_TILE, wslot, rslot)
        nslot = (slot + PEER_AHEAD) % PEER_RING
        per = PEER_SLOTS // 2 // PEER_JGROUPS
        wait_rows(slot)
        r = slot % SUBLANES
        h_t = jnp.concatenate([h_rows[r:r + 1, LANES * s:LANES * (s + 1)] for s in range(SUBLANES)], axis=0)
        dense = jnp.zeros((SUBLANES, LANES), F32)
        for j in range(PEER_JGROUPS):
            prods = buf_ref[slot, SUBLANES * j:SUBLANES * (j + 1), 0:SUBLANES, :] * h_t[None]
            col = jnp.sum(_sublane_sums(prods, sub), axis=1, keepdims=True)
            dense = jnp.where(lane == j, col, dense)
            issue_rows(par_next, t_next, nslot, j * per, (j + 1) * per)
            if j == PEER_JGROUPS // 2 - 1:
                _run(routing, steps=1)
        _run(routing, steps=1)
        act = _gelu_exact(dense)
        gblk = gates_ref[rslot, t // LANES]
        gcol = jnp.sum(jnp.where(lane_s == t % LANES, gblk, 0.0), axis=1, keepdims=True)
        for j in range(PEER_JGROUPS):
            rows = slice(SUBLANES * j, SUBLANES * (j + 1))
            wb_ref[rows, :] = jnp.broadcast_to(act[:, j:j + 1] * gcol[rows, :], (SUBLANES, LANES))

        accs = [jnp.zeros((SUBLANES, LANES), F32) for _ in range(4)]
        for k in range(PEER_SLOTS):
            wk = jnp.broadcast_to(wb_ref[k:k + 1, :], (SUBLANES, LANES))
            accs[k % 4] = accs[k % 4] + buf_ref[slot, k, SUBLANES:SLAB_ROWS, :] * wk
            if k % SUBLANES == SUBLANES - 1:
                j = k // SUBLANES
                issue_rows(par_next, t_next, nslot, PEER_SLOTS // 2 + j * per, PEER_SLOTS // 2 + (j + 1) * per)
            if k == PEER_SLOTS // 2 - 1:
                _run(routing, steps=1)
        out = (accs[0] + accs[1]) + (accs[2] + accs[3])
        return jnp.concatenate([out[s:s + 1, :] for s in range(SUBLANES)], axis=1)

    @pl.when(i == 0)
    def _():
        def first(b, _):
            _run(route_unit(b))
            publish_ids(b)
            return 0

        lax.fori_loop(0, n_blocks, first, 0)

    @pl.when(i == 1)
    def _():
        def prologue(t, _):
            issue_rows(rslot, t, t, 0, PEER_SLOTS)
            return 0

        lax.fori_loop(0, PEER_AHEAD, prologue, 0)

    @pl.when(i > 0)
    def _():
        def block(b, _):
            routing = route_unit(b)
            base = pl.multiple_of(b * PEER_RING, PEER_RING)
            rows = []
            for slot in range(PEER_RING):
                if slot % SUBLANES == 0:
                    h_rows = h_ref[pl.ds(pl.multiple_of(base + slot, SUBLANES), SUBLANES), :]
                rows.append(token(base + slot, slot, h_rows, routing))
            _run(routing)
            po_ref[pl.ds(base, PEER_RING), :] = jnp.concatenate(rows, axis=0)
            publish_ids(b)
            return 0

        lax.fori_loop(0, n_blocks, block, 0)
        x2 = x1_ref[...] + g2_ref[0] * po_ref[...]
        o_ref[...] = _rms(x2, nf_ref[...])

    @pl.when(i == n_tiles)
    def _():
        for slot in range(PEER_AHEAD):
            wait_rows(slot)


def _peer_ffn(q, keys, h2, x1, gate2, nf, uv3, seq):
    t, d = h2.shape
    nhc, _, dh = q.shape
    tt = PEER_TOK_TILE
    n_tiles = t // tt
    per_b = seq // tt
    prev = lambda i: jnp.maximum(i - 1, 0)
    tokblk = pl.BlockSpec((tt, d), lambda i: (prev(i), 0))
    return pl.pallas_call(
        _peer_body,
        grid=(n_tiles + 1,),
        in_specs=[pl.BlockSpec((nhc, tt, dh), lambda i: (0, jnp.minimum(i, n_tiles - 1), 0)),
                  pl.BlockSpec(keys.shape, lambda i: (0, 0, 0)),
                  tokblk, tokblk,
                  pl.BlockSpec((1, 1, d), lambda i: (prev(i) // per_b, 0, 0)),
                  pl.BlockSpec((1, d), lambda i: (0, 0)),
                  pl.BlockSpec(memory_space=pl.ANY)],
        out_specs=tokblk,
        out_shape=jax.ShapeDtypeStruct((t, d), F32),
        scratch_shapes=[pltpu.VMEM((PEER_RING, PEER_SLOTS, SLAB_ROWS, LANES), F32),
                        pltpu.VMEM((tt, d), F32),
                        pltpu.VMEM((PEER_SLOTS, LANES), F32),
                        pltpu.VMEM((2, tt // LANES, PEER_SLOTS, LANES), F32),
                        pltpu.VMEM((PEER_SLOTS, LANES), I32),
                        pltpu.VMEM((LANES, PEER_SLOTS), I32),
                        pltpu.SMEM((2, tt, PEER_SLOTS), I32),
                        pltpu.SemaphoreType.DMA((PEER_RING,)),
                        pltpu.SemaphoreType.DMA(())],
        compiler_params=pltpu.CompilerParams(dimension_semantics=("arbitrary",),
                                             vmem_limit_bytes=VMEM_LIMIT),
        name="peer_ffn",
    )(q, keys, h2, x1, gate2, nf, uv3)


def kernel(x, c, w_ada, b_ada, norm1_g, w_in, conv_w, conv_b, dt_bias, a_log, d_ssd, norm_ssd_g, s5_a_re, s5_a_im, s5_log_dt, s5_b_re, s5_b_im, s5_c_re, s5_c_im, s5_d, glu_w, glu_b, norm_s5_g, w_out, norm2_g, w_query, sub_keys, expert_u, expert_v, norm_f_g):
    bsz, seq, d = x.shape
    t = bsz * seq
    depth = w_ada.shape[0]
    xt = x.reshape(t, d)
    for l in range(depth):
        mod = _adaln_mod(c, w_ada[l], b_ada[l])
        shift1, scale1, gate1, shift2, scale2, gate2 = [m.reshape(bsz, 1, d) for m in jnp.split(mod, 6, axis=-1)]

        wz, wxbc, wdt, wu = jnp.split(w_in[l], [SSD_WIDTH, SSD_WIDTH + 2048, SSD_WIDTH + 2048 + SSD_HEADS], axis=1)
        wu = _s5_cols_to_hg(wu, axis=1)
        w_cat = jnp.concatenate([wz, wxbc, jnp.pad(wdt, ((0, 0), (0, DT_PAD - SSD_HEADS))), wu], axis=1).astype(BF16)
        proj, u = _in_proj(xt, scale1, shift1, norm1_g[l].reshape(1, d), w_cat, seq)

        pad_h = lambda v: jnp.pad(v.astype(F32), (0, DT_PAD - SSD_HEADS)).reshape(1, DT_PAD)
        yssd = _ssd_mixer(proj, conv_w[l], conv_b[l].reshape(1, -1), pad_h(dt_bias[l]),
                          pad_h(-jnp.exp(a_log[l].astype(F32))),
                          jnp.repeat(d_ssd[l].astype(F32), SSD_HEAD_DIM).reshape(1, SSD_WIDTH),
                          norm_ssd_g[l].reshape(1, SSD_WIDTH), bsz, seq)

        blk5 = S5_CHUNK * S5_GROUP_CH
        xg = u.reshape(t // S5_CHUNK, blk5, S5_GROUPS).transpose(2, 0, 1)
        prep = _s5_prepare(s5_a_re[l], s5_a_im[l], s5_log_dt[l], s5_b_re[l], s5_b_im[l], s5_c_re[l],
                           s5_c_im[l], s5_d[l], glu_w[l], glu_b[l])
        og = _s5_mixer(xg, prep, bsz)
        s5o = og.transpose(1, 2, 0).reshape(t, S5_WIDTH)

        w_o = jnp.concatenate([w_out[l][:SSD_WIDTH], _s5_cols_to_hg(w_out[l][SSD_WIDTH:], axis=0)], axis=0)
        x1, h2, q = _out_proj(yssd, s5o, xt, gate1, scale2, shift2, _s5_cols_to_hg(norm_s5_g[l], axis=0).reshape(1, -1),
                              norm2_g[l].reshape(1, d), w_o.astype(BF16), w_query[l].astype(BF16), seq)

        keys = sub_keys[l].reshape(2 * PEER_HEADS, PEER_NKEYS, PEER_HALF).astype(BF16)
        n_exp = expert_u.shape[1]
        uv3 = jnp.concatenate([expert_u[l].reshape(n_exp, SUBLANES, LANES),
                               expert_v[l].reshape(n_exp, SUBLANES, LANES)], axis=1)
        last = l == depth - 1
        assert last, "the final RMSNorm is fused into the last layer's PEER kernel"
        xt = _peer_ffn(q, keys, h2, x1, gate2, norm_f_g.reshape(1, d), uv3, seq)
    return xt.reshape(bsz, seq, d)
```

```python
import functools

import jax
import jax.numpy as jnp
from jax import lax
from jax.experimental import pallas as pl
from jax.experimental.pallas import tpu as pltpu

F32 = jnp.float32
BF16 = jnp.bfloat16
I32 = jnp.int32

D_MODEL = 1024
SSD_WIDTH = 1024
SSD_HEAD_DIM = 64
SSD_HEADS = 16
SSD_GROUPS = 4
SSD_STATE = 128
SSD_CONV = 4
SSD_CHUNK = 128
S5_WIDTH = 1024
S5_GROUP_CH = 16
S5_GROUPS = 64
S5_STATE = 64
S5_CHUNK = 16
PEER_HEADS = 8
PEER_NKEYS = 128
PEER_TOPK = 16
PEER_HALF = 128
PEER_SLOTS = PEER_HEADS * PEER_TOPK
EPS = 1e-6
LANES = 128
DT_PAD = LANES
SSD_PROJ_WIDTH = SSD_WIDTH + 2 * 1024 + DT_PAD
COL_Z, COL_XS, COL_BC = 0, 1, 2
COL_DT = (3 * 1024) // DT_PAD
HIGHEST = lax.Precision.HIGHEST
VMEM_LIMIT = 56 * 1024 * 1024


def _silu(v):
    return v * jax.nn.sigmoid(v)


def _gelu_exact(v):
    return 0.5 * v * (1.0 + lax.erf(v * (2.0 ** -0.5)))


def _rms(v, g):
    return v * lax.rsqrt(jnp.mean(v * v, axis=-1, keepdims=True) + EPS) * g


def _mod_body(c_ref, w_ref, b_ref, o_ref):
    o_ref[...] = jnp.dot(_silu(c_ref[...]), w_ref[...], preferred_element_type=F32) + b_ref[...]


def _adaln_mod(c, w_ada, b_ada):
    bsz, d = c.shape
    n = w_ada.shape[1]
    tn = 1024
    return pl.pallas_call(
        _mod_body,
        grid=(n // tn,),
        in_specs=[pl.BlockSpec((bsz, d), lambda j: (0, 0)),
                  pl.BlockSpec((d, tn), lambda j: (0, j)),
                  pl.BlockSpec((1, tn), lambda j: (0, j))],
        out_specs=pl.BlockSpec((bsz, tn), lambda j: (0, j)),
        out_shape=jax.ShapeDtypeStruct((bsz, n), F32),
        name="adaln_mod",
    )(c, w_ada, b_ada.reshape(1, n))


def _inproj_body(x_ref, sc_ref, sh_ref, g_ref, w_ref, o_ref, u_ref):
    h = _rms(x_ref[...], g_ref[...]) * (1.0 + sc_ref[0]) + sh_ref[0]
    p = jnp.dot(h.astype(BF16), w_ref[...], preferred_element_type=F32)
    o_ref[...] = p[:, 0:SSD_PROJ_WIDTH]
    u_ref[...] = p[:, SSD_PROJ_WIDTH:]


def _in_proj(x2d, scale1, shift1, g1, w_cat, seq):
    t, d = x2d.shape
    n = SSD_PROJ_WIDTH
    tm = 256
    per_b = seq // tm
    return pl.pallas_call(
        _inproj_body,
        grid=(t // tm,),
        in_specs=[pl.BlockSpec((tm, d), lambda i: (i, 0)),
                  pl.BlockSpec((1, 1, d), lambda i: (i // per_b, 0, 0)),
                  pl.BlockSpec((1, 1, d), lambda i: (i // per_b, 0, 0)),
                  pl.BlockSpec((1, d), lambda i: (0, 0)),
                  pl.BlockSpec(w_cat.shape, lambda i: (0, 0))],
        out_specs=[pl.BlockSpec((tm, n), lambda i: (i, 0)), pl.BlockSpec((tm, S5_WIDTH), lambda i: (i, 0))],
        out_shape=[jax.ShapeDtypeStruct((t, n), F32), jax.ShapeDtypeStruct((t, S5_WIDTH), F32)],
        compiler_params=pltpu.CompilerParams(dimension_semantics=("parallel",),
                                             vmem_limit_bytes=VMEM_LIMIT),
        name="in_proj",
    )(x2d, scale1, shift1, g1, w_cat)


HALO = 8
SSD_PAIRS = SSD_HEADS // 2


def _ssd_body(z_ref, xs_ref, bc_ref, dt_ref, cw_ref, cb_ref, dtb_ref, a_ref, d_ref, ng_ref,
              o_ref, win_ref, act_ref, state_ref, y_ref):
    c = pl.program_id(1)
    L = SSD_CHUNK

    @pl.when(c == 0)
    def _():
        win_ref[0:HALO, :] = jnp.zeros((HALO, 2048), F32)
        state_ref[...] = jnp.zeros(state_ref.shape, F32)

    @pl.when(c > 0)
    def _():
        win_ref[0:HALO, :] = win_ref[L:L + HALO, :]

    win_ref[HALO:HALO + L, 0:1024] = xs_ref[...]
    win_ref[HALO:HALO + L, 1024:2048] = bc_ref[...]

    for sl in range(8):
        cols = slice(sl * 256, (sl + 1) * 256)
        acc = jnp.broadcast_to(cb_ref[:, cols], (L, 256))
        for k in range(SSD_CONV):
            r0 = HALO - (SSD_CONV - 1) + k
            acc = acc + win_ref[r0:r0 + L, cols] * cw_ref[k:k + 1, cols]
        act_ref[:, cols] = _silu(acc)

    dtv = dt_ref[...] + dtb_ref[...]
    dt = jnp.maximum(dtv, 0.0) + jnp.log1p(jnp.exp(-jnp.abs(dtv)))
    da = dt * a_ref[...]
    row = lax.broadcasted_iota(I32, (L, L), 0)
    col = lax.broadcasted_iota(I32, (L, L), 1)
    causal = row >= col
    tril = jnp.where(causal, 1.0, 0.0).astype(F32)
    cs = jnp.dot(tril, da, precision=HIGHEST, preferred_element_type=F32)
    cs_t = cs.T
    dec_t = jnp.exp(cs_t[:, L - 1:L] - cs_t)
    ecs = jnp.exp(cs)
    lane = lax.broadcasted_iota(I32, (L, LANES), 1)
    first = lane < SSD_HEAD_DIM

    for g in range(SSD_GROUPS):
        bg = act_ref[:, 1024 + g * 128:1024 + (g + 1) * 128]
        cg = act_ref[:, 1536 + g * 128:1536 + (g + 1) * 128]
        bg_t = bg.T
        cg_b = cg.astype(BF16)
        cb = jnp.dot(cg_b, bg_t.astype(BF16), preferred_element_type=F32)
        for r in range(2):
            p = g * 2 + r
            h0, h1 = 2 * p, 2 * p + 1
            x2 = act_ref[:, p * 128:(p + 1) * 128]
            dt2 = jnp.where(first, dt[:, h0:h0 + 1], dt[:, h1:h1 + 1])
            xdt = (x2 * dt2).astype(BF16)
            ydiag, snew = [], []
            for h in (h0, h1):
                diff = cs[:, h:h + 1] - cs_t[h:h + 1, :]
                lmat = jnp.where(causal, jnp.exp(jnp.where(causal, diff, 0.0)), 0.0)
                ydiag.append(jnp.dot((cb * lmat).astype(BF16), xdt, preferred_element_type=F32))
                bw = (bg_t * dec_t[h:h + 1, :]).astype(BF16)
                snew.append(jnp.dot(bw, xdt, preferred_element_type=F32))
            prev = state_ref[p]
            yoff = jnp.dot(cg_b, prev.astype(BF16), preferred_element_type=F32)
            yoff = yoff * jnp.where(first, ecs[:, h0:h0 + 1], ecs[:, h1:h1 + 1])
            cdec = jnp.where(first[0:1, :], ecs[L - 1:L, h0:h0 + 1], ecs[L - 1:L, h1:h1 + 1])
            state_ref[p] = prev * cdec + jnp.where(first, snew[0], snew[1])
            y2 = jnp.where(first, ydiag[0], ydiag[1]) + yoff + d_ref[:, p * 128:(p + 1) * 128] * x2
            y_ref[:, p * 128:(p + 1) * 128] = y2

    gw = SSD_WIDTH // SSD_GROUPS
    for g in range(SSD_GROUPS):
        cols = slice(g * gw, (g + 1) * gw)
        yg = y_ref[:, cols] * _silu(z_ref[:, cols])
        o_ref[:, cols] = _rms(yg, ng_ref[:, cols])


def _ssd_mixer(proj, conv_w, conv_b, dt_bias_p, a_p, d_exp, norm_g, bsz, seq):
    t = proj.shape[0]
    L = SSD_CHUNK
    nc = seq // L
    row = lambda b, c: b * nc + c
    full = lambda shape: pl.BlockSpec(shape, lambda b, c: (0, 0))
    return pl.pallas_call(
        _ssd_body,
        grid=(bsz, nc),
        in_specs=[pl.BlockSpec((L, 1024), lambda b, c: (row(b, c), COL_Z)),
                  pl.BlockSpec((L, 1024), lambda b, c: (row(b, c), COL_XS)),
                  pl.BlockSpec((L, 1024), lambda b, c: (row(b, c), COL_BC)),
                  pl.BlockSpec((L, DT_PAD), lambda b, c: (row(b, c), COL_DT)),
                  full((SSD_CONV, 2048)), full((1, 2048)), full((1, DT_PAD)), full((1, DT_PAD)),
                  full((1, 1024)), full((1, 1024))],
        out_specs=pl.BlockSpec((L, 1024), lambda b, c: (row(b, c), 0)),
        out_shape=jax.ShapeDtypeStruct((t, SSD_WIDTH), F32),
        scratch_shapes=[pltpu.VMEM((L + HALO, 2048), F32),
                        pltpu.VMEM((L, 2048), F32),
                        pltpu.VMEM((SSD_PAIRS, SSD_STATE, LANES), F32),
                        pltpu.VMEM((L, 1024), F32)],
        compiler_params=pltpu.CompilerParams(dimension_semantics=("parallel", "arbitrary"),
                                             vmem_limit_bytes=VMEM_LIMIT),
        name="ssd_mixer",
    )(proj, proj, proj, proj, conv_w, conv_b, dt_bias_p, a_p, d_exp, norm_g)


def _s5_cols_to_hg(w, axis):
    shape = w.shape
    split = shape[:axis] + (S5_GROUPS, S5_GROUP_CH) + shape[axis + 1:]
    return jnp.swapaxes(w.reshape(split), axis, axis + 1).reshape(shape)


def _s5_prepare(a_re, a_im, log_dt, b_re, b_im, c_re, c_im, d_s5, glu_w, glu_b):
    G, P, H, C = S5_GROUPS, S5_STATE, S5_GROUP_CH, S5_CHUNK
    lr, li = a_re.astype(F32), a_im.astype(F32)
    dt = jnp.exp(log_dt.astype(F32))[:, None]
    mag = jnp.exp(lr * dt)
    lb_re, lb_im = mag * jnp.cos(li * dt), mag * jnp.sin(li * dt)
    den = lr * lr + li * li
    coef_re = ((lb_re - 1.0) * lr + lb_im * li) / den
    coef_im = (lb_im * lr - (lb_re - 1.0) * li) / den
    bb_re = coef_re[..., None] * b_re - coef_im[..., None] * b_im
    bb_im = coef_re[..., None] * b_im + coef_im[..., None] * b_re
    tau = jnp.arange(C + 1, dtype=F32)[:, None, None]
    pm = jnp.exp(lr * dt * tau)
    pw_re, pw_im = pm * jnp.cos(li * dt * tau), pm * jnp.sin(li * dt * tau)
    m_re = pw_re[..., None] * bb_re - pw_im[..., None] * bb_im
    m_im = pw_re[..., None] * bb_im + pw_im[..., None] * bb_re
    e_re = c_re * pw_re[:, :, None, :] - c_im * pw_im[:, :, None, :]
    e_im = c_re * pw_im[:, :, None, :] + c_im * pw_re[:, :, None, :]
    kern = (jnp.einsum('ghp,tgpk->tghk', c_re, m_re[:C], precision=HIGHEST)
            - jnp.einsum('ghp,tgpk->tghk', c_im, m_im[:C], precision=HIGHEST))
    j_in = jnp.arange(C)[:, None]
    j_out = jnp.arange(C)[None, :]
    lag = j_out - j_in
    toep = jnp.where((lag >= 0)[:, :, None, None, None], kern[jnp.clip(lag, 0, C - 1)], 0.0)
    toep = toep.transpose(2, 0, 4, 1, 3).reshape(G, C * H, C * H)
    rev = jnp.arange(C - 1, -1, -1)
    wst_re = m_re[rev].transpose(1, 0, 3, 2).reshape(G, C * H, P)
    wst_im = m_im[rev].transpose(1, 0, 3, 2).reshape(G, C * H, P)
    wout_re = e_re[1:].transpose(1, 3, 0, 2).reshape(G, P, C * H)
    wout_im = (-e_im[1:]).transpose(1, 3, 0, 2).reshape(G, P, C * H)
    lam_re, lam_im = pw_re[C][:, None, :], pw_im[C][:, None, :]
    dvec = jnp.tile(d_s5.astype(F32), (1, C))[:, None, :]
    eye = jnp.eye(C, dtype=F32)
    wglu = (eye[None, :, None, :, None] * glu_w.astype(F32)[:, None, :, None, :]).reshape(G, C * H, C * H)
    bglu = jnp.tile(glu_b.astype(F32), (1, C))[:, None, :]
    return (toep.astype(BF16), wst_re.astype(BF16), wst_im.astype(BF16), lam_re, lam_im,
            wout_re.astype(BF16), wout_im.astype(BF16), dvec, wglu.astype(BF16), bglu)


def _s5_body(nb, x_ref, toep_ref, wsr_ref, wsi_ref, lr_ref, li_ref, wor_ref, woi_ref, d_ref,
             wg_ref, bg_ref, o_ref, sr_ref, si_ref, pr_ref, pi_ref):
    x = x_ref[0]
    xb = x.astype(BF16)
    sr_ref[...] = jnp.dot(xb, wsr_ref[0], preferred_element_type=F32)
    si_ref[...] = jnp.dot(xb, wsi_ref[0], preferred_element_type=F32)
    lam_r = jnp.broadcast_to(lr_ref[0], (nb, S5_STATE))
    lam_i = jnp.broadcast_to(li_ref[0], (nb, S5_STATE))
    n_chunks = x.shape[0] // nb

    def step(c, carry):
        xr, xi = carry
        rows = pl.ds(pl.multiple_of(c * nb, nb), nb)
        pr_ref[rows, :] = xr
        pi_ref[rows, :] = xi
        nr = lam_r * xr - lam_i * xi + sr_ref[rows, :]
        ni = lam_r * xi + lam_i * xr + si_ref[rows, :]
        return nr, ni

    zero = jnp.zeros((nb, S5_STATE), F32)
    lax.fori_loop(0, n_chunks, step, (zero, zero))
    y = (jnp.dot(xb, toep_ref[0], preferred_element_type=F32)
         + jnp.dot(pr_ref[...].astype(BF16), wor_ref[0], preferred_element_type=F32)
         + jnp.dot(pi_ref[...].astype(BF16), woi_ref[0], preferred_element_type=F32)
         + d_ref[0] * x)
    v = _gelu_exact(y)
    gate = jax.nn.sigmoid(jnp.dot(v.astype(BF16), wg_ref[0], preferred_element_type=F32) + bg_ref[0])
    o_ref[0] = v * gate


def _s5_mixer(xg, prep, nb):
    G, R, W = xg.shape
    toep, wsr, wsi, lam_re, lam_im, wor, woi, dvec, wglu, bglu = prep
    P = S5_STATE
    per_g = lambda shape: pl.BlockSpec((1,) + shape, lambda g: (g, 0, 0))
    return pl.pallas_call(
        functools.partial(_s5_body, nb),
        grid=(G,),
        in_specs=[per_g((R, W)), per_g((W, W)), per_g((W, P)), per_g((W, P)), per_g((1, P)), per_g((1, P)),
                  per_g((P, W)), per_g((P, W)), per_g((1, W)), per_g((W, W)), per_g((1, W))],
        out_specs=per_g((R, W)),
        out_shape=jax.ShapeDtypeStruct((G, R, W), F32),
        scratch_shapes=[pltpu.VMEM((R, P), F32)] * 4,
        compiler_params=pltpu.CompilerParams(dimension_semantics=("parallel",),
                                             vmem_limit_bytes=VMEM_LIMIT),
        name="s5_mixer",
    )(xg, toep, wsr, wsi, lam_re, lam_im, wor, woi, dvec, wglu, bglu)


def _outproj_body(ys_ref, s5_ref, x_ref, g1_ref, sc2_ref, sh2_ref, ns5_ref, n2_ref, wo_ref, wq_ref,
                  x1_ref, h2_ref, q_ref):
    a = ys_ref[...].astype(BF16)
    b = _rms(s5_ref[...], ns5_ref[...]).astype(BF16)
    mix = (jnp.dot(a, wo_ref[0:SSD_WIDTH, :], preferred_element_type=F32)
           + jnp.dot(b, wo_ref[SSD_WIDTH:, :], preferred_element_type=F32))
    x1 = x_ref[...] + g1_ref[0] * mix
    x1_ref[...] = x1
    h2 = _rms(x1, n2_ref[...]) * (1.0 + sc2_ref[0]) + sh2_ref[0]
    h2_ref[...] = h2
    q = jnp.dot(h2.astype(BF16), wq_ref[...], preferred_element_type=F32).astype(BF16)
    for hc in range(2 * PEER_HEADS):
        q_ref[hc] = q[:, hc * PEER_HALF:(hc + 1) * PEER_HALF]


def _out_proj(yssd, s5o, x2d, gate1, scale2, shift2, ns5, n2, w_out, w_query, seq):
    t, d = x2d.shape
    nhc = w_query.shape[1] // PEER_HALF
    tm = 256
    per_b = seq // tm
    rowblk = lambda w: pl.BlockSpec((tm, w), lambda i: (i, 0))
    modblk = pl.BlockSpec((1, 1, d), lambda i: (i // per_b, 0, 0))
    full = lambda shape: pl.BlockSpec(shape, lambda i: (0, 0))
    return pl.pallas_call(
        _outproj_body,
        grid=(t // tm,),
        in_specs=[rowblk(d), rowblk(d), rowblk(d), modblk, modblk, modblk, full((1, d)), full((1, d)),
                  full(w_out.shape), full(w_query.shape)],
        out_specs=[rowblk(d), rowblk(d), pl.BlockSpec((nhc, tm, PEER_HALF), lambda i: (0, i, 0))],
        out_shape=[jax.ShapeDtypeStruct((t, d), F32), jax.ShapeDtypeStruct((t, d), F32),
                   jax.ShapeDtypeStruct((nhc, t, PEER_HALF), BF16)],
        compiler_params=pltpu.CompilerParams(dimension_semantics=("parallel",),
                                             vmem_limit_bytes=VMEM_LIMIT),
        name="out_proj_query",
    )(yssd, s5o, x2d, gate1, scale2, shift2, ns5, n2, w_out, w_query)


def _topk_rows(s, n_out, payload=None):
    n = s.shape[0]
    iota = lax.broadcasted_iota(I32, s.shape, 0)
    vals, picks = [], []
    for _ in range(n_out):
        m = jnp.max(s, axis=0, keepdims=True)
        am = jnp.min(jnp.where(s == m, iota, n), axis=0, keepdims=True)
        hit = iota == am
        vals.append(m)
        picks.append(am if payload is None else jnp.max(jnp.where(hit, payload, -1), axis=0, keepdims=True))
        s = jnp.where(hit, -jnp.inf, s)
        yield
    return jnp.concatenate(vals, axis=0), jnp.concatenate(picks, axis=0)


def _route_head_steps(q1, k1, q2, k2):
    K = PEER_TOPK
    nt = (((1,), (1,)), ((), ()))
    s1, i1 = yield from _topk_rows(lax.dot_general(k1, q1.astype(BF16), nt, preferred_element_type=F32), K)
    s2, i2 = yield from _topk_rows(lax.dot_general(k2, q2.astype(BF16), nt, preferred_element_type=F32), K)
    cs = [s1[0:1, :] + s2] + [s1[a:a + 1, :] + s2[0:8, :] for a in range(1, 8)] + [s1[8:K, :] + s2[0:1, :]]
    ci = ([i1[0:1, :] * PEER_NKEYS + i2] + [i1[a:a + 1, :] * PEER_NKEYS + i2[0:8, :] for a in range(1, 8)]
          + [i1[8:K, :] * PEER_NKEYS + i2[0:1, :]])
    top_s, ids = yield from _topk_rows(jnp.concatenate(cs, axis=0), K, payload=jnp.concatenate(ci, axis=0))
    e = jnp.exp(top_s - top_s[0:1, :])
    return e / jnp.sum(e, axis=0, keepdims=True), ids


def _run(gen, steps=None):
    try:
        while steps is None or steps > 0:
            next(gen)
            steps = None if steps is None else steps - 1
    except StopIteration as done:
        return done.value
    return None


SUBLANES = 8
PEER_TOK_TILE = 512
PEER_RING = 16
PEER_AHEAD = 8
PEER_JGROUPS = PEER_SLOTS // SUBLANES
SLAB_ROWS = 2 * SUBLANES


def _sublane_sums(p, sub):
    m4, m2, m1 = sub < 4, (sub & 2) == 0, (sub & 1) == 0
    z = jnp.where(m4, p[0:4], p[4:8]) + pltpu.roll(jnp.where(m4, p[4:8], p[0:4]), 4, 1)
    y = jnp.where(m2, z[0:2] + pltpu.roll(z[0:2], 6, 1), z[2:4] + pltpu.roll(z[2:4], 2, 1))
    return jnp.where(m1, y[0] + pltpu.roll(y[0], 7, 0), y[1] + pltpu.roll(y[1], 1, 0))


def _peer_body(q_ref, keys_ref, h_ref, x1_ref, g2_ref, nf_ref, uv_ref, o_ref,
               buf_ref, po_ref, wb_ref, gates_ref, eidv_ref, eidt_ref, eid_smem, sem, sem_eid):
    i = pl.program_id(0)
    n_tiles = pl.num_programs(0) - 1
    wslot = i % 2
    rslot = 1 - wslot
    n_blocks = PEER_TOK_TILE // PEER_RING
    assert PEER_TOK_TILE // LANES * PEER_HEADS == n_blocks
    sub = lax.broadcasted_iota(I32, (SUBLANES, LANES), 0)
    lane = lax.broadcasted_iota(I32, (SUBLANES, LANES), 1)
    lane_s = lax.broadcasted_iota(I32, (PEER_SLOTS, LANES), 1)

    def route_unit(b):
        rb, h = b // PEER_HEADS, b % PEER_HEADS
        toks = pl.ds(pl.multiple_of(rb * LANES, LANES), LANES)
        gates, ids = yield from _route_head_steps(q_ref[2 * h, toks, :], keys_ref[2 * h],
                                                  q_ref[2 * h + 1, toks, :], keys_ref[2 * h + 1])
        rows = pl.ds(pl.multiple_of(h * PEER_TOPK, PEER_TOPK), PEER_TOPK)
        gates_ref[wslot, rb, rows, :] = gates
        eidv_ref[rows, :] = ids

    def publish_ids(b):
        rb, h = b // PEER_HEADS, b % PEER_HEADS

        @pl.when(h == PEER_HEADS - 1)
        def _():
            toks = pl.ds(pl.multiple_of(rb * LANES, LANES), LANES)
            eidt_ref[...] = eidv_ref[...].astype(F32).T.astype(I32)
            cp = pltpu.make_async_copy(eidt_ref, eid_smem.at[wslot, toks, :], sem_eid)
            cp.start()
            cp.wait()

    def issue_rows(par, t, slot, k0, k1):
        for k in range(k0, k1):
            e = eid_smem[par, t, k]
            pltpu.make_async_copy(uv_ref.at[e], buf_ref.at[slot, k], sem.at[slot]).start(priority=k % 2)

    def wait_rows(slot):
        pltpu.make_async_copy(uv_ref.at[pl.ds(0, PEER_SLOTS)], buf_ref.at[slot], sem.at[slot]).wait()

    def token(t, slot, h_rows, routing):
        t_next = (t + PEER_AHEAD) % PEER_TOK_TILE
        par_next = jnp.where(t + PEER_AHEAD >= PEER_TOK_TILE, wslot, rslot)
        nslot = (slot + PEER_AHEAD) % PEER_RING
        per = PEER_SLOTS // 2 // PEER_JGROUPS
        wait_rows(slot)
        r = slot % SUBLANES
        h_t = jnp.concatenate([h_rows[r:r + 1, LANES * s:LANES * (s + 1)] for s in range(SUBLANES)], axis=0)
        dense = jnp.zeros((SUBLANES, LANES), F32)
        for j in range(PEER_JGROUPS):
            prods = buf_ref[slot, SUBLANES * j:SUBLANES * (j + 1), 0:SUBLANES, :] * h_t[None]
            col = jnp.sum(_sublane_sums(prods, sub), axis=1, keepdims=True)
            dense = jnp.where(lane == j, col, dense)
            issue_rows(par_next, t_next, nslot, j * per, (j + 1) * per)
            if j == PEER_JGROUPS // 2 - 1:
                _run(routing, steps=1)
        _run(routing, steps=1)
        act = _gelu_exact(dense)
        gblk = gates_ref[rslot, t // LANES]
        gcol = jnp.sum(jnp.where(lane_s == t % LANES, gblk, 0.0), axis=1, keepdims=True)
        for j in range(PEER_JGROUPS):
            rows = slice(SUBLANES * j, SUBLANES * (j + 1))
            wb_ref[rows, :] = jnp.broadcast_to(act[:, j:j + 1] * gcol[rows, :], (SUBLANES, LANES))

        accs = [jnp.zeros((SUBLANES, LANES), F32) for _ in range(4)]
        for k in range(PEER_SLOTS):
            wk = jnp.broadcast_to(wb_ref[k:k + 1, :], (SUBLANES, LANES))
            accs[k % 4] = accs[k % 4] + buf_ref[slot, k, SUBLANES:SLAB_ROWS, :] * wk
            if k % SUBLANES == SUBLANES - 1:
                j = k // SUBLANES
                issue_rows(par_next, t_next, nslot, PEER_SLOTS // 2 + j * per, PEER_SLOTS // 2 + (j + 1) * per)
            if k == PEER_SLOTS // 2 - 1:
                _run(routing, steps=1)
        out = (accs[0] + accs[1]) + (accs[2] + accs[3])
        return jnp.concatenate([out[s:s + 1, :] for s in range(SUBLANES)], axis=1)

    @pl.when(i == 0)
    def _():
        def first(b, _):
            _run(route_unit(b))
            publish_ids(b)
            return 0

        lax.fori_loop(0, n_blocks, first, 0)

    @pl.when(i == 1)
    def _():
        def prologue(t, _):
            issue_rows(rslot, t, t, 0, PEER_SLOTS)
            return 0

        lax.fori_loop(0, PEER_AHEAD, prologue, 0)

    @pl.when(i > 0)
    def _():
        def block(b, _):
            routing = route_unit(b)
            base = pl.multiple_of(b * PEER_RING, PEER_RING)
            rows = []
            for slot in range(PEER_RING):
                if slot % SUBLANES == 0:
                    h_rows = h_ref[pl.ds(pl.multiple_of(base + slot, SUBLANES), SUBLANES), :]
                rows.append(token(base + slot, slot, h_rows, routing))
            _run(routing)
            po_ref[pl.ds(base, PEER_RING), :] = jnp.concatenate(rows, axis=0)
            publish_ids(b)
            return 0

        lax.fori_loop(0, n_blocks, block, 0)
        x2 = x1_ref[...] + g2_ref[0] * po_ref[...]
        o_ref[...] = _rms(x2, nf_ref[...])

    @pl.when(i == n_tiles)
    def _():
        for slot in range(PEER_AHEAD):
            wait_rows(slot)


def _peer_ffn(q, keys, h2, x1, gate2, nf, uv3, seq):
    t, d = h2.shape
    nhc, _, dh = q.shape
    tt = PEER_TOK_TILE
    n_tiles = t // tt
    per_b = seq // tt
    prev = lambda i: jnp.maximum(i - 1, 0)
    tokblk = pl.BlockSpec((tt, d), lambda i: (prev(i), 0))
    return pl.pallas_call(
        _peer_body,
        grid=(n_tiles + 1,),
        in_specs=[pl.BlockSpec((nhc, tt, dh), lambda i: (0, jnp.minimum(i, n_tiles - 1), 0)),
                  pl.BlockSpec(keys.shape, lambda i: (0, 0, 0)),
                  tokblk, tokblk,
                  pl.BlockSpec((1, 1, d), lambda i: (prev(i) // per_b, 0, 0)),
                  pl.BlockSpec((1, d), lambda i: (0, 0)),
                  pl.BlockSpec(memory_space=pl.ANY)],
        out_specs=tokblk,
        out_shape=jax.ShapeDtypeStruct((t, d), F32),
        scratch_shapes=[pltpu.VMEM((PEER_RING, PEER_SLOTS, SLAB_ROWS, LANES), F32),
                        pltpu.VMEM((tt, d), F32),
                        pltpu.VMEM((PEER_SLOTS, LANES), F32),
                        pltpu.VMEM((2, tt // LANES, PEER_SLOTS, LANES), F32),
                        pltpu.VMEM((PEER_SLOTS, LANES), I32),
                        pltpu.VMEM((LANES, PEER_SLOTS), I32),
                        pltpu.SMEM((2, tt, PEER_SLOTS), I32),
                        pltpu.SemaphoreType.DMA((PEER_RING,)),
                        pltpu.SemaphoreType.DMA(())],
        compiler_params=pltpu.CompilerParams(dimension_semantics=("arbitrary",),
                                             vmem_limit_bytes=VMEM_LIMIT),
        name="peer_ffn",
    )(q, keys, h2, x1, gate2, nf, uv3)


def kernel(x, c, w_ada, b_ada, norm1_g, w_in, conv_w, conv_b, dt_bias, a_log, d_ssd, norm_ssd_g, s5_a_re, s5_a_im, s5_log_dt, s5_b_re, s5_b_im, s5_c_re, s5_c_im, s5_d, glu_w, glu_b, norm_s5_g, w_out, norm2_g, w_query, sub_keys, expert_u, expert_v, norm_f_g):
    bsz, seq, d = x.shape
    t = bsz * seq
    depth = w_ada.shape[0]
    xt = x.reshape(t, d)
    for l in range(depth):
        mod = _adaln_mod(c, w_ada[l], b_ada[l])
        shift1, scale1, gate1, shift2, scale2, gate2 = [m.reshape(bsz, 1, d) for m in jnp.split(mod, 6, axis=-1)]

        wz, wxbc, wdt, wu = jnp.split(w_in[l], [SSD_WIDTH, SSD_WIDTH + 2048, SSD_WIDTH + 2048 + SSD_HEADS], axis=1)
        wu = _s5_cols_to_hg(wu, axis=1)
        w_cat = jnp.concatenate([wz, wxbc, jnp.pad(wdt, ((0, 0), (0, DT_PAD - SSD_HEADS))), wu], axis=1).astype(BF16)
        proj, u = _in_proj(xt, scale1, shift1, norm1_g[l].reshape(1, d), w_cat, seq)

        pad_h = lambda v: jnp.pad(v.astype(F32), (0, DT_PAD - SSD_HEADS)).reshape(1, DT_PAD)
        yssd = _ssd_mixer(proj, conv_w[l], conv_b[l].reshape(1, -1), pad_h(dt_bias[l]),
                          pad_h(-jnp.exp(a_log[l].astype(F32))),
                          jnp.repeat(d_ssd[l].astype(F32), SSD_HEAD_DIM).reshape(1, SSD_WIDTH),
                          norm_ssd_g[l].reshape(1, SSD_WIDTH), bsz, seq)

        blk5 = S5_CHUNK * S5_GROUP_CH
        nc5 = seq // S5_CHUNK
        xg = u.reshape(bsz, nc5, blk5, S5_GROUPS).transpose(3, 1, 0, 2).reshape(S5_GROUPS, nc5 * bsz, blk5)
        prep = _s5_prepare(s5_a_re[l], s5_a_im[l], s5_log_dt[l], s5_b_re[l], s5_b_im[l], s5_c_re[l],
                           s5_c_im[l], s5_d[l], glu_w[l], glu_b[l])
        og = _s5_mixer(xg, prep, bsz)
        s5o = og.reshape(S5_GROUPS, nc5, bsz, blk5).transpose(2, 1, 3, 0).reshape(t, S5_WIDTH)

        w_o = jnp.concatenate([w_out[l][:SSD_WIDTH], _s5_cols_to_hg(w_out[l][SSD_WIDTH:], axis=0)], axis=0)
        x1, h2, q = _out_proj(yssd, s5o, xt, gate1, scale2, shift2, _s5_cols_to_hg(norm_s5_g[l], axis=0).reshape(1, -1),
                              norm2_g[l].reshape(1, d), w_o.astype(BF16), w_query[l].astype(BF16), seq)

        keys = sub_keys[l].reshape(2 * PEER_HEADS, PEER_NKEYS, PEER_HALF).astype(BF16)
        n_exp = expert_u.shape[1]
        uv3 = jnp.concatenate([expert_u[l].reshape(n_exp, SUBLANES, LANES),
                               expert_v[l].reshape(n_exp, SUBLANES, LANES)], axis=1)
        last = l == depth - 1
        assert last, "the final RMSNorm is fused into the last layer's PEER kernel"
        xt = _peer_ffn(q, keys, h2, x1, gate2, norm_f_g.reshape(1, d), uv3, seq)
    return xt.reshape(bsz, seq, d)
```
